```python
import math
import jax, jax.numpy as jnp
from jax import lax
import numpy as np

D_MODEL = 2048
BATCH = 2
SEQ = 8192
DEPTH = 4

N_MIXERS = 2
N_A = (DEPTH + N_MIXERS - 1) // N_MIXERS
N_B = DEPTH // N_MIXERS
HEAD_DIM = 64
N_MAPS = D_MODEL // HEAD_DIM
DA_HEADS = N_MAPS // 2
DA_VDIM = 2 * HEAD_DIM
SW_Q_HEADS = N_MAPS
SW_KV_HEADS = 4
SW_WINDOW = 128
BLOCK = 128
NUM_BUCKETS = 32
MAX_DISTANCE = 128
D_FF = 5632
CONV_WIDTH = 3
EPS = 1e-6

kernel_name = "hybrid_diffattn_swa_sink_convffn"


def rms_norm(x, g):
    xf = x.astype(jnp.float32)
    y = xf * lax.rsqrt(jnp.mean(xf * xf, axis=-1, keepdims=True) + EPS)
    return (y * g.astype(jnp.float32)).astype(x.dtype)


def t5_bucket(dist):
    max_exact = NUM_BUCKETS // 2
    d = jnp.maximum(dist, 0)
    df = jnp.maximum(d, 1).astype(jnp.float32)
    large = max_exact + (jnp.log(df / max_exact) / math.log(MAX_DISTANCE / max_exact)
                         * (NUM_BUCKETS - max_exact)).astype(jnp.int32)
    large = jnp.minimum(large, NUM_BUCKETS - 1)
    return jnp.where(d < max_exact, d, large)


def diff_attention(h, w_qkv, w_o, q_gain, k_gain, lq1, lk1, lq2, lk2, sub_gain, rel_table, layer_idx):
    B, S, _ = h.shape
    qkv = h @ w_qkv
    q, k, v = jnp.split(qkv, [DA_HEADS * 2 * HEAD_DIM, 2 * DA_HEADS * 2 * HEAD_DIM], axis=-1)
    q = rms_norm(q.reshape(B, S, DA_HEADS, 2, HEAD_DIM), q_gain) * (HEAD_DIM ** -0.5)
    k = rms_norm(k.reshape(B, S, DA_HEADS, 2, HEAD_DIM), k_gain)
    v = v.reshape(B, S, DA_HEADS, DA_VDIM)
    lam_init = 0.8 - 0.6 * math.exp(-0.3 * layer_idx)
    f32 = jnp.float32
    lam = (jnp.exp(jnp.sum(lq1.astype(f32) * lk1.astype(f32)))
           - jnp.exp(jnp.sum(lq2.astype(f32) * lk2.astype(f32))) + lam_init)
    nb = S // BLOCK
    qb = q.reshape(B, nb, BLOCK, DA_HEADS, 2, HEAD_DIM).transpose(1, 0, 2, 3, 4, 5)
    key_pos = jnp.arange(S)

    def one_block(args):
        qblk, bi = args
        q_pos = bi * BLOCK + jnp.arange(BLOCK)
        dist = q_pos[:, None] - key_pos[None, :]
        bias = rel_table[t5_bucket(dist)].astype(f32)
        bias = bias.reshape(BLOCK, S, DA_HEADS, 2).transpose(2, 3, 0, 1)
        s = jnp.einsum('bqhmd,bkhmd->bhmqk', qblk, k).astype(f32) + bias
        s = jnp.where(dist >= 0, s, -jnp.inf)
        p = jax.nn.softmax(s, axis=-1)
        a = p[:, :, 0] - lam * p[:, :, 1]
        return jnp.einsum('bhqk,bkhe->bqhe', a.astype(v.dtype), v)

    o = lax.map(one_block, (qb, jnp.arange(nb)))
    o = o.transpose(1, 0, 2, 3, 4).reshape(B, S, DA_HEADS, DA_VDIM)
    o = rms_norm(o, sub_gain) * (1.0 - lam_init)
    return o.reshape(B, S, DA_HEADS * DA_VDIM) @ w_o


def swa_attention(h, w_qkv, w_o, q_gain, k_gain, sinks, rel_table):
    B, S, _ = h.shape
    G = SW_Q_HEADS // SW_KV_HEADS
    f32 = jnp.float32
    qkv = h @ w_qkv
    q, k, v = jnp.split(qkv, [SW_Q_HEADS * HEAD_DIM, (SW_Q_HEADS + SW_KV_HEADS) * HEAD_DIM], axis=-1)
    q = rms_norm(q.reshape(B, S, SW_KV_HEADS, G, HEAD_DIM), q_gain) * (HEAD_DIM ** -0.5)
    k = rms_norm(k.reshape(B, S, SW_KV_HEADS, HEAD_DIM), k_gain)
    v = v.reshape(B, S, SW_KV_HEADS, HEAD_DIM)
    nb = S // BLOCK
    qb = q.reshape(B, nb, BLOCK, SW_KV_HEADS, G, HEAD_DIM)

    def band(t):
        tp = jnp.pad(t, ((0, 0), (BLOCK, 0), (0, 0), (0, 0))).reshape(B, nb + 1, BLOCK, SW_KV_HEADS, HEAD_DIM)
        return jnp.concatenate([tp[:, :-1], tp[:, 1:]], axis=2)

    kb, vb = band(k), band(v)
    q_loc = jnp.arange(BLOCK) + BLOCK
    k_loc = jnp.arange(2 * BLOCK)
    dist = q_loc[:, None] - k_loc[None, :]
    in_window = (dist >= 0) & (dist < SW_WINDOW)
    key_ok = (jnp.arange(nb)[:, None] * BLOCK - BLOCK + k_loc[None, :]) >= 0
    mask = in_window[None] & key_ok[:, None, :]
    bias = rel_table[t5_bucket(dist)].astype(f32)
    bias = bias.reshape(BLOCK, 2 * BLOCK, SW_KV_HEADS, G).transpose(2, 3, 0, 1)
    s = jnp.einsum('bnqhgd,bnkhd->bnhgqk', qb, kb).astype(f32) + bias
    s = jnp.where(mask[None, :, None, None], s, -jnp.inf)
    sink = sinks.astype(f32).reshape(SW_KV_HEADS, G)[None, None, :, :, None, None]
    m = jnp.maximum(jnp.max(s, axis=-1, keepdims=True), sink)
    e = jnp.exp(s - m)
    p = e / (jnp.sum(e, axis=-1, keepdims=True) + jnp.exp(sink - m))
    o = jnp.einsum('bnhgqk,bnkhd->bnqhgd', p.astype(v.dtype), vb)
    return o.reshape(B, S, SW_Q_HEADS * HEAD_DIM) @ w_o


def conv_ffn(h, w_in, conv_w, conv_b, w_out):
    u = h @ w_in
    u = lax.conv_general_dilated(u, conv_w[:, None, :], window_strides=(1,),
                                 padding=[(CONV_WIDTH - 1, 0)],
                                 dimension_numbers=('NWC', 'WIO', 'NWC'),
                                 feature_group_count=2 * D_FF) + conv_b
    gate, up = jnp.split(u, 2, axis=-1)
    return (jax.nn.silu(gate) * up) @ w_out


def setup_inputs(seed: int = 0) -> dict:
    key = jax.random.key(seed)
    ks = jax.random.split(key, 23)

    def nrm(k, shape, scale):
        return jax.random.normal(k, shape, jnp.float32) * scale

    da_w = 2 * DA_HEADS * 2 * HEAD_DIM + DA_HEADS * DA_VDIM
    sw_w = (SW_Q_HEADS + 2 * SW_KV_HEADS) * HEAD_DIM
    return {
        "x": nrm(ks[0], (BATCH, SEQ, D_MODEL), 1.0),
        "rel_table": nrm(ks[1], (NUM_BUCKETS, N_MAPS), 0.5),
        "attn_norm_g": 1.0 + nrm(ks[2], (DEPTH, D_MODEL), 0.02),
        "ffn_norm_g": 1.0 + nrm(ks[3], (DEPTH, D_MODEL), 0.02),
        "da_w_qkv": nrm(ks[4], (N_A, D_MODEL, da_w), D_MODEL ** -0.5),
        "da_w_o": nrm(ks[5], (N_A, DA_HEADS * DA_VDIM, D_MODEL), (DA_HEADS * DA_VDIM) ** -0.5),
        "da_q_gain": 1.0 + nrm(ks[6], (N_A, HEAD_DIM), 0.02),
        "da_k_gain": 1.0 + nrm(ks[7], (N_A, HEAD_DIM), 0.02),
        "da_lam_q1": nrm(ks[8], (N_A, HEAD_DIM), 0.1),
        "da_lam_k1": nrm(ks[9], (N_A, HEAD_DIM), 0.1),
        "da_lam_q2": nrm(ks[10], (N_A, HEAD_DIM), 0.1),
        "da_lam_k2": nrm(ks[11], (N_A, HEAD_DIM), 0.1),
        "da_sub_gain": 1.0 + nrm(ks[12], (N_A, DA_VDIM), 0.02),
        "sw_w_qkv": nrm(ks[13], (N_B, D_MODEL, sw_w), D_MODEL ** -0.5),
        "sw_w_o": nrm(ks[14], (N_B, SW_Q_HEADS * HEAD_DIM, D_MODEL), (SW_Q_HEADS * HEAD_DIM) ** -0.5),
        "sw_q_gain": 1.0 + nrm(ks[15], (N_B, HEAD_DIM), 0.02),
        "sw_k_gain": 1.0 + nrm(ks[16], (N_B, HEAD_DIM), 0.02),
        "sw_sinks": nrm(ks[17], (N_B, SW_Q_HEADS), 1.0),
        "ffn_w_in": nrm(ks[18], (DEPTH, D_MODEL, 2 * D_FF), D_MODEL ** -0.5),
        "ffn_conv_w": nrm(ks[19], (DEPTH, CONV_WIDTH, 2 * D_FF), CONV_WIDTH ** -0.5),
        "ffn_conv_b": nrm(ks[20], (DEPTH, 2 * D_FF), 0.02),
        "ffn_w_out": nrm(ks[21], (DEPTH, D_FF, D_MODEL), D_FF ** -0.5),
    }


def reference(x, rel_table, attn_norm_g, ffn_norm_g, da_w_qkv, da_w_o, da_q_gain, da_k_gain,
              da_lam_q1, da_lam_k1, da_lam_q2, da_lam_k2, da_sub_gain, sw_w_qkv, sw_w_o,
              sw_q_gain, sw_k_gain, sw_sinks, ffn_w_in, ffn_conv_w, ffn_conv_b, ffn_w_out):
    h = x
    for i in range(DEPTH):
        a = rms_norm(h, attn_norm_g[i])
        j = i // N_MIXERS
        if i % N_MIXERS == 0:
            h = h + diff_attention(a, da_w_qkv[j], da_w_o[j], da_q_gain[j], da_k_gain[j],
                                   da_lam_q1[j], da_lam_k1[j], da_lam_q2[j], da_lam_k2[j],
                                   da_sub_gain[j], rel_table, i)
        else:
            h = h + swa_attention(a, sw_w_qkv[j], sw_w_o[j], sw_q_gain[j], sw_k_gain[j],
                                  sw_sinks[j], rel_table)
        f = rms_norm(h, ffn_norm_g[i])
        h = h + conv_ffn(f, ffn_w_in[i], ffn_conv_w[i], ffn_conv_b[i], ffn_w_out[i])
    return h
```

```python
import functools
import math

import numpy as np
import jax
import jax.numpy as jnp
from jax import lax
from jax.experimental import pallas as pl
from jax.experimental.pallas import tpu as pltpu

F32 = jnp.float32
BF16 = jnp.bfloat16

HEAD_DIM = 64
N_MAPS = 32
DA_HEADS = 16
DA_VDIM = 128
SW_Q_HEADS = 32
SW_KV_HEADS = 4
SW_GROUP = SW_Q_HEADS // SW_KV_HEADS
SW_WINDOW = 128
NUM_BUCKETS = 32
MAX_DISTANCE = 128
CONV_WIDTH = 3
EPS = 1e-6
N_MIXERS = 2

LANES = 128
NEG = -1e30
VMEM_LIMIT = 56 * 1024 * 1024

PROJ_TM = 1024
PROJ_TN = 512
OUT_TM = 1024
OUT_TN = 1024
FFN_TM = 512
FFN_TF = 512
DA_T = 256
SW_TQ = 256


def _cparams(*sem):
    return pltpu.CompilerParams(dimension_semantics=sem, vmem_limit_bytes=VMEM_LIMIT)


def _t5_bucket_np(dist):
    max_exact = NUM_BUCKETS // 2
    d = np.maximum(dist, 0)
    df = np.maximum(d, 1).astype(np.float64)
    large = max_exact + (np.log(df / max_exact) / math.log(MAX_DISTANCE / max_exact)
                         * (NUM_BUCKETS - max_exact)).astype(np.int32)
    large = np.minimum(large, NUM_BUCKETS - 1)
    return np.where(d < max_exact, d, large).astype(np.int32)


def _da_bias_kernel(tab_ref, bucket_ref, out_ref, *, t):
    h = pl.program_id(0)
    for off in range(2):
        bk = bucket_ref[off]
        for m in range(2):
            col = 2 * h + m
            far = tab_ref[NUM_BUCKETS - 1, col]
            acc = jnp.zeros(bk.shape, F32)
            for b in range(NUM_BUCKETS - 1):
                acc = jnp.where(bk == b, tab_ref[b, col] - far, acc)
            out_ref[0, off, :, m * t:(m + 1) * t] = jnp.where(bk < 0, NEG, acc)


def _da_bias(rel_table, t):
    key = np.arange(t)[:, None]
    qry = np.arange(t)[None, :]
    d0 = qry - key
    d1 = t + qry - key
    bucket = np.stack([np.where(d0 >= 0, _t5_bucket_np(d0), -1), _t5_bucket_np(d1)]).astype(np.int32)
    return pl.pallas_call(
        functools.partial(_da_bias_kernel, t=t),
        grid=(DA_HEADS,),
        in_specs=[pl.BlockSpec(memory_space=pltpu.SMEM),
                  pl.BlockSpec((2, t, t), lambda h: (0, 0, 0))],
        out_specs=pl.BlockSpec((1, 2, t, 2 * t), lambda h: (h, 0, 0, 0)),
        out_shape=jax.ShapeDtypeStruct((DA_HEADS, 2, t, 2 * t), F32),
        compiler_params=_cparams("arbitrary"),
        name="da_bias",
    )(rel_table, jnp.asarray(bucket))


def _sw_bias_kernel(tab_ref, bucket_ref, out_ref, *, tq):
    p = pl.program_id(0)
    bk = bucket_ref[...]
    for m in range(2):
        col = 2 * p + m
        acc = jnp.zeros(bk.shape, F32)
        for b in range(NUM_BUCKETS):
            acc = jnp.where(bk == b, tab_ref[b, col], acc)
        out_ref[0, :, m * tq:(m + 1) * tq] = jnp.where(bk < 0, NEG, acc)


def _sw_bias(rel_table, tq):
    band = SW_WINDOW + tq
    key = np.arange(band)[:, None]
    qry = np.arange(tq)[None, :]
    dist = qry + SW_WINDOW - key
    bucket = np.where((dist >= 0) & (dist < SW_WINDOW), _t5_bucket_np(dist), -1).astype(np.int32)
    return pl.pallas_call(
        functools.partial(_sw_bias_kernel, tq=tq),
        grid=(SW_Q_HEADS // 2,),
        in_specs=[pl.BlockSpec(memory_space=pltpu.SMEM),
                  pl.BlockSpec((band, tq), lambda p: (0, 0))],
        out_specs=pl.BlockSpec((1, band, 2 * tq), lambda p: (p, 0, 0)),
        out_shape=jax.ShapeDtypeStruct((SW_Q_HEADS // 2, band, 2 * tq), F32),
        compiler_params=_cparams("arbitrary"),
        name="sw_bias",
    )(rel_table, jnp.asarray(bucket))


def _rms_rows(x, g):
    ms = jnp.mean(x * x, axis=-1, keepdims=True)
    return x * lax.rsqrt(ms + EPS) * g


def _norm_proj_kernel(x_ref, g_ref, w_ref, flag_ref, gain_ref, o_ref, xn_ref):
    @pl.when(pl.program_id(1) == 0)
    def _():
        xn_ref[...] = _rms_rows(x_ref[...], g_ref[...]).astype(BF16)

    y = jnp.dot(xn_ref[...], w_ref[...], preferred_element_type=F32)
    lane = lax.broadcasted_iota(jnp.int32, (1, LANES), 1)
    low = lane < HEAD_DIM
    for c in range(y.shape[1] // LANES):
        sl = slice(c * LANES, (c + 1) * LANES)
        yc = y[:, sl]
        y2 = yc * yc
        ms_lo = jnp.sum(jnp.where(low, y2, 0.0), axis=-1, keepdims=True) * (1.0 / HEAD_DIM)
        ms_hi = jnp.sum(jnp.where(low, 0.0, y2), axis=-1, keepdims=True) * (1.0 / HEAD_DIM)
        inv = jnp.where(low, lax.rsqrt(ms_lo + EPS), lax.rsqrt(ms_hi + EPS))
        scale = jnp.where(flag_ref[:, sl] > 0.0, inv, 1.0) * gain_ref[:, sl]
        o_ref[:, sl] = (yc * scale).astype(BF16)


def _norm_proj(x, g, w, flag, gain):
    t, d = x.shape
    n = w.shape[1]
    tm, tn = PROJ_TM, PROJ_TN
    assert t % tm == 0 and n % tn == 0
    return pl.pallas_call(
        _norm_proj_kernel,
        grid=(t // tm, n // tn),
        in_specs=[pl.BlockSpec((tm, d), lambda i, j: (i, 0)),
                  pl.BlockSpec((1, d), lambda i, j: (0, 0)),
                  pl.BlockSpec((d, tn), lambda i, j: (0, j)),
                  pl.BlockSpec((1, tn), lambda i, j: (0, j)),
                  pl.BlockSpec((1, tn), lambda i, j: (0, j))],
        out_specs=pl.BlockSpec((tm, tn), lambda i, j: (i, j)),
        out_shape=jax.ShapeDtypeStruct((t, n), BF16),
        scratch_shapes=[pltpu.VMEM((tm, d), BF16)],
        compiler_params=_cparams("arbitrary", "arbitrary"),
        name="norm_proj",
    )(x, g, w, flag, gain)


def _out_proj_kernel(a_ref, w_ref, h_ref, o_ref):
    o_ref[...] = h_ref[...] + jnp.dot(a_ref[...], w_ref[...], preferred_element_type=F32)


def _out_proj(a, w, h):
    t, k = a.shape
    n = w.shape[1]
    tm, tn = OUT_TM, OUT_TN
    assert t % tm == 0 and n % tn == 0
    return pl.pallas_call(
        _out_proj_kernel,
        grid=(t // tm, n // tn),
        in_specs=[pl.BlockSpec((tm, k), lambda i, j: (i, 0)),
                  pl.BlockSpec((k, tn), lambda i, j: (0, j)),
                  pl.BlockSpec((tm, tn), lambda i, j: (i, j))],
        out_specs=pl.BlockSpec((tm, tn), lambda i, j: (i, j)),
        out_shape=jax.ShapeDtypeStruct((t, n), F32),
        compiler_params=_cparams("arbitrary", "arbitrary"),
        name="out_proj",
    )(a, w, h)


def _stack_maps(q):
    lane = lax.broadcasted_iota(jnp.int32, q.shape, 1)
    qf = q.astype(F32)
    return jnp.concatenate([jnp.where(lane < HEAD_DIM, qf, 0.0), jnp.where(lane < HEAD_DIM, 0.0, qf)],
                           axis=0).astype(BF16)


def _da_attn_kernel(q_ref, k_ref, vt_ref, bias_ref, lamv_ref, sg_ref, o_ref, m_ref, l_ref, acc_ref, *, t, lam_init):
    i = pl.program_id(2)
    qq = _stack_maps(q_ref[...])
    m_ref[...] = jnp.full(m_ref.shape, NEG, F32)
    l_ref[...] = jnp.zeros(l_ref.shape, F32)
    acc_ref[...] = jnp.zeros(acc_ref.shape, F32)

    def block(j, bias):
        k = k_ref[pl.ds(pl.multiple_of(j * t, t), t), :]
        st = lax.dot_general(k, qq, (((1,), (1,)), ((), ())), preferred_element_type=F32)
        if bias is not None:
            st = st + bias
        m_prev = m_ref[...]
        m_new = jnp.maximum(m_prev, jnp.max(st, axis=0, keepdims=True))
        alpha = jnp.exp(m_prev - m_new)
        p = jnp.exp(st - m_new)
        l_ref[...] = alpha * l_ref[...] + jnp.sum(p, axis=0, keepdims=True)
        pv = jnp.dot(vt_ref[j], p.astype(BF16), preferred_element_type=F32)
        acc_ref[...] = alpha * acc_ref[...] + pv
        m_ref[...] = m_new

    def far_block(j, carry):
        block(j, None)
        return carry

    lax.fori_loop(0, jnp.maximum(i - 1, 0), far_block, 0)

    @pl.when(i >= 1)
    def _():
        block(i - 1, bias_ref[0, 1])

    block(i, bias_ref[0, 0])

    lv = lamv_ref[...]
    lam = (jnp.exp(jnp.sum(lv[0:1] * lv[1:2], axis=-1, keepdims=True))
           - jnp.exp(jnp.sum(lv[2:3] * lv[3:4], axis=-1, keepdims=True)) + lam_init)
    acc = acc_ref[...] * (1.0 / l_ref[...])
    ot = acc[:, :t] - lam * acc[:, t:]
    ms = jnp.mean(ot * ot, axis=0, keepdims=True)
    ot = ot * lax.rsqrt(ms + EPS) * (sg_ref[...] * (1.0 - lam_init))
    o_ref[...] = ot.T.astype(BF16)


def _da_attn(qkv, vt, bias, lamv, sub_gain, batch, seq, lam_init):
    t = DA_T
    nq = seq // t
    return pl.pallas_call(
        functools.partial(_da_attn_kernel, t=t, lam_init=lam_init),
        grid=(batch, DA_HEADS, nq),
        in_specs=[pl.BlockSpec((t, LANES), lambda b, h, i: (b * nq + i, h)),
                  pl.BlockSpec((seq, LANES), lambda b, h, i: (b, DA_HEADS + h)),
                  pl.BlockSpec((None, None, nq, DA_VDIM, t), lambda b, h, i: (b, h, 0, 0, 0)),
                  pl.BlockSpec((1, 2, t, 2 * t), lambda b, h, i: (h, 0, 0, 0)),
                  pl.BlockSpec((4, HEAD_DIM), lambda b, h, i: (0, 0)),
                  pl.BlockSpec((DA_VDIM, 1), lambda b, h, i: (0, 0))],
        out_specs=pl.BlockSpec((t, LANES), lambda b, h, i: (b * nq + i, h)),
        out_shape=jax.ShapeDtypeStruct((batch * seq, DA_HEADS * DA_VDIM), BF16),
        scratch_shapes=[pltpu.VMEM((1, 2 * t), F32), pltpu.VMEM((1, 2 * t), F32),
                        pltpu.VMEM((DA_VDIM, 2 * t), F32)],
        compiler_params=_cparams("arbitrary", "arbitrary", "arbitrary"),
        name="da_attn",
    )(qkv, qkv, vt, bias, lamv, sub_gain)


def _sw_attn_kernel(sink_ref, q_ref, kp_ref, kc_ref, vp_ref, vc_ref, bias_ref, o_ref, *, tq):
    g = pl.program_id(0)
    i = pl.program_id(2)
    band = SW_WINDOW + tq
    kb = jnp.concatenate([kp_ref[...], kc_ref[...]], axis=0)
    vtb = jnp.concatenate([vp_ref[...], vc_ref[...]], axis=1)
    row = lax.broadcasted_iota(jnp.int32, (band, 1), 0)
    before_start = row < jnp.where(i == 0, SW_WINDOW, 0)
    col = lax.broadcasted_iota(jnp.int32, (1, 2 * tq), 1)
    for pp in range(SW_GROUP // 2):
        sl = slice(pp * LANES, (pp + 1) * LANES)
        head = g * SW_GROUP + 2 * pp
        sink = jnp.where(col < tq, sink_ref[head], sink_ref[head + 1])
        qq = _stack_maps(q_ref[:, sl])
        st = lax.dot_general(kb, qq, (((1,), (1,)), ((), ())), preferred_element_type=F32)
        st = jnp.where(before_start, NEG, st + bias_ref[pp])
        m = jnp.maximum(jnp.max(st, axis=0, keepdims=True), sink)
        e = jnp.exp(st - m)
        denom = jnp.sum(e, axis=0, keepdims=True) + jnp.exp(sink - m)
        ot = jnp.dot(vtb, e.astype(BF16), preferred_element_type=F32) * (1.0 / denom)
        o_ref[:, sl] = jnp.concatenate([ot[:, :tq], ot[:, tq:]], axis=0).T.astype(BF16)


def _sw_attn(q, kdup, vt, bias, sinks, batch, seq):
    tq = SW_TQ
    nq = seq // tq
    r = tq // SW_WINDOW
    band = SW_WINDOW + tq
    gw = SW_GROUP * HEAD_DIM
    return pl.pallas_call(
        functools.partial(_sw_attn_kernel, tq=tq),
        grid=(SW_KV_HEADS, batch, nq),
        in_specs=[pl.BlockSpec(memory_space=pltpu.SMEM),
                  pl.BlockSpec((tq, gw), lambda g, b, i: (b * nq + i, g)),
                  pl.BlockSpec((SW_WINDOW, LANES), lambda g, b, i: (jnp.maximum((b * nq + i) * r - 1, 0), g)),
                  pl.BlockSpec((tq, LANES), lambda g, b, i: (b * nq + i, g)),
                  pl.BlockSpec((None, None, HEAD_DIM, SW_WINDOW), lambda g, b, i: (b, g, 0, jnp.maximum(i * r - 1, 0))),
                  pl.BlockSpec((None, None, HEAD_DIM, tq), lambda g, b, i: (b, g, 0, i)),
                  pl.BlockSpec((SW_GROUP // 2, band, 2 * tq), lambda g, b, i: (g, 0, 0))],
        out_specs=pl.BlockSpec((tq, gw), lambda g, b, i: (b * nq + i, g)),
        out_shape=jax.ShapeDtypeStruct((batch * seq, SW_Q_HEADS * HEAD_DIM), BF16),
        compiler_params=_cparams("arbitrary", "arbitrary", "arbitrary"),
        name="sw_attn",
    )(sinks, q, kdup, kdup, vt, vt, bias)


def _causal_conv(u, halo, cw_ref, cb_ref):
    row = lax.broadcasted_iota(jnp.int32, (u.shape[0], 1), 0)
    u1 = jnp.where(row == 0, halo[7:8], pltpu.roll(u, 1, axis=0))
    u2 = jnp.where(row == 0, halo[6:7], jnp.where(row == 1, halo[7:8], pltpu.roll(u, 2, axis=0)))
    return cw_ref[0:1, :] * u2 + cw_ref[1:2, :] * u1 + cw_ref[2:3, :] * u + cb_ref[...]


def _ffn_kernel(h_ref, g_ref, wg_ref, wu_ref, cwg_ref, cwu_ref, cbg_ref, cbu_ref, wo_ref, o_ref,
                f_ref, halo_g_ref, halo_u_ref, *, tiles_per_seq):
    i = pl.program_id(0)
    j = pl.program_id(1)
    tm = h_ref.shape[0]

    @pl.when(j == 0)
    def _():
        x = h_ref[...]
        f_ref[...] = _rms_rows(x, g_ref[...]).astype(BF16)
        o_ref[...] = x

    f = f_ref[...]
    seq_start = (i % tiles_per_seq) == 0

    def branch(w_ref, cw_ref, cb_ref, halo_ref):
        u = jnp.dot(f, w_ref[...], preferred_element_type=F32)

        @pl.when(seq_start)
        def _():
            halo_ref[j] = jnp.zeros(halo_ref.shape[1:], F32)

        halo = halo_ref[j]
        halo_ref[j] = u[tm - 8:, :]
        return _causal_conv(u, halo, cw_ref, cb_ref)

    gate = branch(wg_ref, cwg_ref, cbg_ref, halo_g_ref)
    up = branch(wu_ref, cwu_ref, cbu_ref, halo_u_ref)
    act = (gate * (1.0 / (1.0 + jnp.exp(-gate))) * up).astype(BF16)
    o_ref[...] += jnp.dot(act, wo_ref[...], preferred_element_type=F32)


def _ffn(h, g, w_in, conv_w, conv_b, w_out, seq):
    t, d = h.shape
    dff = w_out.shape[0]
    tm, tf = FFN_TM, FFN_TF
    assert t % tm == 0 and seq % tm == 0 and dff % tf == 0
    nj = dff // tf
    return pl.pallas_call(
        functools.partial(_ffn_kernel, tiles_per_seq=seq // tm),
        grid=(t // tm, nj),
        in_specs=[pl.BlockSpec((tm, d), lambda i, j: (i, 0)),
                  pl.BlockSpec((1, d), lambda i, j: (0, 0)),
                  pl.BlockSpec((d, tf), lambda i, j: (0, j)),
                  pl.BlockSpec((d, tf), lambda i, j: (0, nj + j)),
                  pl.BlockSpec((CONV_WIDTH, tf), lambda i, j: (0, j)),
                  pl.BlockSpec((CONV_WIDTH, tf), lambda i, j: (0, nj + j)),
                  pl.BlockSpec((1, tf), lambda i, j: (0, j)),
                  pl.BlockSpec((1, tf), lambda i, j: (0, nj + j)),
                  pl.BlockSpec((tf, d), lambda i, j: (j, 0))],
        out_specs=pl.BlockSpec((tm, d), lambda i, j: (i, 0)),
        out_shape=jax.ShapeDtypeStruct((t, d), F32),
        scratch_shapes=[pltpu.VMEM((tm, d), BF16),
                        pltpu.VMEM((nj, 8, tf), F32),
                        pltpu.VMEM((nj, 8, tf), F32)],
        compiler_params=_cparams("arbitrary", "arbitrary"),
        name="conv_ffn",
    )(h, g, w_in, w_in, conv_w, conv_w, conv_b, conv_b, w_out)


def _row(v):
    return v.reshape(1, -1).astype(F32)


def kernel(x, rel_table, attn_norm_g, ffn_norm_g, da_w_qkv, da_w_o, da_q_gain, da_k_gain, da_lam_q1, da_lam_k1, da_lam_q2, da_lam_k2, da_sub_gain, sw_w_qkv, sw_w_o, sw_q_gain, sw_k_gain, sw_sinks, ffn_w_in, ffn_conv_w, ffn_conv_b, ffn_w_out):
    batch, seq, d = x.shape
    depth = attn_norm_g.shape[0]
    t = batch * seq
    h = x.reshape(t, d)
    q_scale = HEAD_DIM ** -0.5

    da_bias = _da_bias(rel_table, DA_T)
    sw_bias = _sw_bias(rel_table, SW_TQ)

    da_qk = 2 * DA_HEADS * HEAD_DIM
    da_flag = jnp.concatenate([jnp.ones((1, 2 * da_qk), F32), jnp.zeros((1, DA_HEADS * DA_VDIM), F32)], axis=1)
    sw_qw = SW_Q_HEADS * HEAD_DIM
    sw_kw = SW_KV_HEADS * HEAD_DIM
    sw_flag = jnp.concatenate([jnp.ones((1, sw_qw + sw_kw), F32), jnp.zeros((1, sw_kw), F32)], axis=1)

    for i in range(depth):
        j = i // N_MIXERS
        if i % N_MIXERS == 0:
            gain = jnp.concatenate([jnp.tile(da_q_gain[j], da_qk // HEAD_DIM) * q_scale,
                                    jnp.tile(da_k_gain[j], da_qk // HEAD_DIM),
                                    jnp.ones((DA_HEADS * DA_VDIM,), F32)])
            qkv = _norm_proj(h, _row(attn_norm_g[i]), da_w_qkv[j].astype(BF16), da_flag, _row(gain))
            nk = seq // DA_T
            vt = qkv[:, 2 * da_qk:].reshape(batch, nk, DA_T, DA_HEADS, DA_VDIM).transpose(0, 3, 1, 4, 2)
            lamv = jnp.stack([da_lam_q1[j], da_lam_k1[j], da_lam_q2[j], da_lam_k2[j]]).astype(F32)
            lam_init = 0.8 - 0.6 * math.exp(-0.3 * i)
            a = _da_attn(qkv, vt, da_bias, lamv, da_sub_gain[j].reshape(DA_VDIM, 1).astype(F32), batch, seq, lam_init)
            h = _out_proj(a, da_w_o[j].astype(BF16), h)
        else:
            gain = jnp.concatenate([jnp.tile(sw_q_gain[j], SW_Q_HEADS) * q_scale,
                                    jnp.tile(sw_k_gain[j], SW_KV_HEADS),
                                    jnp.ones((sw_kw,), F32)])
            qkv = _norm_proj(h, _row(attn_norm_g[i]), sw_w_qkv[j].astype(BF16), sw_flag, _row(gain))
            k = qkv[:, sw_qw:sw_qw + sw_kw].reshape(t, SW_KV_HEADS, 1, HEAD_DIM)
            kdup = jnp.broadcast_to(k, (t, SW_KV_HEADS, 2, HEAD_DIM)).reshape(t, SW_KV_HEADS * LANES)
            vt = qkv[:, sw_qw + sw_kw:].reshape(batch, seq, SW_KV_HEADS, HEAD_DIM).transpose(0, 2, 3, 1)
            a = _sw_attn(qkv, kdup, vt, sw_bias, sw_sinks[j].astype(F32), batch, seq)
            h = _out_proj(a, sw_w_o[j].astype(BF16), h)
        h = _ffn(h, _row(ffn_norm_g[i]), ffn_w_in[i].astype(BF16), ffn_conv_w[i].astype(F32),
                 _row(ffn_conv_b[i]), ffn_w_out[i].astype(BF16), seq)
    return h.reshape(batch, seq, d)
```

```python
import functools
import math

import numpy as np
import jax
import jax.numpy as jnp
from jax import lax
from jax.experimental import pallas as pl
from jax.experimental.pallas import tpu as pltpu

F32 = jnp.float32
BF16 = jnp.bfloat16

HEAD_DIM = 64
N_MAPS = 32
DA_HEADS = 16
DA_VDIM = 128
SW_Q_HEADS = 32
SW_KV_HEADS = 4
SW_GROUP = SW_Q_HEADS // SW_KV_HEADS
SW_WINDOW = 128
NUM_BUCKETS = 32
MAX_DISTANCE = 128
CONV_WIDTH = 3
EPS = 1e-6
N_MIXERS = 2

LANES = 128
NEG = -1e30
LOG2E = math.log2(math.e)
VMEM_LIMIT = 56 * 1024 * 1024

PROJ_TM = 1024
PROJ_TN = 512
OUT_TM = 1024
OUT_TN = 1024
FFN_TM = 512
FFN_TF = 512
DA_T = 512
DA_VROWS = DA_VDIM + 16
DA_FAST_BOUND = 60.0
SW_TQ = 256


def _cparams(*sem):
    return pltpu.CompilerParams(dimension_semantics=sem, vmem_limit_bytes=VMEM_LIMIT)


def _t5_bucket_np(dist):
    max_exact = NUM_BUCKETS // 2
    d = np.maximum(dist, 0)
    df = np.maximum(d, 1).astype(np.float64)
    large = max_exact + (np.log(df / max_exact) / math.log(MAX_DISTANCE / max_exact)
                         * (NUM_BUCKETS - max_exact)).astype(np.int32)
    large = np.minimum(large, NUM_BUCKETS - 1)
    return np.where(d < max_exact, d, large).astype(np.int32)


def _da_bias_kernel(tab_ref, bucket_ref, out_ref, *, t):
    h = pl.program_id(0)
    for off in range(2):
        bk = bucket_ref[off]
        for m in range(2):
            col = 2 * h + m
            far = tab_ref[NUM_BUCKETS - 1, col]
            acc = jnp.zeros(bk.shape, F32)
            for b in range(NUM_BUCKETS - 1):
                acc = jnp.where(bk == b, (tab_ref[b, col] - far) * LOG2E, acc)
            out_ref[0, off, :, m * t:(m + 1) * t] = jnp.where(bk < 0, NEG, acc)


def _da_bias(rel_table, t):
    key = np.arange(t)[:, None]
    qry = np.arange(t)[None, :]
    d0 = qry - key
    d1 = t + qry - key
    bucket = np.stack([np.where(d0 >= 0, _t5_bucket_np(d0), -1), _t5_bucket_np(d1)]).astype(np.int32)
    return pl.pallas_call(
        functools.partial(_da_bias_kernel, t=t),
        grid=(DA_HEADS,),
        in_specs=[pl.BlockSpec(memory_space=pltpu.SMEM),
                  pl.BlockSpec((2, t, t), lambda h: (0, 0, 0))],
        out_specs=pl.BlockSpec((1, 2, t, 2 * t), lambda h: (h, 0, 0, 0)),
        out_shape=jax.ShapeDtypeStruct((DA_HEADS, 2, t, 2 * t), F32),
        compiler_params=_cparams("arbitrary"),
        name="da_bias",
    )(rel_table, jnp.asarray(bucket))


def _sw_bias_kernel(tab_ref, bucket_ref, out_ref, *, tq):
    p = pl.program_id(0)
    bk = bucket_ref[...]
    for m in range(2):
        col = 2 * p + m
        acc = jnp.zeros(bk.shape, F32)
        for b in range(NUM_BUCKETS):
            acc = jnp.where(bk == b, tab_ref[b, col], acc)
        out_ref[0, :, m * tq:(m + 1) * tq] = jnp.where(bk < 0, NEG, acc)


def _sw_bias(rel_table, tq):
    band = SW_WINDOW + tq
    key = np.arange(band)[:, None]
    qry = np.arange(tq)[None, :]
    dist = qry + SW_WINDOW - key
    bucket = np.where((dist >= 0) & (dist < SW_WINDOW), _t5_bucket_np(dist), -1).astype(np.int32)
    return pl.pallas_call(
        functools.partial(_sw_bias_kernel, tq=tq),
        grid=(SW_Q_HEADS // 2,),
        in_specs=[pl.BlockSpec(memory_space=pltpu.SMEM),
                  pl.BlockSpec((band, tq), lambda p: (0, 0))],
        out_specs=pl.BlockSpec((1, band, 2 * tq), lambda p: (p, 0, 0)),
        out_shape=jax.ShapeDtypeStruct((SW_Q_HEADS // 2, band, 2 * tq), F32),
        compiler_params=_cparams("arbitrary"),
        name="sw_bias",
    )(rel_table, jnp.asarray(bucket))


def _rms_rows(x, g):
    ms = jnp.mean(x * x, axis=-1, keepdims=True)
    return x * lax.rsqrt(ms + EPS) * g


def _norm_proj_kernel(x_ref, g_ref, w_ref, flag_ref, gain_ref, o_ref, xn_ref):
    @pl.when(pl.program_id(1) == 0)
    def _():
        xn_ref[...] = _rms_rows(x_ref[...], g_ref[...]).astype(BF16)

    y = jnp.dot(xn_ref[...], w_ref[...], preferred_element_type=F32)
    lane = lax.broadcasted_iota(jnp.int32, (1, LANES), 1)
    low = lane < HEAD_DIM
    for c in range(y.shape[1] // LANES):
        sl = slice(c * LANES, (c + 1) * LANES)
        yc = y[:, sl]
        y2 = yc * yc
        ms_lo = jnp.sum(jnp.where(low, y2, 0.0), axis=-1, keepdims=True) * (1.0 / HEAD_DIM)
        ms_hi = jnp.sum(jnp.where(low, 0.0, y2), axis=-1, keepdims=True) * (1.0 / HEAD_DIM)
        inv = jnp.where(low, lax.rsqrt(ms_lo + EPS), lax.rsqrt(ms_hi + EPS))
        scale = jnp.where(flag_ref[:, sl] > 0.0, inv, 1.0) * gain_ref[:, sl]
        o_ref[:, sl] = (yc * scale).astype(BF16)


def _norm_proj(x, g, w, flag, gain):
    t, d = x.shape
    n = w.shape[1]
    tm, tn = PROJ_TM, PROJ_TN
    assert t % tm == 0 and n % tn == 0
    return pl.pallas_call(
        _norm_proj_kernel,
        grid=(t // tm, n // tn),
        in_specs=[pl.BlockSpec((tm, d), lambda i, j: (i, 0)),
                  pl.BlockSpec((1, d), lambda i, j: (0, 0)),
                  pl.BlockSpec((d, tn), lambda i, j: (0, j)),
                  pl.BlockSpec((1, tn), lambda i, j: (0, j)),
                  pl.BlockSpec((1, tn), lambda i, j: (0, j))],
        out_specs=pl.BlockSpec((tm, tn), lambda i, j: (i, j)),
        out_shape=jax.ShapeDtypeStruct((t, n), BF16),
        scratch_shapes=[pltpu.VMEM((tm, d), BF16)],
        compiler_params=_cparams("arbitrary", "arbitrary"),
        name="norm_proj",
    )(x, g, w, flag, gain)


def _out_proj_kernel(a_ref, w_ref, h_ref, o_ref):
    o_ref[...] = h_ref[...] + jnp.dot(a_ref[...], w_ref[...], preferred_element_type=F32)


def _out_proj(a, w, h):
    t, k = a.shape
    n = w.shape[1]
    tm, tn = OUT_TM, OUT_TN
    assert t % tm == 0 and n % tn == 0
    return pl.pallas_call(
        _out_proj_kernel,
        grid=(t // tm, n // tn),
        in_specs=[pl.BlockSpec((tm, k), lambda i, j: (i, 0)),
                  pl.BlockSpec((k, tn), lambda i, j: (0, j)),
                  pl.BlockSpec((tm, tn), lambda i, j: (i, j))],
        out_specs=pl.BlockSpec((tm, tn), lambda i, j: (i, j)),
        out_shape=jax.ShapeDtypeStruct((t, n), F32),
        compiler_params=_cparams("arbitrary", "arbitrary"),
        name="out_proj",
    )(a, w, h)


def _stack_maps(q):
    lane = lax.broadcasted_iota(jnp.int32, q.shape, 1)
    qf = q.astype(F32)
    return jnp.concatenate([jnp.where(lane < HEAD_DIM, qf, 0.0), jnp.where(lane < HEAD_DIM, 0.0, qf)],
                           axis=0).astype(BF16)


def _da_attn_kernel(fast_ref, q_ref, k_ref, vt_ref, bias_ref, lamv_ref, sg_ref, o_ref, acc_ref, m_ref, *, t, lam_init):
    i = pl.program_id(2)
    qt = q_ref[...].astype(F32).T
    row = lax.broadcasted_iota(jnp.int32, qt.shape, 0)
    qqt = jnp.concatenate([jnp.where(row < HEAD_DIM, qt, 0.0), jnp.where(row < HEAD_DIM, 0.0, qt)],
                          axis=1).astype(BF16)
    acc_ref[...] = jnp.zeros(acc_ref.shape, F32)

    def scores(u, bias):
        k = k_ref[pl.ds(pl.multiple_of(u * t, t), t), :]
        st = jnp.dot(k, qqt, preferred_element_type=F32)
        return st if bias is None else st + bias

    def chunk(u0, biases):
        pv = None
        for n, bias in enumerate(biases):
            p = jnp.exp2(scores(u0 + n, bias)).astype(BF16)
            d = jnp.dot(vt_ref[u0 + n], p, preferred_element_type=F32)
            pv = d if pv is None else pv + d
        acc_ref[...] += pv

    def unit_running_max(u, bias):
        st = scores(u, bias)
        m_prev = m_ref[...]
        m_new = jnp.maximum(m_prev, jnp.max(st, axis=0, keepdims=True))
        p = jnp.exp2(st - m_new).astype(BF16)
        acc_ref[...] = jnp.exp2(m_prev - m_new) * acc_ref[...] + jnp.dot(vt_ref[u], p, preferred_element_type=F32)
        m_ref[...] = m_new

    fast = fast_ref[0] == 1
    odd = (i % 2) == 1

    @pl.when(fast)
    def _():
        def far_pair(c, carry):
            chunk(2 * c, (None, None))
            return carry

        lax.fori_loop(0, jnp.maximum(i - 1, 0) // 2, far_pair, 0)

    @pl.when(fast & (i == 0))
    def _():
        chunk(0, (bias_ref[0, 0],))

    @pl.when(fast & odd)
    def _():
        chunk(i - 1, (bias_ref[0, 1], bias_ref[0, 0]))

    @pl.when(fast & jnp.logical_not(odd) & (i > 0))
    def _():
        chunk(i - 2, (None, bias_ref[0, 1], bias_ref[0, 0]))

    @pl.when(jnp.logical_not(fast))
    def _():
        m_ref[...] = jnp.full(m_ref.shape, NEG, F32)

        def far_unit(u, carry):
            unit_running_max(u, None)
            return carry

        lax.fori_loop(0, jnp.maximum(i - 1, 0), far_unit, 0)

        @pl.when(i >= 1)
        def _():
            unit_running_max(i - 1, bias_ref[0, 1])

        unit_running_max(i, bias_ref[0, 0])

    lv = lamv_ref[...]
    lam = (jnp.exp(jnp.sum(lv[0:1] * lv[1:2], axis=-1, keepdims=True))
           - jnp.exp(jnp.sum(lv[2:3] * lv[3:4], axis=-1, keepdims=True)) + lam_init)
    acc = acc_ref[...]
    o = acc[:DA_VDIM] * (1.0 / acc[DA_VDIM:DA_VDIM + 1])
    ot = o[:, :t] - lam * o[:, t:]
    ms = jnp.mean(ot * ot, axis=0, keepdims=True)
    ot = ot * lax.rsqrt(ms + EPS) * (sg_ref[...] * (1.0 - lam_init))
    o_ref[...] = ot.T.astype(BF16)


def _da_attn(fast, qkv, vt, bias, lamv, sub_gain, batch, seq, lam_init):
    t = DA_T
    nq = seq // t
    return pl.pallas_call(
        functools.partial(_da_attn_kernel, t=t, lam_init=lam_init),
        grid=(batch, DA_HEADS, nq),
        in_specs=[pl.BlockSpec(memory_space=pltpu.SMEM),
                  pl.BlockSpec((t, LANES), lambda b, h, i: (b * nq + i, h)),
                  pl.BlockSpec((seq, LANES), lambda b, h, i: (b, DA_HEADS + h)),
                  pl.BlockSpec((None, None, nq, DA_VROWS, t), lambda b, h, i: (b, h, 0, 0, 0)),
                  pl.BlockSpec((1, 2, t, 2 * t), lambda b, h, i: (h, 0, 0, 0)),
                  pl.BlockSpec((4, HEAD_DIM), lambda b, h, i: (0, 0)),
                  pl.BlockSpec((DA_VDIM, 1), lambda b, h, i: (0, 0))],
        out_specs=pl.BlockSpec((t, LANES), lambda b, h, i: (b * nq + i, h)),
        out_shape=jax.ShapeDtypeStruct((batch * seq, DA_HEADS * DA_VDIM), BF16),
        scratch_shapes=[pltpu.VMEM((DA_VROWS, 2 * t), F32), pltpu.VMEM((1, 2 * t), F32)],
        compiler_params=_cparams("arbitrary", "arbitrary", "arbitrary"),
        name="da_attn",
    )(fast, qkv, qkv, vt, bias, lamv, sub_gain)


def _sw_attn_kernel(sink_ref, q_ref, kp_ref, kc_ref, vp_ref, vc_ref, bias_ref, o_ref, *, tq):
    g = pl.program_id(0)
    i = pl.program_id(2)
    band = SW_WINDOW + tq
    kb = jnp.concatenate([kp_ref[...], kc_ref[...]], axis=0)
    vtb = jnp.concatenate([vp_ref[...], vc_ref[...]], axis=1)
    row = lax.broadcasted_iota(jnp.int32, (band, 1), 0)
    before_start = row < jnp.where(i == 0, SW_WINDOW, 0)
    col = lax.broadcasted_iota(jnp.int32, (1, 2 * tq), 1)
    for pp in range(SW_GROUP // 2):
        sl = slice(pp * LANES, (pp + 1) * LANES)
        head = g * SW_GROUP + 2 * pp
        sink = jnp.where(col < tq, sink_ref[head], sink_ref[head + 1])
        qq = _stack_maps(q_ref[:, sl])
        st = lax.dot_general(kb, qq, (((1,), (1,)), ((), ())), preferred_element_type=F32)
        st = jnp.where(before_start, NEG, st + bias_ref[pp])
        m = jnp.maximum(jnp.max(st, axis=0, keepdims=True), sink)
        e = jnp.exp(st - m)
        denom = jnp.sum(e, axis=0, keepdims=True) + jnp.exp(sink - m)
        ot = jnp.dot(vtb, e.astype(BF16), preferred_element_type=F32) * (1.0 / denom)
        o_ref[:, sl] = jnp.concatenate([ot[:, :tq], ot[:, tq:]], axis=0).T.astype(BF16)


def _sw_attn(q, kdup, vt, bias, sinks, batch, seq):
    tq = SW_TQ
    nq = seq // tq
    r = tq // SW_WINDOW
    band = SW_WINDOW + tq
    gw = SW_GROUP * HEAD_DIM
    return pl.pallas_call(
        functools.partial(_sw_attn_kernel, tq=tq),
        grid=(SW_KV_HEADS, batch, nq),
        in_specs=[pl.BlockSpec(memory_space=pltpu.SMEM),
                  pl.BlockSpec((tq, gw), lambda g, b, i: (b * nq + i, g)),
                  pl.BlockSpec((SW_WINDOW, LANES), lambda g, b, i: (jnp.maximum((b * nq + i) * r - 1, 0), g)),
                  pl.BlockSpec((tq, LANES), lambda g, b, i: (b * nq + i, g)),
                  pl.BlockSpec((None, None, HEAD_DIM, SW_WINDOW), lambda g, b, i: (b, g, 0, jnp.maximum(i * r - 1, 0))),
                  pl.BlockSpec((None, None, HEAD_DIM, tq), lambda g, b, i: (b, g, 0, i)),
                  pl.BlockSpec((SW_GROUP // 2, band, 2 * tq), lambda g, b, i: (g, 0, 0))],
        out_specs=pl.BlockSpec((tq, gw), lambda g, b, i: (b * nq + i, g)),
        out_shape=jax.ShapeDtypeStruct((batch * seq, SW_Q_HEADS * HEAD_DIM), BF16),
        compiler_params=_cparams("arbitrary", "arbitrary", "arbitrary"),
        name="sw_attn",
    )(sinks, q, kdup, kdup, vt, vt, bias)


def _causal_conv(u, halo, cw_ref, cb_ref):
    row = lax.broadcasted_iota(jnp.int32, (u.shape[0], 1), 0)
    u1 = jnp.where(row == 0, halo[7:8], pltpu.roll(u, 1, axis=0))
    u2 = jnp.where(row == 0, halo[6:7], jnp.where(row == 1, halo[7:8], pltpu.roll(u, 2, axis=0)))
    return cw_ref[0:1, :] * u2 + cw_ref[1:2, :] * u1 + cw_ref[2:3, :] * u + cb_ref[...]


def _ffn_kernel(h_ref, g_ref, wg_ref, wu_ref, cwg_ref, cwu_ref, cbg_ref, cbu_ref, wo_ref, o_ref,
                f_ref, halo_g_ref, halo_u_ref, *, tiles_per_seq):
    i = pl.program_id(0)
    j = pl.program_id(1)
    tm = h_ref.shape[0]

    @pl.when(j == 0)
    def _():
        x = h_ref[...]
        f_ref[...] = _rms_rows(x, g_ref[...]).astype(BF16)
        o_ref[...] = x

    f = f_ref[...]
    seq_start = (i % tiles_per_seq) == 0

    def branch(w_ref, cw_ref, cb_ref, halo_ref):
        u = jnp.dot(f, w_ref[...], preferred_element_type=F32)

        @pl.when(seq_start)
        def _():
            halo_ref[j] = jnp.zeros(halo_ref.shape[1:], F32)

        halo = halo_ref[j]
        halo_ref[j] = u[tm - 8:, :]
        return _causal_conv(u, halo, cw_ref, cb_ref)

    gate = branch(wg_ref, cwg_ref, cbg_ref, halo_g_ref)
    up = branch(wu_ref, cwu_ref, cbu_ref, halo_u_ref)
    act = (gate * (1.0 / (1.0 + jnp.exp(-gate))) * up).astype(BF16)
    o_ref[...] += jnp.dot(act, wo_ref[...], preferred_element_type=F32)


def _ffn(h, g, w_in, conv_w, conv_b, w_out, seq):
    t, d = h.shape
    dff = w_out.shape[0]
    tm, tf = FFN_TM, FFN_TF
    assert t % tm == 0 and seq % tm == 0 and dff % tf == 0
    nj = dff // tf
    return pl.pallas_call(
        functools.partial(_ffn_kernel, tiles_per_seq=seq // tm),
        grid=(t // tm, nj),
        in_specs=[pl.BlockSpec((tm, d), lambda i, j: (i, 0)),
                  pl.BlockSpec((1, d), lambda i, j: (0, 0)),
                  pl.BlockSpec((d, tf), lambda i, j: (0, j)),
                  pl.BlockSpec((d, tf), lambda i, j: (0, nj + j)),
                  pl.BlockSpec((CONV_WIDTH, tf), lambda i, j: (0, j)),
                  pl.BlockSpec((CONV_WIDTH, tf), lambda i, j: (0, nj + j)),
                  pl.BlockSpec((1, tf), lambda i, j: (0, j)),
                  pl.BlockSpec((1, tf), lambda i, j: (0, nj + j)),
                  pl.BlockSpec((tf, d), lambda i, j: (j, 0))],
        out_specs=pl.BlockSpec((tm, d), lambda i, j: (i, 0)),
        out_shape=jax.ShapeDtypeStruct((t, d), F32),
        scratch_shapes=[pltpu.VMEM((tm, d), BF16),
                        pltpu.VMEM((nj, 8, tf), F32),
                        pltpu.VMEM((nj, 8, tf), F32)],
        compiler_params=_cparams("arbitrary", "arbitrary"),
        name="conv_ffn",
    )(h, g, w_in, w_in, conv_w, conv_w, conv_b, conv_b, w_out)


def _row(v):
    return v.reshape(1, -1).astype(F32)


def kernel(x, rel_table, attn_norm_g, ffn_norm_g, da_w_qkv, da_w_o, da_q_gain, da_k_gain, da_lam_q1, da_lam_k1, da_lam_q2, da_lam_k2, da_sub_gain, sw_w_qkv, sw_w_o, sw_q_gain, sw_k_gain, sw_sinks, ffn_w_in, ffn_conv_w, ffn_conv_b, ffn_w_out):
    batch, seq, d = x.shape
    depth = attn_norm_g.shape[0]
    t = batch * seq
    h = x.reshape(t, d)
    q_scale = HEAD_DIM ** -0.5

    da_bias = _da_bias(rel_table, DA_T)
    sw_bias = _sw_bias(rel_table, SW_TQ)

    da_qk = 2 * DA_HEADS * HEAD_DIM
    da_flag = jnp.concatenate([jnp.ones((1, 2 * da_qk), F32), jnp.zeros((1, DA_HEADS * DA_VDIM), F32)], axis=1)
    sw_qw = SW_Q_HEADS * HEAD_DIM
    sw_kw = SW_KV_HEADS * HEAD_DIM
    sw_flag = jnp.concatenate([jnp.ones((1, sw_qw + sw_kw), F32), jnp.zeros((1, sw_kw), F32)], axis=1)

    for i in range(depth):
        j = i // N_MIXERS
        if i % N_MIXERS == 0:
            gain = jnp.concatenate([jnp.tile(da_q_gain[j], da_qk // HEAD_DIM) * (q_scale * LOG2E),
                                    jnp.tile(da_k_gain[j], da_qk // HEAD_DIM),
                                    jnp.ones((DA_HEADS * DA_VDIM,), F32)])
            qkv = _norm_proj(h, _row(attn_norm_g[i]), da_w_qkv[j].astype(BF16), da_flag, _row(gain))
            nk = seq // DA_T
            vt = qkv[:, 2 * da_qk:].reshape(batch, nk, DA_T, DA_HEADS, DA_VDIM).transpose(0, 3, 1, 4, 2)
            ones_rows = jnp.zeros((DA_VROWS - DA_VDIM, DA_T), BF16).at[0].set(1.0)
            vt = jnp.concatenate([vt, jnp.broadcast_to(ones_rows, vt.shape[:3] + ones_rows.shape)], axis=3)
            score_bound = (HEAD_DIM * q_scale * jnp.max(jnp.abs(da_q_gain[j])) * jnp.max(jnp.abs(da_k_gain[j]))
                           + jnp.max(jnp.abs(rel_table - rel_table[NUM_BUCKETS - 1]))) * LOG2E
            fast = (score_bound <= DA_FAST_BOUND).astype(jnp.int32).reshape(1)
            lamv = jnp.stack([da_lam_q1[j], da_lam_k1[j], da_lam_q2[j], da_lam_k2[j]]).astype(F32)
            lam_init = 0.8 - 0.6 * math.exp(-0.3 * i)
            a = _da_attn(fast, qkv, vt, da_bias, lamv, da_sub_gain[j].reshape(DA_VDIM, 1).astype(F32), batch, seq,
                         lam_init)
            h = _out_proj(a, da_w_o[j].astype(BF16), h)
        else:
            gain = jnp.concatenate([jnp.tile(sw_q_gain[j], SW_Q_HEADS) * q_scale,
                                    jnp.tile(sw_k_gain[j], SW_KV_HEADS),
                                    jnp.ones((sw_kw,), F32)])
            qkv = _norm_proj(h, _row(attn_norm_g[i]), sw_w_qkv[j].astype(BF16), sw_flag, _row(gain))
            k = qkv[:, sw_qw:sw_qw + sw_kw].reshape(t, SW_KV_HEADS, 1, HEAD_DIM)
            kdup = jnp.broadcast_to(k, (t, SW_KV_HEADS, 2, HEAD_DIM)).reshape(t, SW_KV_HEADS * LANES)
            vt = qkv[:, sw_qw + sw_kw:].reshape(batch, seq, SW_KV_HEADS, HEAD_DIM).transpose(0, 2, 3, 1)
            a = _sw_attn(qkv, kdup, vt, sw_bias, sw_sinks[j].astype(F32), batch, seq)
            h = _out_proj(a, sw_w_o[j].astype(BF16), h)
        h = _ffn(h, _row(ffn_norm_g[i]), ffn_w_in[i].astype(BF16), ffn_conv_w[i].astype(F32),
                 _row(ffn_conv_b[i]), ffn_w_out[i].astype(BF16), seq)
    return h.reshape(batch, seq, d)
```

```python
import functools
import math

import numpy as np
import jax
import jax.numpy as jnp
from jax import lax
from jax.experimental import pallas as pl
from jax.experimental.pallas import tpu as pltpu

F32 = jnp.float32
BF16 = jnp.bfloat16

HEAD_DIM = 64
N_MAPS = 32
DA_HEADS = 16
DA_VDIM = 128
SW_Q_HEADS = 32
SW_KV_HEADS = 4
SW_GROUP = SW_Q_HEADS // SW_KV_HEADS
SW_WINDOW = 128
NUM_BUCKETS = 32
MAX_DISTANCE = 128
CONV_WIDTH = 3
EPS = 1e-6
N_MIXERS = 2

LANES = 128
NEG = -1e30
LOG2E = math.log2(math.e)
VMEM_LIMIT = 56 * 1024 * 1024

PROJ_TM = 1024
PROJ_TN = 512
PROJ_ROWS = 256
OUT_TM = 1024
OUT_TN = 1024
FFN_TM = 512
FFN_TF = 512
FFN_ROWS = 256
FFN_GATE_ROWS = 32
DA_T = 512
DA_VROWS = DA_VDIM + 16
DA_FAST_BOUND = 60.0
SW_TQ = 256


def _cparams(*sem):
    return pltpu.CompilerParams(dimension_semantics=sem, vmem_limit_bytes=VMEM_LIMIT)


def _t5_bucket_np(dist):
    max_exact = NUM_BUCKETS // 2
    d = np.maximum(dist, 0)
    df = np.maximum(d, 1).astype(np.float64)
    large = max_exact + (np.log(df / max_exact) / math.log(MAX_DISTANCE / max_exact)
                         * (NUM_BUCKETS - max_exact)).astype(np.int32)
    large = np.minimum(large, NUM_BUCKETS - 1)
    return np.where(d < max_exact, d, large).astype(np.int32)


def _da_bias_kernel(tab_ref, bucket_ref, out_ref, *, t):
    h = pl.program_id(0)
    for off in range(2):
        bk = bucket_ref[off]
        for m in range(2):
            col = 2 * h + m
            far = tab_ref[NUM_BUCKETS - 1, col]
            acc = jnp.zeros(bk.shape, F32)
            for b in range(NUM_BUCKETS - 1):
                acc = jnp.where(bk == b, (tab_ref[b, col] - far) * LOG2E, acc)
            out_ref[0, off, :, m * t:(m + 1) * t] = jnp.where(bk < 0, NEG, acc)


def _da_bias(rel_table, t):
    key = np.arange(t)[:, None]
    qry = np.arange(t)[None, :]
    d0 = qry - key
    d1 = t + qry - key
    bucket = np.stack([np.where(d0 >= 0, _t5_bucket_np(d0), -1), _t5_bucket_np(d1)]).astype(np.int32)
    return pl.pallas_call(
        functools.partial(_da_bias_kernel, t=t),
        grid=(DA_HEADS,),
        in_specs=[pl.BlockSpec(memory_space=pltpu.SMEM),
                  pl.BlockSpec((2, t, t), lambda h: (0, 0, 0))],
        out_specs=pl.BlockSpec((1, 2, t, 2 * t), lambda h: (h, 0, 0, 0)),
        out_shape=jax.ShapeDtypeStruct((DA_HEADS, 2, t, 2 * t), F32),
        compiler_params=_cparams("arbitrary"),
        name="da_bias",
    )(rel_table, jnp.asarray(bucket))


def _sw_bias_kernel(tab_ref, bucket_ref, out_ref, *, tq):
    p = pl.program_id(0)
    bk = bucket_ref[...]
    for m in range(2):
        col = 2 * p + m
        acc = jnp.zeros(bk.shape, F32)
        for b in range(NUM_BUCKETS):
            acc = jnp.where(bk == b, tab_ref[b, col], acc)
        out_ref[0, :, m * tq:(m + 1) * tq] = jnp.where(bk < 0, NEG, acc)


def _sw_bias(rel_table, tq):
    band = SW_WINDOW + tq
    key = np.arange(band)[:, None]
    qry = np.arange(tq)[None, :]
    dist = qry + SW_WINDOW - key
    bucket = np.where((dist >= 0) & (dist < SW_WINDOW), _t5_bucket_np(dist), -1).astype(np.int32)
    return pl.pallas_call(
        functools.partial(_sw_bias_kernel, tq=tq),
        grid=(SW_Q_HEADS // 2,),
        in_specs=[pl.BlockSpec(memory_space=pltpu.SMEM),
                  pl.BlockSpec((band, tq), lambda p: (0, 0))],
        out_specs=pl.BlockSpec((1, band, 2 * tq), lambda p: (p, 0, 0)),
        out_shape=jax.ShapeDtypeStruct((SW_Q_HEADS // 2, band, 2 * tq), F32),
        compiler_params=_cparams("arbitrary"),
        name="sw_bias",
    )(rel_table, jnp.asarray(bucket))


def _rms_rows(x, g):
    ms = jnp.mean(x * x, axis=-1, keepdims=True)
    return x * lax.rsqrt(ms + EPS) * g


def _norm_proj_kernel(x_ref, g_ref, w_ref, flag_ref, gain_ref, o_ref, xn_ref):
    @pl.when(pl.program_id(1) == 0)
    def _():
        xn_ref[...] = _rms_rows(x_ref[...], g_ref[...]).astype(BF16)

    y = jnp.dot(xn_ref[...], w_ref[...], preferred_element_type=F32)
    lane = lax.broadcasted_iota(jnp.int32, (1, LANES), 1)
    low = lane < HEAD_DIM
    for c in range(y.shape[1] // LANES):
        sl = slice(c * LANES, (c + 1) * LANES)
        yc = y[:, sl]
        y2 = yc * yc
        ms_lo = jnp.sum(jnp.where(low, y2, 0.0), axis=-1, keepdims=True) * (1.0 / HEAD_DIM)
        ms_hi = jnp.sum(jnp.where(low, 0.0, y2), axis=-1, keepdims=True) * (1.0 / HEAD_DIM)
        inv = jnp.where(low, lax.rsqrt(ms_lo + EPS), lax.rsqrt(ms_hi + EPS))
        scale = jnp.where(flag_ref[:, sl] > 0.0, inv, 1.0) * gain_ref[:, sl]
        o_ref[:, sl] = (yc * scale).astype(BF16)


def _norm_proj(x, g, w, flag, gain):
    t, d = x.shape
    n = w.shape[1]
    tm, tn = PROJ_TM, PROJ_TN
    assert t % tm == 0 and n % tn == 0
    return pl.pallas_call(
        _norm_proj_kernel,
        grid=(t // tm, n // tn),
        in_specs=[pl.BlockSpec((tm, d), lambda i, j: (i, 0)),
                  pl.BlockSpec((1, d), lambda i, j: (0, 0)),
                  pl.BlockSpec((d, tn), lambda i, j: (0, j)),
                  pl.BlockSpec((1, tn), lambda i, j: (0, j)),
                  pl.BlockSpec((1, tn), lambda i, j: (0, j))],
        out_specs=pl.BlockSpec((tm, tn), lambda i, j: (i, j)),
        out_shape=jax.ShapeDtypeStruct((t, n), BF16),
        scratch_shapes=[pltpu.VMEM((tm, d), BF16)],
        compiler_params=_cparams("arbitrary", "arbitrary"),
        name="norm_proj",
    )(x, g, w, flag, gain)


def _out_proj_kernel(a_ref, w_ref, h_ref, o_ref):
    o_ref[...] = h_ref[...] + jnp.dot(a_ref[...], w_ref[...], preferred_element_type=F32)


def _out_proj(a, w, h):
    t, k = a.shape
    n = w.shape[1]
    tm, tn = OUT_TM, OUT_TN
    assert t % tm == 0 and n % tn == 0
    return pl.pallas_call(
        _out_proj_kernel,
        grid=(t // tm, n // tn),
        in_specs=[pl.BlockSpec((tm, k), lambda i, j: (i, 0)),
                  pl.BlockSpec((k, tn), lambda i, j: (0, j)),
                  pl.BlockSpec((tm, tn), lambda i, j: (i, j))],
        out_specs=pl.BlockSpec((tm, tn), lambda i, j: (i, j)),
        out_shape=jax.ShapeDtypeStruct((t, n), F32),
        compiler_params=_cparams("arbitrary", "arbitrary"),
        name="out_proj",
    )(a, w, h)


def _stack_maps(q):
    lane = lax.broadcasted_iota(jnp.int32, q.shape, 1)
    qf = q.astype(F32)
    return jnp.concatenate([jnp.where(lane < HEAD_DIM, qf, 0.0), jnp.where(lane < HEAD_DIM, 0.0, qf)],
                           axis=0).astype(BF16)


def _da_attn_kernel(fast_ref, q_ref, k_ref, vt_ref, bias_ref, lamv_ref, sg_ref, o_ref, acc_ref, m_ref, *, t, lam_init):
    i = pl.program_id(2)
    qt = q_ref[...].astype(F32).T
    row = lax.broadcasted_iota(jnp.int32, qt.shape, 0)
    qqt = jnp.concatenate([jnp.where(row < HEAD_DIM, qt, 0.0), jnp.where(row < HEAD_DIM, 0.0, qt)],
                          axis=1).astype(BF16)
    acc_ref[...] = jnp.zeros(acc_ref.shape, F32)

    def scores(u, bias):
        k = k_ref[pl.ds(pl.multiple_of(u * t, t), t), :]
        st = jnp.dot(k, qqt, preferred_element_type=F32)
        return st if bias is None else st + bias

    def chunk(u0, biases):
        pv = None
        for n, bias in enumerate(biases):
            p = jnp.exp2(scores(u0 + n, bias)).astype(BF16)
            d = jnp.dot(vt_ref[u0 + n], p, preferred_element_type=F32)
            pv = d if pv is None else pv + d
        acc_ref[...] += pv

    def unit_running_max(u, bias):
        st = scores(u, bias)
        m_prev = m_ref[...]
        m_new = jnp.maximum(m_prev, jnp.max(st, axis=0, keepdims=True))
        p = jnp.exp2(st - m_new).astype(BF16)
        acc_ref[...] = jnp.exp2(m_prev - m_new) * acc_ref[...] + jnp.dot(vt_ref[u], p, preferred_element_type=F32)
        m_ref[...] = m_new

    fast = fast_ref[0] == 1
    odd = (i % 2) == 1

    @pl.when(fast)
    def _():
        def far_pair(c, carry):
            chunk(2 * c, (None, None))
            return carry

        lax.fori_loop(0, jnp.maximum(i - 1, 0) // 2, far_pair, 0)

    @pl.when(fast & (i == 0))
    def _():
        chunk(0, (bias_ref[0, 0],))

    @pl.when(fast & odd)
    def _():
        chunk(i - 1, (bias_ref[0, 1], bias_ref[0, 0]))

    @pl.when(fast & jnp.logical_not(odd) & (i > 0))
    def _():
        chunk(i - 2, (None, bias_ref[0, 1], bias_ref[0, 0]))

    @pl.when(jnp.logical_not(fast))
    def _():
        m_ref[...] = jnp.full(m_ref.shape, NEG, F32)

        def far_unit(u, carry):
            unit_running_max(u, None)
            return carry

        lax.fori_loop(0, jnp.maximum(i - 1, 0), far_unit, 0)

        @pl.when(i >= 1)
        def _():
            unit_running_max(i - 1, bias_ref[0, 1])

        unit_running_max(i, bias_ref[0, 0])

    lv = lamv_ref[...]
    lam = (jnp.exp(jnp.sum(lv[0:1] * lv[1:2], axis=-1, keepdims=True))
           - jnp.exp(jnp.sum(lv[2:3] * lv[3:4], axis=-1, keepdims=True)) + lam_init)
    acc = acc_ref[...]
    o = acc[:DA_VDIM] * (1.0 / acc[DA_VDIM:DA_VDIM + 1])
    ot = o[:, :t] - lam * o[:, t:]
    ms = jnp.mean(ot * ot, axis=0, keepdims=True)
    ot = ot * lax.rsqrt(ms + EPS) * (sg_ref[...] * (1.0 - lam_init))
    o_ref[...] = ot.T.astype(BF16)


def _da_attn(fast, qkv, vt, bias, lamv, sub_gain, batch, seq, lam_init):
    t = DA_T
    nq = seq // t
    return pl.pallas_call(
        functools.partial(_da_attn_kernel, t=t, lam_init=lam_init),
        grid=(batch, DA_HEADS, nq),
        in_specs=[pl.BlockSpec(memory_space=pltpu.SMEM),
                  pl.BlockSpec((t, LANES), lambda b, h, i: (b * nq + i, h)),
                  pl.BlockSpec((seq, LANES), lambda b, h, i: (b, DA_HEADS + h)),
                  pl.BlockSpec((None, None, nq, DA_VROWS, t), lambda b, h, i: (b, h, 0, 0, 0)),
                  pl.BlockSpec((1, 2, t, 2 * t), lambda b, h, i: (h, 0, 0, 0)),
                  pl.BlockSpec((4, HEAD_DIM), lambda b, h, i: (0, 0)),
                  pl.BlockSpec((DA_VDIM, 1), lambda b, h, i: (0, 0))],
        out_specs=pl.BlockSpec((t, LANES), lambda b, h, i: (b * nq + i, h)),
        out_shape=jax.ShapeDtypeStruct((batch * seq, DA_HEADS * DA_VDIM), BF16),
        scratch_shapes=[pltpu.VMEM((DA_VROWS, 2 * t), F32), pltpu.VMEM((1, 2 * t), F32)],
        compiler_params=_cparams("arbitrary", "arbitrary", "arbitrary"),
        name="da_attn",
    )(fast, qkv, qkv, vt, bias, lamv, sub_gain)


def _sw_attn_kernel(sink_ref, q_ref, kp_ref, kc_ref, vp_ref, vc_ref, bias_ref, o_ref, *, tq):
    g = pl.program_id(0)
    i = pl.program_id(2)
    band = SW_WINDOW + tq
    kb = jnp.concatenate([kp_ref[...], kc_ref[...]], axis=0)
    vtb = jnp.concatenate([vp_ref[...], vc_ref[...]], axis=1)
    row = lax.broadcasted_iota(jnp.int32, (band, 1), 0)
    before_start = row < jnp.where(i == 0, SW_WINDOW, 0)
    col = lax.broadcasted_iota(jnp.int32, (1, 2 * tq), 1)
    for pp in range(SW_GROUP // 2):
        sl = slice(pp * LANES, (pp + 1) * LANES)
        head = g * SW_GROUP + 2 * pp
        sink = jnp.where(col < tq, sink_ref[head], sink_ref[head + 1])
        qq = _stack_maps(q_ref[:, sl])
        st = lax.dot_general(kb, qq, (((1,), (1,)), ((), ())), preferred_element_type=F32)
        st = jnp.where(before_start, NEG, st + bias_ref[pp])
        m = jnp.maximum(jnp.max(st, axis=0, keepdims=True), sink)
        e = jnp.exp(st - m)
        denom = jnp.sum(e, axis=0, keepdims=True) + jnp.exp(sink - m)
        ot = jnp.dot(vtb, e.astype(BF16), preferred_element_type=F32) * (1.0 / denom)
        o_ref[:, sl] = jnp.concatenate([ot[:, :tq], ot[:, tq:]], axis=0).T.astype(BF16)


def _sw_attn(q, kdup, vt, bias, sinks, batch, seq):
    tq = SW_TQ
    nq = seq // tq
    r = tq // SW_WINDOW
    band = SW_WINDOW + tq
    gw = SW_GROUP * HEAD_DIM
    return pl.pallas_call(
        functools.partial(_sw_attn_kernel, tq=tq),
        grid=(SW_KV_HEADS, batch, nq),
        in_specs=[pl.BlockSpec(memory_space=pltpu.SMEM),
                  pl.BlockSpec((tq, gw), lambda g, b, i: (b * nq + i, g)),
                  pl.BlockSpec((SW_WINDOW, LANES), lambda g, b, i: (jnp.maximum((b * nq + i) * r - 1, 0), g)),
                  pl.BlockSpec((tq, LANES), lambda g, b, i: (b * nq + i, g)),
                  pl.BlockSpec((None, None, HEAD_DIM, SW_WINDOW), lambda g, b, i: (b, g, 0, jnp.maximum(i * r - 1, 0))),
                  pl.BlockSpec((None, None, HEAD_DIM, tq), lambda g, b, i: (b, g, 0, i)),
                  pl.BlockSpec((SW_GROUP // 2, band, 2 * tq), lambda g, b, i: (g, 0, 0))],
        out_specs=pl.BlockSpec((tq, gw), lambda g, b, i: (b * nq + i, g)),
        out_shape=jax.ShapeDtypeStruct((batch * seq, SW_Q_HEADS * HEAD_DIM), BF16),
        compiler_params=_cparams("arbitrary", "arbitrary", "arbitrary"),
        name="sw_attn",
    )(sinks, q, kdup, kdup, vt, vt, bias)


def _causal_conv(u, halo, cw_ref, cb_ref):
    row = lax.broadcasted_iota(jnp.int32, (u.shape[0], 1), 0)
    u1 = jnp.where(row == 0, halo[7:8], pltpu.roll(u, 1, axis=0))
    u2 = jnp.where(row == 0, halo[6:7], jnp.where(row == 1, halo[7:8], pltpu.roll(u, 2, axis=0)))
    return cw_ref[0:1, :] * u2 + cw_ref[1:2, :] * u1 + cw_ref[2:3, :] * u + cb_ref[...]


def _ffn_kernel(h_ref, g_ref, wg_ref, wu_ref, cwg_ref, cwu_ref, cbg_ref, cbu_ref, wo_ref, o_ref,
                f_ref, ug0_ref, ug1_ref, uu0_ref, uu1_ref, act_ref, halo_g_ref, halo_u_ref, *, tiles_per_seq, nj):
    i = pl.program_id(0)
    j = pl.program_id(1)
    tm = h_ref.shape[0]

    @pl.when(j == 0)
    def _():
        x = h_ref[...]
        f_ref[...] = _rms_rows(x, g_ref[...]).astype(BF16)
        o_ref[...] = x

    @pl.when(((i % tiles_per_seq) == 0) & (j < nj))
    def _():
        halo_g_ref[j] = jnp.zeros(halo_g_ref.shape[1:], F32)
        halo_u_ref[j] = jnp.zeros(halo_u_ref.shape[1:], F32)

    jb = j - 1

    def up_and_gate(parity, do_gate, do_up):
        ug_cur, ug_prev = (ug0_ref, ug1_ref)[::1 - 2 * parity]
        uu_cur, uu_prev = (uu0_ref, uu1_ref)[::1 - 2 * parity]

        def conv(u_ref, cw_ref, cb_ref, rows, halo):
            u = u_ref[rows, :]
            return _causal_conv(u, halo, cw_ref, cb_ref), u[FFN_GATE_ROWS - 8:, :]

        if do_gate:
            halo_g, halo_u = halo_g_ref[jb], halo_u_ref[jb]
        for r in range(tm // FFN_ROWS):
            rows = slice(r * FFN_ROWS, (r + 1) * FFN_ROWS)
            if do_gate:
                for q in range(r * FFN_ROWS, (r + 1) * FFN_ROWS, FFN_GATE_ROWS):
                    piece = slice(q, q + FFN_GATE_ROWS)
                    gate, halo_g = conv(ug_prev, cwg_ref, cbg_ref, piece, halo_g)
                    up, halo_u = conv(uu_prev, cwu_ref, cbu_ref, piece, halo_u)
                    act_ref[piece, :] = (gate * (1.0 / (1.0 + jnp.exp(-gate))) * up).astype(BF16)
            if do_up:
                f = f_ref[rows, :]
                ug_cur[rows, :] = jnp.dot(f, wg_ref[...], preferred_element_type=F32)
                uu_cur[rows, :] = jnp.dot(f, wu_ref[...], preferred_element_type=F32)
        if do_gate:
            halo_g_ref[jb] = halo_g
            halo_u_ref[jb] = halo_u

    even = (j % 2) == 0
    inner = (j >= 1) & (j < nj)
    pl.when(j == 0)(functools.partial(up_and_gate, 0, False, True))
    pl.when(inner & even)(functools.partial(up_and_gate, 0, True, True))
    pl.when(inner & jnp.logical_not(even))(functools.partial(up_and_gate, 1, True, True))
    pl.when(j == nj)(functools.partial(up_and_gate, nj % 2, True, False))

    @pl.when(j >= 1)
    def _():
        o_ref[...] += jnp.dot(act_ref[...], wo_ref[...], preferred_element_type=F32)


def _ffn(h, g, w_in, conv_w, conv_b, w_out, seq):
    t, d = h.shape
    dff = w_out.shape[0]
    tm, tf = FFN_TM, FFN_TF
    assert t % tm == 0 and seq % tm == 0 and dff % tf == 0
    nj = dff // tf

    def tile(j):
        return jnp.clip(j, 0, nj - 1)

    return pl.pallas_call(
        functools.partial(_ffn_kernel, tiles_per_seq=seq // tm, nj=nj),
        grid=(t // tm, nj + 1),
        in_specs=[pl.BlockSpec((tm, d), lambda i, j: (i, 0)),
                  pl.BlockSpec((1, d), lambda i, j: (0, 0)),
                  pl.BlockSpec((d, tf), lambda i, j: (0, tile(j))),
                  pl.BlockSpec((d, tf), lambda i, j: (0, nj + tile(j))),
                  pl.BlockSpec((CONV_WIDTH, tf), lambda i, j: (0, tile(j - 1))),
                  pl.BlockSpec((CONV_WIDTH, tf), lambda i, j: (0, nj + tile(j - 1))),
                  pl.BlockSpec((1, tf), lambda i, j: (0, tile(j - 1))),
                  pl.BlockSpec((1, tf), lambda i, j: (0, nj + tile(j - 1))),
                  pl.BlockSpec((tf, d), lambda i, j: (tile(j - 1), 0))],
        out_specs=pl.BlockSpec((tm, d), lambda i, j: (i, 0)),
        out_shape=jax.ShapeDtypeStruct((t, d), F32),
        scratch_shapes=[pltpu.VMEM((tm, d), BF16),
                        pltpu.VMEM((tm, tf), F32), pltpu.VMEM((tm, tf), F32),
                        pltpu.VMEM((tm, tf), F32), pltpu.VMEM((tm, tf), F32),
                        pltpu.VMEM((tm, tf), BF16),
                        pltpu.VMEM((nj, 8, tf), F32),
                        pltpu.VMEM((nj, 8, tf), F32)],
        compiler_params=_cparams("arbitrary", "arbitrary"),
        name="conv_ffn",
    )(h, g, w_in, w_in, conv_w, conv_w, conv_b, conv_b, w_out)


def _row(v):
    return v.reshape(1, -1).astype(F32)


def kernel(x, rel_table, attn_norm_g, ffn_norm_g, da_w_qkv, da_w_o, da_q_gain, da_k_gain, da_lam_q1, da_lam_k1, da_lam_q2, da_lam_k2, da_sub_gain, sw_w_qkv, sw_w_o, sw_q_gain, sw_k_gain, sw_sinks, ffn_w_in, ffn_conv_w, ffn_conv_b, ffn_w_out):
    batch, seq, d = x.shape
    depth = attn_norm_g.shape[0]
    t = batch * seq
    h = x.reshape(t, d)
    q_scale = HEAD_DIM ** -0.5

    da_bias = _da_bias(rel_table, DA_T)
    sw_bias = _sw_bias(rel_table, SW_TQ)

    da_qk = 2 * DA_HEADS * HEAD_DIM
    da_flag = jnp.concatenate([jnp.ones((1, 2 * da_qk), F32), jnp.zeros((1, DA_HEADS * DA_VDIM), F32)], axis=1)
    sw_qw = SW_Q_HEADS * HEAD_DIM
    sw_kw = SW_KV_HEADS * HEAD_DIM
    sw_flag = jnp.concatenate([jnp.ones((1, sw_qw + sw_kw), F32), jnp.zeros((1, sw_kw), F32)], axis=1)

    for i in range(depth):
        j = i // N_MIXERS
        if i % N_MIXERS == 0:
            gain = jnp.concatenate([jnp.tile(da_q_gain[j], da_qk // HEAD_DIM) * (q_scale * LOG2E),
                                    jnp.tile(da_k_gain[j], da_qk // HEAD_DIM),
                                    jnp.ones((DA_HEADS * DA_VDIM,), F32)])
            qkv = _norm_proj(h, _row(attn_norm_g[i]), da_w_qkv[j].astype(BF16), da_flag, _row(gain))
            nk = seq // DA_T
            vt = qkv[:, 2 * da_qk:].reshape(batch, nk, DA_T, DA_HEADS, DA_VDIM).transpose(0, 3, 1, 4, 2)
            ones_rows = jnp.zeros((DA_VROWS - DA_VDIM, DA_T), BF16).at[0].set(1.0)
            vt = jnp.concatenate([vt, jnp.broadcast_to(ones_rows, vt.shape[:3] + ones_rows.shape)], axis=3)
            score_bound = (HEAD_DIM * q_scale * jnp.max(jnp.abs(da_q_gain[j])) * jnp.max(jnp.abs(da_k_gain[j]))
                           + jnp.max(jnp.abs(rel_table - rel_table[NUM_BUCKETS - 1]))) * LOG2E
            fast = (score_bound <= DA_FAST_BOUND).astype(jnp.int32).reshape(1)
            lamv = jnp.stack([da_lam_q1[j], da_lam_k1[j], da_lam_q2[j], da_lam_k2[j]]).astype(F32)
            lam_init = 0.8 - 0.6 * math.exp(-0.3 * i)
            a = _da_attn(fast, qkv, vt, da_bias, lamv, da_sub_gain[j].reshape(DA_VDIM, 1).astype(F32), batch, seq,
                         lam_init)
            h = _out_proj(a, da_w_o[j].astype(BF16), h)
        else:
            gain = jnp.concatenate([jnp.tile(sw_q_gain[j], SW_Q_HEADS) * q_scale,
                                    jnp.tile(sw_k_gain[j], SW_KV_HEADS),
                                    jnp.ones((sw_kw,), F32)])
            qkv = _norm_proj(h, _row(attn_norm_g[i]), sw_w_qkv[j].astype(BF16), sw_flag, _row(gain))
            k = qkv[:, sw_qw:sw_qw + sw_kw].reshape(t, SW_KV_HEADS, 1, HEAD_DIM)
            kdup = jnp.broadcast_to(k, (t, SW_KV_HEADS, 2, HEAD_DIM)).reshape(t, SW_KV_HEADS * LANES)
            vt = qkv[:, sw_qw + sw_kw:].reshape(batch, seq, SW_KV_HEADS, HEAD_DIM).transpose(0, 2, 3, 1)
            a = _sw_attn(qkv, kdup, vt, sw_bias, sw_sinks[j].astype(F32), batch, seq)
            h = _out_proj(a, sw_w_o[j].astype(BF16), h)
        h = _ffn(h, _row(ffn_norm_g[i]), ffn_w_in[i].astype(BF16), ffn_conv_w[i].astype(F32),
                 _row(ffn_conv_b[i]), ffn_w_out[i].astype(BF16), seq)
    return h.reshape(batch, seq, d)
```

```python
import functools
import math

import numpy as np
import jax
import jax.numpy as jnp
from jax import lax
from jax.experimental import pallas as pl
from jax.experimental.pallas import tpu as pltpu

F32 = jnp.float32
BF16 = jnp.bfloat16

HEAD_DIM = 64
N_MAPS = 32
DA_HEADS = 16
DA_VDIM = 128
SW_Q_HEADS = 32
SW_KV_HEADS = 4
SW_GROUP = SW_Q_HEADS // SW_KV_HEADS
SW_WINDOW = 128
NUM_BUCKETS = 32
MAX_DISTANCE = 128
CONV_WIDTH = 3
EPS = 1e-6
N_MIXERS = 2

LANES = 128
NEG = -1e30
LOG2E = math.log2(math.e)
VMEM_LIMIT = 56 * 1024 * 1024

PROJ_TM = 1024
PROJ_TN = 512
PROJ_ROWS = 256
OUT_TM = 1024
OUT_TN = 1024
FFN_TM = 512
FFN_TF = 512
FFN_ROWS = 256
FFN_GATE_ROWS = 32
DA_T = 512
DA_VROWS = DA_VDIM + 16
DA_FAST_BOUND = 60.0
SW_TQ = 256


def _cparams(*sem):
    return pltpu.CompilerParams(dimension_semantics=sem, vmem_limit_bytes=VMEM_LIMIT)


def _t5_bucket_np(dist):
    max_exact = NUM_BUCKETS // 2
    d = np.maximum(dist, 0)
    df = np.maximum(d, 1).astype(np.float64)
    large = max_exact + (np.log(df / max_exact) / math.log(MAX_DISTANCE / max_exact)
                         * (NUM_BUCKETS - max_exact)).astype(np.int32)
    large = np.minimum(large, NUM_BUCKETS - 1)
    return np.where(d < max_exact, d, large).astype(np.int32)


def _da_bias_kernel(tab_ref, bucket_ref, out_ref, *, t):
    h = pl.program_id(0)
    for off in range(2):
        bk = bucket_ref[off]
        for m in range(2):
            col = 2 * h + m
            far = tab_ref[NUM_BUCKETS - 1, col]
            acc = jnp.zeros(bk.shape, F32)
            for b in range(NUM_BUCKETS - 1):
                acc = jnp.where(bk == b, (tab_ref[b, col] - far) * LOG2E, acc)
            out_ref[0, off, :, m * t:(m + 1) * t] = jnp.where(bk < 0, NEG, acc)


def _da_bias(rel_table, t):
    key = np.arange(t)[:, None]
    qry = np.arange(t)[None, :]
    d0 = qry - key
    d1 = t + qry - key
    bucket = np.stack([np.where(d0 >= 0, _t5_bucket_np(d0), -1), _t5_bucket_np(d1)]).astype(np.int32)
    return pl.pallas_call(
        functools.partial(_da_bias_kernel, t=t),
        grid=(DA_HEADS,),
        in_specs=[pl.BlockSpec(memory_space=pltpu.SMEM),
                  pl.BlockSpec((2, t, t), lambda h: (0, 0, 0))],
        out_specs=pl.BlockSpec((1, 2, t, 2 * t), lambda h: (h, 0, 0, 0)),
        out_shape=jax.ShapeDtypeStruct((DA_HEADS, 2, t, 2 * t), F32),
        compiler_params=_cparams("arbitrary"),
        name="da_bias",
    )(rel_table, jnp.asarray(bucket))


def _sw_bias_kernel(tab_ref, bucket_ref, out_ref, *, tq):
    p = pl.program_id(0)
    bk = bucket_ref[...]
    for m in range(2):
        col = 2 * p + m
        acc = jnp.zeros(bk.shape, F32)
        for b in range(NUM_BUCKETS):
            acc = jnp.where(bk == b, tab_ref[b, col], acc)
        out_ref[0, :, m * tq:(m + 1) * tq] = jnp.where(bk < 0, NEG, acc)


def _sw_bias(rel_table, tq):
    band = SW_WINDOW + tq
    key = np.arange(band)[:, None]
    qry = np.arange(tq)[None, :]
    dist = qry + SW_WINDOW - key
    bucket = np.where((dist >= 0) & (dist < SW_WINDOW), _t5_bucket_np(dist), -1).astype(np.int32)
    return pl.pallas_call(
        functools.partial(_sw_bias_kernel, tq=tq),
        grid=(SW_Q_HEADS // 2,),
        in_specs=[pl.BlockSpec(memory_space=pltpu.SMEM),
                  pl.BlockSpec((band, tq), lambda p: (0, 0))],
        out_specs=pl.BlockSpec((1, band, 2 * tq), lambda p: (p, 0, 0)),
        out_shape=jax.ShapeDtypeStruct((SW_Q_HEADS // 2, band, 2 * tq), F32),
        compiler_params=_cparams("arbitrary"),
        name="sw_bias",
    )(rel_table, jnp.asarray(bucket))


def _rms_rows(x, g):
    ms = jnp.mean(x * x, axis=-1, keepdims=True)
    return x * lax.rsqrt(ms + EPS) * g


def _norm_proj_kernel(x_ref, g_ref, w_ref, flag_ref, gain_ref, o_ref, xn_ref, y0_ref, y1_ref, *, nj):
    j = pl.program_id(1)
    tm, tn = y0_ref.shape

    @pl.when(j == 0)
    def _():
        xn_ref[...] = _rms_rows(x_ref[...], g_ref[...]).astype(BF16)

    def run(parity, do_norm, do_project):
        y_cur, y_prev = (y0_ref, y1_ref)[::1 - 2 * parity]
        lane = lax.broadcasted_iota(jnp.int32, (1, LANES), 1)
        low = lane < HEAD_DIM
        for r in range(tm // PROJ_ROWS):
            rows = slice(r * PROJ_ROWS, (r + 1) * PROJ_ROWS)
            if do_norm:
                for c in range(tn // LANES):
                    sl = slice(c * LANES, (c + 1) * LANES)
                    yc = y_prev[rows, sl]
                    y2 = yc * yc
                    ms_lo = jnp.sum(jnp.where(low, y2, 0.0), axis=-1, keepdims=True) * (1.0 / HEAD_DIM)
                    ms_hi = jnp.sum(jnp.where(low, 0.0, y2), axis=-1, keepdims=True) * (1.0 / HEAD_DIM)
                    inv = jnp.where(low, lax.rsqrt(ms_lo + EPS), lax.rsqrt(ms_hi + EPS))
                    scale = jnp.where(flag_ref[:, sl] > 0.0, inv, 1.0) * gain_ref[:, sl]
                    o_ref[rows, sl] = (yc * scale).astype(BF16)
            if do_project:
                y_cur[rows, :] = jnp.dot(xn_ref[rows, :], w_ref[...], preferred_element_type=F32)

    even = (j % 2) == 0
    inner = (j >= 1) & (j < nj)
    pl.when(j == 0)(functools.partial(run, 0, False, True))
    pl.when(inner & even)(functools.partial(run, 0, True, True))
    pl.when(inner & jnp.logical_not(even))(functools.partial(run, 1, True, True))
    pl.when(j == nj)(functools.partial(run, nj % 2, True, False))


def _norm_proj(x, g, w_tiles, flag, gain):
    t, d = x.shape
    nj, _, tn = w_tiles.shape
    tm = PROJ_TM
    assert t % tm == 0 and tm % PROJ_ROWS == 0

    def tile(j):
        return jnp.clip(j, 0, nj - 1)

    return pl.pallas_call(
        functools.partial(_norm_proj_kernel, nj=nj),
        grid=(t // tm, nj + 1),
        in_specs=[pl.BlockSpec((tm, d), lambda i, j: (i, 0)),
                  pl.BlockSpec((1, d), lambda i, j: (0, 0)),
                  pl.BlockSpec((None, d, tn), lambda i, j: (tile(j), 0, 0)),
                  pl.BlockSpec((1, tn), lambda i, j: (0, tile(j - 1))),
                  pl.BlockSpec((1, tn), lambda i, j: (0, tile(j - 1)))],
        out_specs=pl.BlockSpec((tm, tn), lambda i, j: (i, tile(j - 1))),
        out_shape=jax.ShapeDtypeStruct((t, nj * tn), BF16),
        scratch_shapes=[pltpu.VMEM((tm, d), BF16), pltpu.VMEM((tm, tn), F32), pltpu.VMEM((tm, tn), F32)],
        compiler_params=_cparams("arbitrary", "arbitrary"),
        name="norm_proj",
    )(x, g, w_tiles, flag, gain)


def _out_proj_kernel(a_ref, w_ref, h_ref, o_ref):
    o_ref[...] = h_ref[...] + jnp.dot(a_ref[...], w_ref[...], preferred_element_type=F32)


def _out_proj(a, w_tiles, h):
    t, k = a.shape
    nj, _, tn = w_tiles.shape
    tm = OUT_TM
    assert t % tm == 0
    return pl.pallas_call(
        _out_proj_kernel,
        grid=(t // tm, nj),
        in_specs=[pl.BlockSpec((tm, k), lambda i, j: (i, 0)),
                  pl.BlockSpec((None, k, tn), lambda i, j: (j, 0, 0)),
                  pl.BlockSpec((tm, tn), lambda i, j: (i, j))],
        out_specs=pl.BlockSpec((tm, tn), lambda i, j: (i, j)),
        out_shape=jax.ShapeDtypeStruct((t, nj * tn), F32),
        compiler_params=_cparams("arbitrary", "arbitrary"),
        name="out_proj",
    )(a, w_tiles, h)


def _stack_maps(q):
    lane = lax.broadcasted_iota(jnp.int32, q.shape, 1)
    qf = q.astype(F32)
    return jnp.concatenate([jnp.where(lane < HEAD_DIM, qf, 0.0), jnp.where(lane < HEAD_DIM, 0.0, qf)],
                           axis=0).astype(BF16)


def _da_attn_kernel(fast_ref, q_ref, k_ref, vt_ref, bias_ref, lamv_ref, sg_ref, o_ref, acc_ref, m_ref, *, t, lam_init):
    i = pl.program_id(2)
    qt = q_ref[...].astype(F32).T
    row = lax.broadcasted_iota(jnp.int32, qt.shape, 0)
    qqt = jnp.concatenate([jnp.where(row < HEAD_DIM, qt, 0.0), jnp.where(row < HEAD_DIM, 0.0, qt)],
                          axis=1).astype(BF16)
    acc_ref[...] = jnp.zeros(acc_ref.shape, F32)

    def scores(u, bias):
        k = k_ref[pl.ds(pl.multiple_of(u * t, t), t), :]
        st = jnp.dot(k, qqt, preferred_element_type=F32)
        return st if bias is None else st + bias

    def chunk(u0, biases):
        pv = None
        for n, bias in enumerate(biases):
            p = jnp.exp2(scores(u0 + n, bias)).astype(BF16)
            d = jnp.dot(vt_ref[u0 + n], p, preferred_element_type=F32)
            pv = d if pv is None else pv + d
        acc_ref[...] += pv

    def unit_running_max(u, bias):
        st = scores(u, bias)
        m_prev = m_ref[...]
        m_new = jnp.maximum(m_prev, jnp.max(st, axis=0, keepdims=True))
        p = jnp.exp2(st - m_new).astype(BF16)
        acc_ref[...] = jnp.exp2(m_prev - m_new) * acc_ref[...] + jnp.dot(vt_ref[u], p, preferred_element_type=F32)
        m_ref[...] = m_new

    fast = fast_ref[0] == 1
    odd = (i % 2) == 1

    @pl.when(fast)
    def _():
        def far_pair(c, carry):
            chunk(2 * c, (None, None))
            return carry

        lax.fori_loop(0, jnp.maximum(i - 1, 0) // 2, far_pair, 0)

    @pl.when(fast & (i == 0))
    def _():
        chunk(0, (bias_ref[0, 0],))

    @pl.when(fast & odd)
    def _():
        chunk(i - 1, (bias_ref[0, 1], bias_ref[0, 0]))

    @pl.when(fast & jnp.logical_not(odd) & (i > 0))
    def _():
        chunk(i - 2, (None, bias_ref[0, 1], bias_ref[0, 0]))

    @pl.when(jnp.logical_not(fast))
    def _():
        m_ref[...] = jnp.full(m_ref.shape, NEG, F32)

        def far_unit(u, carry):
            unit_running_max(u, None)
            return carry

        lax.fori_loop(0, jnp.maximum(i - 1, 0), far_unit, 0)

        @pl.when(i >= 1)
        def _():
            unit_running_max(i - 1, bias_ref[0, 1])

        unit_running_max(i, bias_ref[0, 0])

    lv = lamv_ref[...]
    lam = (jnp.exp(jnp.sum(lv[0:1] * lv[1:2], axis=-1, keepdims=True))
           - jnp.exp(jnp.sum(lv[2:3] * lv[3:4], axis=-1, keepdims=True)) + lam_init)
    acc = acc_ref[...]
    o = acc[:DA_VDIM] * (1.0 / acc[DA_VDIM:DA_VDIM + 1])
    ot = o[:, :t] - lam * o[:, t:]
    ms = jnp.mean(ot * ot, axis=0, keepdims=True)
    ot = ot * lax.rsqrt(ms + EPS) * (sg_ref[...] * (1.0 - lam_init))
    o_ref[...] = ot.T.astype(BF16)


def _da_attn(fast, qkv, vt, bias, lamv, sub_gain, batch, seq, lam_init):
    t = DA_T
    nq = seq // t
    return pl.pallas_call(
        functools.partial(_da_attn_kernel, t=t, lam_init=lam_init),
        grid=(batch, DA_HEADS, nq),
        in_specs=[pl.BlockSpec(memory_space=pltpu.SMEM),
                  pl.BlockSpec((t, LANES), lambda b, h, i: (b * nq + i, h)),
                  pl.BlockSpec((seq, LANES), lambda b, h, i: (b, DA_HEADS + h)),
                  pl.BlockSpec((None, None, nq, DA_VROWS, t), lambda b, h, i: (b, h, 0, 0, 0)),
                  pl.BlockSpec((1, 2, t, 2 * t), lambda b, h, i: (h, 0, 0, 0)),
                  pl.BlockSpec((4, HEAD_DIM), lambda b, h, i: (0, 0)),
                  pl.BlockSpec((DA_VDIM, 1), lambda b, h, i: (0, 0))],
        out_specs=pl.BlockSpec((t, LANES), lambda b, h, i: (b * nq + i, h)),
        out_shape=jax.ShapeDtypeStruct((batch * seq, DA_HEADS * DA_VDIM), BF16),
        scratch_shapes=[pltpu.VMEM((DA_VROWS, 2 * t), F32), pltpu.VMEM((1, 2 * t), F32)],
        compiler_params=_cparams("arbitrary", "arbitrary", "arbitrary"),
        name="da_attn",
    )(fast, qkv, qkv, vt, bias, lamv, sub_gain)


def _sw_attn_kernel(sink_ref, q_ref, kp_ref, kc_ref, vp_ref, vc_ref, bias_ref, o_ref, *, tq):
    g = pl.program_id(0)
    i = pl.program_id(2)
    band = SW_WINDOW + tq
    kb = jnp.concatenate([kp_ref[...], kc_ref[...]], axis=0)
    vtb = jnp.concatenate([vp_ref[...], vc_ref[...]], axis=1)
    row = lax.broadcasted_iota(jnp.int32, (band, 1), 0)
    before_start = row < jnp.where(i == 0, SW_WINDOW, 0)
    col = lax.broadcasted_iota(jnp.int32, (1, 2 * tq), 1)
    for pp in range(SW_GROUP // 2):
        sl = slice(pp * LANES, (pp + 1) * LANES)
        head = g * SW_GROUP + 2 * pp
        sink = jnp.where(col < tq, sink_ref[head], sink_ref[head + 1])
        qq = _stack_maps(q_ref[:, sl])
        st = lax.dot_general(kb, qq, (((1,), (1,)), ((), ())), preferred_element_type=F32)
        st = jnp.where(before_start, NEG, st + bias_ref[pp])
        m = jnp.maximum(jnp.max(st, axis=0, keepdims=True), sink)
        e = jnp.exp(st - m)
        denom = jnp.sum(e, axis=0, keepdims=True) + jnp.exp(sink - m)
        ot = jnp.dot(vtb, e.astype(BF16), preferred_element_type=F32) * (1.0 / denom)
        o_ref[:, sl] = jnp.concatenate([ot[:, :tq], ot[:, tq:]], axis=0).T.astype(BF16)


def _sw_attn(q, kdup, vt, bias, sinks, batch, seq):
    tq = SW_TQ
    nq = seq // tq
    r = tq // SW_WINDOW
    band = SW_WINDOW + tq
    gw = SW_GROUP * HEAD_DIM
    return pl.pallas_call(
        functools.partial(_sw_attn_kernel, tq=tq),
        grid=(SW_KV_HEADS, batch, nq),
        in_specs=[pl.BlockSpec(memory_space=pltpu.SMEM),
                  pl.BlockSpec((tq, gw), lambda g, b, i: (b * nq + i, g)),
                  pl.BlockSpec((SW_WINDOW, LANES), lambda g, b, i: (jnp.maximum((b * nq + i) * r - 1, 0), g)),
                  pl.BlockSpec((tq, LANES), lambda g, b, i: (b * nq + i, g)),
                  pl.BlockSpec((None, None, HEAD_DIM, SW_WINDOW), lambda g, b, i: (b, g, 0, jnp.maximum(i * r - 1, 0))),
                  pl.BlockSpec((None, None, HEAD_DIM, tq), lambda g, b, i: (b, g, 0, i)),
                  pl.BlockSpec((SW_GROUP // 2, band, 2 * tq), lambda g, b, i: (g, 0, 0))],
        out_specs=pl.BlockSpec((tq, gw), lambda g, b, i: (b * nq + i, g)),
        out_shape=jax.ShapeDtypeStruct((batch * seq, SW_Q_HEADS * HEAD_DIM), BF16),
        compiler_params=_cparams("arbitrary", "arbitrary", "arbitrary"),
        name="sw_attn",
    )(sinks, q, kdup, kdup, vt, vt, bias)


def _causal_conv(u, halo, cw_ref, cb_ref):
    row = lax.broadcasted_iota(jnp.int32, (u.shape[0], 1), 0)
    u1 = jnp.where(row == 0, halo[7:8], pltpu.roll(u, 1, axis=0))
    u2 = jnp.where(row == 0, halo[6:7], jnp.where(row == 1, halo[7:8], pltpu.roll(u, 2, axis=0)))
    return cw_ref[0:1, :] * u2 + cw_ref[1:2, :] * u1 + cw_ref[2:3, :] * u + cb_ref[...]


def _ffn_kernel(h_ref, g_ref, wg_ref, wu_ref, cwg_ref, cwu_ref, cbg_ref, cbu_ref, wo_ref, o_ref,
                f_ref, ug0_ref, ug1_ref, uu0_ref, uu1_ref, act_ref, halo_g_ref, halo_u_ref, *, tiles_per_seq, nj):
    i = pl.program_id(0)
    j = pl.program_id(1)
    tm = h_ref.shape[0]

    @pl.when(j == 0)
    def _():
        x = h_ref[...]
        f_ref[...] = _rms_rows(x, g_ref[...]).astype(BF16)
        o_ref[...] = x

    @pl.when(((i % tiles_per_seq) == 0) & (j < nj))
    def _():
        halo_g_ref[j] = jnp.zeros(halo_g_ref.shape[1:], F32)
        halo_u_ref[j] = jnp.zeros(halo_u_ref.shape[1:], F32)

    jb = j - 1

    def up_and_gate(parity, do_gate, do_up):
        ug_cur, ug_prev = (ug0_ref, ug1_ref)[::1 - 2 * parity]
        uu_cur, uu_prev = (uu0_ref, uu1_ref)[::1 - 2 * parity]

        def conv(u_ref, cw_ref, cb_ref, rows, halo):
            u = u_ref[rows, :]
            return _causal_conv(u, halo, cw_ref, cb_ref), u[FFN_GATE_ROWS - 8:, :]

        if do_gate:
            halo_g, halo_u = halo_g_ref[jb], halo_u_ref[jb]
        for r in range(tm // FFN_ROWS):
            rows = slice(r * FFN_ROWS, (r + 1) * FFN_ROWS)
            if do_gate:
                for q in range(r * FFN_ROWS, (r + 1) * FFN_ROWS, FFN_GATE_ROWS):
                    piece = slice(q, q + FFN_GATE_ROWS)
                    gate, halo_g = conv(ug_prev, cwg_ref, cbg_ref, piece, halo_g)
                    up, halo_u = conv(uu_prev, cwu_ref, cbu_ref, piece, halo_u)
                    act_ref[piece, :] = (gate * (1.0 / (1.0 + jnp.exp(-gate))) * up).astype(BF16)
            if do_up:
                f = f_ref[rows, :]
                ug_cur[rows, :] = jnp.dot(f, wg_ref[...], preferred_element_type=F32)
                uu_cur[rows, :] = jnp.dot(f, wu_ref[...], preferred_element_type=F32)
        if do_gate:
            halo_g_ref[jb] = halo_g
            halo_u_ref[jb] = halo_u

    even = (j % 2) == 0
    inner = (j >= 1) & (j < nj)
    pl.when(j == 0)(functools.partial(up_and_gate, 0, False, True))
    pl.when(inner & even)(functools.partial(up_and_gate, 0, True, True))
    pl.when(inner & jnp.logical_not(even))(functools.partial(up_and_gate, 1, True, True))
    pl.when(j == nj)(functools.partial(up_and_gate, nj % 2, True, False))

    @pl.when(j >= 1)
    def _():
        o_ref[...] += jnp.dot(act_ref[...], wo_ref[...], preferred_element_type=F32)


def _ffn(h, g, w_in_tiles, conv_w, conv_b, w_out, seq):
    t, d = h.shape
    tm = FFN_TM
    tf = w_in_tiles.shape[2]
    nj = w_in_tiles.shape[0] // 2
    assert t % tm == 0 and seq % tm == 0 and w_out.shape[0] == nj * tf

    def tile(j):
        return jnp.clip(j, 0, nj - 1)

    return pl.pallas_call(
        functools.partial(_ffn_kernel, tiles_per_seq=seq // tm, nj=nj),
        grid=(t // tm, nj + 1),
        in_specs=[pl.BlockSpec((tm, d), lambda i, j: (i, 0)),
                  pl.BlockSpec((1, d), lambda i, j: (0, 0)),
                  pl.BlockSpec((None, d, tf), lambda i, j: (tile(j), 0, 0)),
                  pl.BlockSpec((None, d, tf), lambda i, j: (nj + tile(j), 0, 0)),
                  pl.BlockSpec((CONV_WIDTH, tf), lambda i, j: (0, tile(j - 1))),
                  pl.BlockSpec((CONV_WIDTH, tf), lambda i, j: (0, nj + tile(j - 1))),
                  pl.BlockSpec((1, tf), lambda i, j: (0, tile(j - 1))),
                  pl.BlockSpec((1, tf), lambda i, j: (0, nj + tile(j - 1))),
                  pl.BlockSpec((tf, d), lambda i, j: (tile(j - 1), 0))],
        out_specs=pl.BlockSpec((tm, d), lambda i, j: (i, 0)),
        out_shape=jax.ShapeDtypeStruct((t, d), F32),
        scratch_shapes=[pltpu.VMEM((tm, d), BF16),
                        pltpu.VMEM((tm, tf), F32), pltpu.VMEM((tm, tf), F32),
                        pltpu.VMEM((tm, tf), F32), pltpu.VMEM((tm, tf), F32),
                        pltpu.VMEM((tm, tf), BF16),
                        pltpu.VMEM((nj, 8, tf), F32),
                        pltpu.VMEM((nj, 8, tf), F32)],
        compiler_params=_cparams("arbitrary", "arbitrary"),
        name="conv_ffn",
    )(h, g, w_in_tiles, w_in_tiles, conv_w, conv_w, conv_b, conv_b, w_out)


def _row(v):
    return v.reshape(1, -1).astype(F32)


def _col_tiles(w, tn):
    k, n = w.shape
    assert n % tn == 0
    return w.astype(BF16).reshape(k, n // tn, tn).transpose(1, 0, 2)


def kernel(x, rel_table, attn_norm_g, ffn_norm_g, da_w_qkv, da_w_o, da_q_gain, da_k_gain, da_lam_q1, da_lam_k1, da_lam_q2, da_lam_k2, da_sub_gain, sw_w_qkv, sw_w_o, sw_q_gain, sw_k_gain, sw_sinks, ffn_w_in, ffn_conv_w, ffn_conv_b, ffn_w_out):
    batch, seq, d = x.shape
    depth = attn_norm_g.shape[0]
    t = batch * seq
    h = x.reshape(t, d)
    q_scale = HEAD_DIM ** -0.5

    da_bias = _da_bias(rel_table, DA_T)
    sw_bias = _sw_bias(rel_table, SW_TQ)

    da_qk = 2 * DA_HEADS * HEAD_DIM
    da_flag = jnp.concatenate([jnp.ones((1, 2 * da_qk), F32), jnp.zeros((1, DA_HEADS * DA_VDIM), F32)], axis=1)
    sw_qw = SW_Q_HEADS * HEAD_DIM
    sw_kw = SW_KV_HEADS * HEAD_DIM
    sw_flag = jnp.concatenate([jnp.ones((1, sw_qw + sw_kw), F32), jnp.zeros((1, sw_kw), F32)], axis=1)

    for i in range(depth):
        j = i // N_MIXERS
        if i % N_MIXERS == 0:
            gain = jnp.concatenate([jnp.tile(da_q_gain[j], da_qk // HEAD_DIM) * (q_scale * LOG2E),
                                    jnp.tile(da_k_gain[j], da_qk // HEAD_DIM),
                                    jnp.ones((DA_HEADS * DA_VDIM,), F32)])
            qkv = _norm_proj(h, _row(attn_norm_g[i]), _col_tiles(da_w_qkv[j], PROJ_TN), da_flag, _row(gain))
            nk = seq // DA_T
            vt = qkv[:, 2 * da_qk:].reshape(batch, nk, DA_T, DA_HEADS, DA_VDIM).transpose(0, 3, 1, 4, 2)
            ones_rows = jnp.zeros((DA_VROWS - DA_VDIM, DA_T), BF16).at[0].set(1.0)
            vt = jnp.concatenate([vt, jnp.broadcast_to(ones_rows, vt.shape[:3] + ones_rows.shape)], axis=3)
            score_bound = (HEAD_DIM * q_scale * jnp.max(jnp.abs(da_q_gain[j])) * jnp.max(jnp.abs(da_k_gain[j]))
                           + jnp.max(jnp.abs(rel_table - rel_table[NUM_BUCKETS - 1]))) * LOG2E
            fast = (score_bound <= DA_FAST_BOUND).astype(jnp.int32).reshape(1)
            lamv = jnp.stack([da_lam_q1[j], da_lam_k1[j], da_lam_q2[j], da_lam_k2[j]]).astype(F32)
            lam_init = 0.8 - 0.6 * math.exp(-0.3 * i)
            a = _da_attn(fast, qkv, vt, da_bias, lamv, da_sub_gain[j].reshape(DA_VDIM, 1).astype(F32), batch, seq,
                         lam_init)
            h = _out_proj(a, _col_tiles(da_w_o[j], OUT_TN), h)
        else:
            gain = jnp.concatenate([jnp.tile(sw_q_gain[j], SW_Q_HEADS) * q_scale,
                                    jnp.tile(sw_k_gain[j], SW_KV_HEADS),
                                    jnp.ones((sw_kw,), F32)])
            qkv = _norm_proj(h, _row(attn_norm_g[i]), _col_tiles(sw_w_qkv[j], PROJ_TN), sw_flag, _row(gain))
            k = qkv[:, sw_qw:sw_qw + sw_kw].reshape(t, SW_KV_HEADS, 1, HEAD_DIM)
            kdup = jnp.broadcast_to(k, (t, SW_KV_HEADS, 2, HEAD_DIM)).reshape(t, SW_KV_HEADS * LANES)
            vt = qkv[:, sw_qw + sw_kw:].reshape(batch, seq, SW_KV_HEADS, HEAD_DIM).transpose(0, 2, 3, 1)
            a = _sw_attn(qkv, kdup, vt, sw_bias, sw_sinks[j].astype(F32), batch, seq)
            h = _out_proj(a, _col_tiles(sw_w_o[j], OUT_TN), h)
        h = _ffn(h, _row(ffn_norm_g[i]), _col_tiles(ffn_w_in[i], FFN_TF), ffn_conv_w[i].astype(F32),
                 _row(ffn_conv_b[i]), ffn_w_out[i].astype(BF16), seq)
    return h.reshape(batch, seq, d)
```

```python
import functools
import math

import numpy as np
import jax
import jax.numpy as jnp
from jax import lax
from jax.experimental import pallas as pl
from jax.experimental.pallas import tpu as pltpu

F32 = jnp.float32
BF16 = jnp.bfloat16

HEAD_DIM = 64
N_MAPS = 32
DA_HEADS = 16
DA_VDIM = 128
SW_Q_HEADS = 32
SW_KV_HEADS = 4
SW_GROUP = SW_Q_HEADS // SW_KV_HEADS
SW_WINDOW = 128
NUM_BUCKETS = 32
MAX_DISTANCE = 128
CONV_WIDTH = 3
EPS = 1e-6
N_MIXERS = 2

LANES = 128
NEG = -1e30
LOG2E = math.log2(math.e)
VMEM_LIMIT = 56 * 1024 * 1024

PROJ_TM = 1024
PROJ_TN = 512
PROJ_ROWS = 256
OUT_TM = 1024
OUT_TN = 1024
FFN_TM = 512
FFN_TF = 512
FFN_ROWS = 256
FFN_GATE_ROWS = 32
DA_T = 512
DA_VROWS = DA_VDIM + 16
DA_FAST_BOUND = 60.0
SW_TQ = 256


def _cparams(*sem):
    return pltpu.CompilerParams(dimension_semantics=sem, vmem_limit_bytes=VMEM_LIMIT)


def _t5_bucket_np(dist):
    max_exact = NUM_BUCKETS // 2
    d = np.maximum(dist, 0)
    df = np.maximum(d, 1).astype(np.float64)
    large = max_exact + (np.log(df / max_exact) / math.log(MAX_DISTANCE / max_exact)
                         * (NUM_BUCKETS - max_exact)).astype(np.int32)
    large = np.minimum(large, NUM_BUCKETS - 1)
    return np.where(d < max_exact, d, large).astype(np.int32)


def _da_bias_kernel(tab_ref, bucket_ref, out_ref, *, t):
    h = pl.program_id(0)
    for off in range(2):
        bk = bucket_ref[off]
        for m in range(2):
            col = 2 * h + m
            far = tab_ref[NUM_BUCKETS - 1, col]
            acc = jnp.zeros(bk.shape, F32)
            for b in range(NUM_BUCKETS - 1):
                acc = jnp.where(bk == b, (tab_ref[b, col] - far) * LOG2E, acc)
            out_ref[0, off, :, m * t:(m + 1) * t] = jnp.where(bk < 0, NEG, acc)


def _da_bias(rel_table, t):
    key = np.arange(t)[:, None]
    qry = np.arange(t)[None, :]
    d0 = qry - key
    d1 = t + qry - key
    bucket = np.stack([np.where(d0 >= 0, _t5_bucket_np(d0), -1), _t5_bucket_np(d1)]).astype(np.int32)
    return pl.pallas_call(
        functools.partial(_da_bias_kernel, t=t),
        grid=(DA_HEADS,),
        in_specs=[pl.BlockSpec(memory_space=pltpu.SMEM),
                  pl.BlockSpec((2, t, t), lambda h: (0, 0, 0))],
        out_specs=pl.BlockSpec((1, 2, t, 2 * t), lambda h: (h, 0, 0, 0)),
        out_shape=jax.ShapeDtypeStruct((DA_HEADS, 2, t, 2 * t), F32),
        compiler_params=_cparams("arbitrary"),
        name="da_bias",
    )(rel_table, jnp.asarray(bucket))


def _sw_bias_kernel(tab_ref, bucket_ref, out_ref, *, tq):
    p = pl.program_id(0)
    bk = bucket_ref[...]
    for m in range(2):
        col = 2 * p + m
        acc = jnp.zeros(bk.shape, F32)
        for b in range(NUM_BUCKETS):
            acc = jnp.where(bk == b, tab_ref[b, col], acc)
        out_ref[0, :, m * tq:(m + 1) * tq] = jnp.where(bk < 0, NEG, acc)


def _sw_bias(rel_table, tq):
    band = SW_WINDOW + tq
    key = np.arange(band)[:, None]
    qry = np.arange(tq)[None, :]
    dist = qry + SW_WINDOW - key
    bucket = np.where((dist >= 0) & (dist < SW_WINDOW), _t5_bucket_np(dist), -1).astype(np.int32)
    return pl.pallas_call(
        functools.partial(_sw_bias_kernel, tq=tq),
        grid=(SW_Q_HEADS // 2,),
        in_specs=[pl.BlockSpec(memory_space=pltpu.SMEM),
                  pl.BlockSpec((band, tq), lambda p: (0, 0))],
        out_specs=pl.BlockSpec((1, band, 2 * tq), lambda p: (p, 0, 0)),
        out_shape=jax.ShapeDtypeStruct((SW_Q_HEADS // 2, band, 2 * tq), F32),
        compiler_params=_cparams("arbitrary"),
        name="sw_bias",
    )(rel_table, jnp.asarray(bucket))


def _rms_rows(x, g):
    ms = jnp.mean(x * x, axis=-1, keepdims=True)
    return x * lax.rsqrt(ms + EPS) * g


def _norm_proj_kernel(x_ref, g_ref, w_ref, flag_ref, gain_ref, o_ref, xn_ref, y0_ref, y1_ref):
    i = pl.program_id(0)
    j = pl.program_id(1)
    tm, tn = y0_ref.shape

    @pl.when((i == 0) & (j == 0))
    def _():
        y1_ref[...] = jnp.zeros(y1_ref.shape, F32)

    @pl.when(j == 0)
    def _():
        xn_ref[...] = _rms_rows(x_ref[...], g_ref[...]).astype(BF16)

    def run(parity):
        y_cur, y_prev = (y0_ref, y1_ref)[::1 - 2 * parity]
        lane = lax.broadcasted_iota(jnp.int32, (1, LANES), 1)
        low = lane < HEAD_DIM
        for r in range(tm // PROJ_ROWS):
            rows = slice(r * PROJ_ROWS, (r + 1) * PROJ_ROWS)
            for c in range(tn // LANES):
                sl = slice(c * LANES, (c + 1) * LANES)
                yc = y_prev[rows, sl]
                y2 = yc * yc
                ms_lo = jnp.sum(jnp.where(low, y2, 0.0), axis=-1, keepdims=True) * (1.0 / HEAD_DIM)
                ms_hi = jnp.sum(jnp.where(low, 0.0, y2), axis=-1, keepdims=True) * (1.0 / HEAD_DIM)
                inv = jnp.where(low, lax.rsqrt(ms_lo + EPS), lax.rsqrt(ms_hi + EPS))
                scale = jnp.where(flag_ref[:, sl] > 0.0, inv, 1.0) * gain_ref[:, sl]
                o_ref[rows, sl] = (yc * scale).astype(BF16)
            y_cur[rows, :] = jnp.dot(xn_ref[rows, :], w_ref[...], preferred_element_type=F32)

    even = (j % 2) == 0
    pl.when(even)(functools.partial(run, 0))
    pl.when(jnp.logical_not(even))(functools.partial(run, 1))


def _norm_proj(x, g, w, flag, gain):
    t, d = x.shape
    n = w.shape[1]
    tm, tn = PROJ_TM, PROJ_TN
    assert t % tm == 0 and n % tn == 0 and tm % PROJ_ROWS == 0
    nj = n // tn

    def tile(j):
        return jnp.clip(j, 0, nj - 1)

    return pl.pallas_call(
        _norm_proj_kernel,
        grid=(t // tm, nj + 1),
        in_specs=[pl.BlockSpec((tm, d), lambda i, j: (i, 0)),
                  pl.BlockSpec((1, d), lambda i, j: (0, 0)),
                  pl.BlockSpec((d, tn), lambda i, j: (0, tile(j))),
                  pl.BlockSpec((1, tn), lambda i, j: (0, tile(j - 1))),
                  pl.BlockSpec((1, tn), lambda i, j: (0, tile(j - 1)))],
        out_specs=pl.BlockSpec((tm, tn), lambda i, j: (i, tile(j - 1))),
        out_shape=jax.ShapeDtypeStruct((t, n), BF16),
        scratch_shapes=[pltpu.VMEM((tm, d), BF16), pltpu.VMEM((tm, tn), F32), pltpu.VMEM((tm, tn), F32)],
        compiler_params=_cparams("arbitrary", "arbitrary"),
        name="norm_proj",
    )(x, g, w, flag, gain)


def _out_proj_kernel(a_ref, w_ref, h_ref, o_ref):
    o_ref[...] = h_ref[...] + jnp.dot(a_ref[...], w_ref[...], preferred_element_type=F32)


def _out_proj(a, w, h):
    t, k = a.shape
    n = w.shape[1]
    tm, tn = OUT_TM, OUT_TN
    assert t % tm == 0 and n % tn == 0
    return pl.pallas_call(
        _out_proj_kernel,
        grid=(t // tm, n // tn),
        in_specs=[pl.BlockSpec((tm, k), lambda i, j: (i, 0)),
                  pl.BlockSpec((k, tn), lambda i, j: (0, j)),
                  pl.BlockSpec((tm, tn), lambda i, j: (i, j))],
        out_specs=pl.BlockSpec((tm, tn), lambda i, j: (i, j)),
        out_shape=jax.ShapeDtypeStruct((t, n), F32),
        compiler_params=_cparams("arbitrary", "arbitrary"),
        name="out_proj",
    )(a, w, h)


def _stack_maps(q):
    lane = lax.broadcasted_iota(jnp.int32, q.shape, 1)
    qf = q.astype(F32)
    return jnp.concatenate([jnp.where(lane < HEAD_DIM, qf, 0.0), jnp.where(lane < HEAD_DIM, 0.0, qf)],
                           axis=0).astype(BF16)


def _da_attn_kernel(fast_ref, q_ref, k_ref, vt_ref, bias_ref, lamv_ref, sg_ref, o_ref, acc_ref, m_ref, *, t, lam_init):
    i = pl.program_id(2)
    qt = q_ref[...].astype(F32).T
    row = lax.broadcasted_iota(jnp.int32, qt.shape, 0)
    qqt = jnp.concatenate([jnp.where(row < HEAD_DIM, qt, 0.0), jnp.where(row < HEAD_DIM, 0.0, qt)],
                          axis=1).astype(BF16)
    acc_ref[...] = jnp.zeros(acc_ref.shape, F32)

    def scores(u, bias):
        k = k_ref[pl.ds(pl.multiple_of(u * t, t), t), :]
        st = jnp.dot(k, qqt, preferred_element_type=F32)
        return st if bias is None else st + bias

    def chunk(u0, biases):
        pv = None
        for n, bias in enumerate(biases):
            p = jnp.exp2(scores(u0 + n, bias)).astype(BF16)
            d = jnp.dot(vt_ref[u0 + n], p, preferred_element_type=F32)
            pv = d if pv is None else pv + d
        acc_ref[...] += pv

    def unit_running_max(u, bias):
        st = scores(u, bias)
        m_prev = m_ref[...]
        m_new = jnp.maximum(m_prev, jnp.max(st, axis=0, keepdims=True))
        p = jnp.exp2(st - m_new).astype(BF16)
        acc_ref[...] = jnp.exp2(m_prev - m_new) * acc_ref[...] + jnp.dot(vt_ref[u], p, preferred_element_type=F32)
        m_ref[...] = m_new

    fast = fast_ref[0] == 1
    odd = (i % 2) == 1

    @pl.when(fast)
    def _():
        def far_pair(c, carry):
            chunk(2 * c, (None, None))
            return carry

        lax.fori_loop(0, jnp.maximum(i - 1, 0) // 2, far_pair, 0)

    @pl.when(fast & (i == 0))
    def _():
        chunk(0, (bias_ref[0, 0],))

    @pl.when(fast & odd)
    def _():
        chunk(i - 1, (bias_ref[0, 1], bias_ref[0, 0]))

    @pl.when(fast & jnp.logical_not(odd) & (i > 0))
    def _():
        chunk(i - 2, (None, bias_ref[0, 1], bias_ref[0, 0]))

    @pl.when(jnp.logical_not(fast))
    def _():
        m_ref[...] = jnp.full(m_ref.shape, NEG, F32)

        def far_unit(u, carry):
            unit_running_max(u, None)
            return carry

        lax.fori_loop(0, jnp.maximum(i - 1, 0), far_unit, 0)

        @pl.when(i >= 1)
        def _():
            unit_running_max(i - 1, bias_ref[0, 1])

        unit_running_max(i, bias_ref[0, 0])

    lv = lamv_ref[...]
    lam = (jnp.exp(jnp.sum(lv[0:1] * lv[1:2], axis=-1, keepdims=True))
           - jnp.exp(jnp.sum(lv[2:3] * lv[3:4], axis=-1, keepdims=True)) + lam_init)
    acc = acc_ref[...]
    o = acc[:DA_VDIM] * (1.0 / acc[DA_VDIM:DA_VDIM + 1])
    ot = o[:, :t] - lam * o[:, t:]
    ms = jnp.mean(ot * ot, axis=0, keepdims=True)
    ot = ot * lax.rsqrt(ms + EPS) * (sg_ref[...] * (1.0 - lam_init))
    o_ref[...] = ot.T.astype(BF16)


def _da_attn(fast, qkv, vt, bias, lamv, sub_gain, batch, seq, lam_init):
    t = DA_T
    nq = seq // t
    return pl.pallas_call(
        functools.partial(_da_attn_kernel, t=t, lam_init=lam_init),
        grid=(batch, DA_HEADS, nq),
        in_specs=[pl.BlockSpec(memory_space=pltpu.SMEM),
                  pl.BlockSpec((t, LANES), lambda b, h, i: (b * nq + i, h)),
                  pl.BlockSpec((seq, LANES), lambda b, h, i: (b, DA_HEADS + h)),
                  pl.BlockSpec((None, None, nq, DA_VROWS, t), lambda b, h, i: (b, h, 0, 0, 0)),
                  pl.BlockSpec((1, 2, t, 2 * t), lambda b, h, i: (h, 0, 0, 0)),
                  pl.BlockSpec((4, HEAD_DIM), lambda b, h, i: (0, 0)),
                  pl.BlockSpec((DA_VDIM, 1), lambda b, h, i: (0, 0))],
        out_specs=pl.BlockSpec((t, LANES), lambda b, h, i: (b * nq + i, h)),
        out_shape=jax.ShapeDtypeStruct((batch * seq, DA_HEADS * DA_VDIM), BF16),
        scratch_shapes=[pltpu.VMEM((DA_VROWS, 2 * t), F32), pltpu.VMEM((1, 2 * t), F32)],
        compiler_params=_cparams("arbitrary", "arbitrary", "arbitrary"),
        name="da_attn",
    )(fast, qkv, qkv, vt, bias, lamv, sub_gain)


def _sw_attn_kernel(sink_ref, q_ref, kp_ref, kc_ref, vp_ref, vc_ref, bias_ref, o_ref, *, tq):
    g = pl.program_id(0)
    i = pl.program_id(2)
    band = SW_WINDOW + tq
    kb = jnp.concatenate([kp_ref[...], kc_ref[...]], axis=0)
    vtb = jnp.concatenate([vp_ref[...], vc_ref[...]], axis=1)
    row = lax.broadcasted_iota(jnp.int32, (band, 1), 0)
    before_start = row < jnp.where(i == 0, SW_WINDOW, 0)
    col = lax.broadcasted_iota(jnp.int32, (1, 2 * tq), 1)
    for pp in range(SW_GROUP // 2):
        sl = slice(pp * LANES, (pp + 1) * LANES)
        head = g * SW_GROUP + 2 * pp
        sink = jnp.where(col < tq, sink_ref[head], sink_ref[head + 1])
        qq = _stack_maps(q_ref[:, sl])
        st = lax.dot_general(kb, qq, (((1,), (1,)), ((), ())), preferred_element_type=F32)
        st = jnp.where(before_start, NEG, st + bias_ref[pp])
        m = jnp.maximum(jnp.max(st, axis=0, keepdims=True), sink)
        e = jnp.exp(st - m)
        denom = jnp.sum(e, axis=0, keepdims=True) + jnp.exp(sink - m)
        ot = jnp.dot(vtb, e.astype(BF16), preferred_element_type=F32) * (1.0 / denom)
        o_ref[:, sl] = jnp.concatenate([ot[:, :tq], ot[:, tq:]], axis=0).T.astype(BF16)


def _sw_attn(q, kdup, vt, bias, sinks, batch, seq):
    tq = SW_TQ
    nq = seq // tq
    r = tq // SW_WINDOW
    band = SW_WINDOW + tq
    gw = SW_GROUP * HEAD_DIM
    return pl.pallas_call(
        functools.partial(_sw_attn_kernel, tq=tq),
        grid=(SW_KV_HEADS, batch, nq),
        in_specs=[pl.BlockSpec(memory_space=pltpu.SMEM),
                  pl.BlockSpec((tq, gw), lambda g, b, i: (b * nq + i, g)),
                  pl.BlockSpec((SW_WINDOW, LANES), lambda g, b, i: (jnp.maximum((b * nq + i) * r - 1, 0), g)),
                  pl.BlockSpec((tq, LANES), lambda g, b, i: (b * nq + i, g)),
                  pl.BlockSpec((None, None, HEAD_DIM, SW_WINDOW), lambda g, b, i: (b, g, 0, jnp.maximum(i * r - 1, 0))),
                  pl.BlockSpec((None, None, HEAD_DIM, tq), lambda g, b, i: (b, g, 0, i)),
                  pl.BlockSpec((SW_GROUP // 2, band, 2 * tq), lambda g, b, i: (g, 0, 0))],
        out_specs=pl.BlockSpec((tq, gw), lambda g, b, i: (b * nq + i, g)),
        out_shape=jax.ShapeDtypeStruct((batch * seq, SW_Q_HEADS * HEAD_DIM), BF16),
        compiler_params=_cparams("arbitrary", "arbitrary", "arbitrary"),
        name="sw_attn",
    )(sinks, q, kdup, kdup, vt, vt, bias)


def _causal_conv(u, halo, cw_ref, cb_ref):
    row = lax.broadcasted_iota(jnp.int32, (u.shape[0], 1), 0)
    u1 = jnp.where(row == 0, halo[7:8], pltpu.roll(u, 1, axis=0))
    u2 = jnp.where(row == 0, halo[6:7], jnp.where(row == 1, halo[7:8], pltpu.roll(u, 2, axis=0)))
    return cw_ref[0:1, :] * u2 + cw_ref[1:2, :] * u1 + cw_ref[2:3, :] * u + cb_ref[...]


def _ffn_kernel(h_ref, g_ref, wg_ref, wu_ref, cwg_ref, cwu_ref, cbg_ref, cbu_ref, wo_ref, o_ref,
                f_ref, ug0_ref, ug1_ref, uu0_ref, uu1_ref, act_ref, halo_g_ref, halo_u_ref, *, tiles_per_seq, nj):
    i = pl.program_id(0)
    j = pl.program_id(1)
    tm = h_ref.shape[0]

    @pl.when((i == 0) & (j == 0))
    def _():
        ug1_ref[...] = jnp.zeros(ug1_ref.shape, F32)
        uu1_ref[...] = jnp.zeros(uu1_ref.shape, F32)

    @pl.when(j == 0)
    def _():
        x = h_ref[...]
        f_ref[...] = _rms_rows(x, g_ref[...]).astype(BF16)
        o_ref[...] = x

    @pl.when((i % tiles_per_seq) == 0)
    def _():
        halo_g_ref[j] = jnp.zeros(halo_g_ref.shape[1:], F32)
        halo_u_ref[j] = jnp.zeros(halo_u_ref.shape[1:], F32)

    def up_and_gate(parity):
        ug_cur, ug_prev = (ug0_ref, ug1_ref)[::1 - 2 * parity]
        uu_cur, uu_prev = (uu0_ref, uu1_ref)[::1 - 2 * parity]

        def conv(u_ref, cw_ref, cb_ref, rows, halo):
            u = u_ref[rows, :]
            return _causal_conv(u, halo, cw_ref, cb_ref), u[FFN_GATE_ROWS - 8:, :]

        halo_g, halo_u = halo_g_ref[j], halo_u_ref[j]
        for r in range(tm // FFN_ROWS):
            rows = slice(r * FFN_ROWS, (r + 1) * FFN_ROWS)
            for q in range(r * FFN_ROWS, (r + 1) * FFN_ROWS, FFN_GATE_ROWS):
                piece = slice(q, q + FFN_GATE_ROWS)
                gate, halo_g = conv(ug_prev, cwg_ref, cbg_ref, piece, halo_g)
                up, halo_u = conv(uu_prev, cwu_ref, cbu_ref, piece, halo_u)
                act_ref[piece, :] = (gate * (1.0 / (1.0 + jnp.exp(-gate))) * up).astype(BF16)
            f = f_ref[rows, :]
            ug_cur[rows, :] = jnp.dot(f, wg_ref[...], preferred_element_type=F32)
            uu_cur[rows, :] = jnp.dot(f, wu_ref[...], preferred_element_type=F32)
        halo_g_ref[j] = halo_g
        halo_u_ref[j] = halo_u

    even = (j % 2) == 0
    pl.when(even)(functools.partial(up_and_gate, 0))
    pl.when(jnp.logical_not(even))(functools.partial(up_and_gate, 1))

    @pl.when(j >= 1)
    def _():
        o_ref[...] += jnp.dot(act_ref[...], wo_ref[...], preferred_element_type=F32)


def _ffn(h, g, w_in, conv_w, conv_b, w_out, seq):
    t, d = h.shape
    dff = w_out.shape[0]
    tm, tf = FFN_TM, FFN_TF
    assert t % tm == 0 and seq % tm == 0 and dff % tf == 0
    nj = dff // tf

    def tile(j):
        return jnp.clip(j, 0, nj - 1)

    return pl.pallas_call(
        functools.partial(_ffn_kernel, tiles_per_seq=seq // tm, nj=nj),
        grid=(t // tm, nj + 1),
        in_specs=[pl.BlockSpec((tm, d), lambda i, j: (i, 0)),
                  pl.BlockSpec((1, d), lambda i, j: (0, 0)),
                  pl.BlockSpec((d, tf), lambda i, j: (0, tile(j))),
                  pl.BlockSpec((d, tf), lambda i, j: (0, nj + tile(j))),
                  pl.BlockSpec((CONV_WIDTH, tf), lambda i, j: (0, tile(j - 1))),
                  pl.BlockSpec((CONV_WIDTH, tf), lambda i, j: (0, nj + tile(j - 1))),
                  pl.BlockSpec((1, tf), lambda i, j: (0, tile(j - 1))),
                  pl.BlockSpec((1, tf), lambda i, j: (0, nj + tile(j - 1))),
                  pl.BlockSpec((tf, d), lambda i, j: (tile(j - 1), 0))],
        out_specs=pl.BlockSpec((tm, d), lambda i, j: (i, 0)),
        out_shape=jax.ShapeDtypeStruct((t, d), F32),
        scratch_shapes=[pltpu.VMEM((tm, d), BF16),
                        pltpu.VMEM((tm, tf), F32), pltpu.VMEM((tm, tf), F32),
                        pltpu.VMEM((tm, tf), F32), pltpu.VMEM((tm, tf), F32),
                        pltpu.VMEM((tm, tf), BF16),
                        pltpu.VMEM((nj + 1, 8, tf), F32),
                        pltpu.VMEM((nj + 1, 8, tf), F32)],
        compiler_params=_cparams("arbitrary", "arbitrary"),
        name="conv_ffn",
    )(h, g, w_in, w_in, conv_w, conv_w, conv_b, conv_b, w_out)


def _row(v):
    return v.reshape(1, -1).astype(F32)


def kernel(x, rel_table, attn_norm_g, ffn_norm_g, da_w_qkv, da_w_o, da_q_gain, da_k_gain, da_lam_q1, da_lam_k1, da_lam_q2, da_lam_k2, da_sub_gain, sw_w_qkv, sw_w_o, sw_q_gain, sw_k_gain, sw_sinks, ffn_w_in, ffn_conv_w, ffn_conv_b, ffn_w_out):
    batch, seq, d = x.shape
    depth = attn_norm_g.shape[0]
    t = batch * seq
    h = x.reshape(t, d)
    q_scale = HEAD_DIM ** -0.5

    da_bias = _da_bias(rel_table, DA_T)
    sw_bias = _sw_bias(rel_table, SW_TQ)

    da_qk = 2 * DA_HEADS * HEAD_DIM
    da_flag = jnp.concatenate([jnp.ones((1, 2 * da_qk), F32), jnp.zeros((1, DA_HEADS * DA_VDIM), F32)], axis=1)
    sw_qw = SW_Q_HEADS * HEAD_DIM
    sw_kw = SW_KV_HEADS * HEAD_DIM
    sw_flag = jnp.concatenate([jnp.ones((1, sw_qw + sw_kw), F32), jnp.zeros((1, sw_kw), F32)], axis=1)

    for i in range(depth):
        j = i // N_MIXERS
        if i % N_MIXERS == 0:
            gain = jnp.concatenate([jnp.tile(da_q_gain[j], da_qk // HEAD_DIM) * (q_scale * LOG2E),
                                    jnp.tile(da_k_gain[j], da_qk // HEAD_DIM),
                                    jnp.ones((DA_HEADS * DA_VDIM,), F32)])
            qkv = _norm_proj(h, _row(attn_norm_g[i]), da_w_qkv[j].astype(BF16), da_flag, _row(gain))
            nk = seq // DA_T
            vt = qkv[:, 2 * da_qk:].reshape(batch, nk, DA_T, DA_HEADS, DA_VDIM).transpose(0, 3, 1, 4, 2)
            ones_rows = jnp.zeros((DA_VROWS - DA_VDIM, DA_T), BF16).at[0].set(1.0)
            vt = jnp.concatenate([vt, jnp.broadcast_to(ones_rows, vt.shape[:3] + ones_rows.shape)], axis=3)
            score_bound = (HEAD_DIM * q_scale * jnp.max(jnp.abs(da_q_gain[j])) * jnp.max(jnp.abs(da_k_gain[j]))
                           + jnp.max(jnp.abs(rel_table - rel_table[NUM_BUCKETS - 1]))) * LOG2E
            fast = (score_bound <= DA_FAST_BOUND).astype(jnp.int32).reshape(1)
            lamv = jnp.stack([da_lam_q1[j], da_lam_k1[j], da_lam_q2[j], da_lam_k2[j]]).astype(F32)
            lam_init = 0.8 - 0.6 * math.exp(-0.3 * i)
            a = _da_attn(fast, qkv, vt, da_bias, lamv, da_sub_gain[j].reshape(DA_VDIM, 1).astype(F32), batch, seq,
                         lam_init)
            h = _out_proj(a, da_w_o[j].astype(BF16), h)
        else:
            gain = jnp.concatenate([jnp.tile(sw_q_gain[j], SW_Q_HEADS) * q_scale,
                                    jnp.tile(sw_k_gain[j], SW_KV_HEADS),
                                    jnp.ones((sw_kw,), F32)])
            qkv = _norm_proj(h, _row(attn_norm_g[i]), sw_w_qkv[j].astype(BF16), sw_flag, _row(gain))
            k = qkv[:, sw_qw:sw_qw + sw_kw].reshape(t, SW_KV_HEADS, 1, HEAD_DIM)
            kdup = jnp.broadcast_to(k, (t, SW_KV_HEADS, 2, HEAD_DIM)).reshape(t, SW_KV_HEADS * LANES)
            vt = qkv[:, sw_qw + sw_kw:].reshape(batch, seq, SW_KV_HEADS, HEAD_DIM).transpose(0, 2, 3, 1)
            a = _sw_attn(qkv, kdup, vt, sw_bias, sw_sinks[j].astype(F32), batch, seq)
            h = _out_proj(a, sw_w_o[j].astype(BF16), h)
        h = _ffn(h, _row(ffn_norm_g[i]), ffn_w_in[i].astype(BF16), ffn_conv_w[i].astype(F32),
                 _row(ffn_conv_b[i]), ffn_w_out[i].astype(BF16), seq)
    return h.reshape(batch, seq, d)
```

```python
import functools
import math

import numpy as np
import jax
import jax.numpy as jnp
from jax import lax
from jax.experimental import pallas as pl
from jax.experimental.pallas import tpu as pltpu

F32 = jnp.float32
BF16 = jnp.bfloat16

HEAD_DIM = 64
N_MAPS = 32
DA_HEADS = 16
DA_VDIM = 128
SW_Q_HEADS = 32
SW_KV_HEADS = 4
SW_GROUP = SW_Q_HEADS // SW_KV_HEADS
SW_WINDOW = 128
NUM_BUCKETS = 32
MAX_DISTANCE = 128
CONV_WIDTH = 3
EPS = 1e-6
N_MIXERS = 2

LANES = 128
NEG = -1e30
LOG2E = math.log2(math.e)
VMEM_LIMIT = 56 * 1024 * 1024

PROJ_TM = 1024
PROJ_TN = 512
PROJ_ROWS = 256
OUT_TM = 1024
OUT_TN = 1024
FFN_TM = 1024
FFN_TF = 256
FFN_ROWS = 512
FFN_GATE_ROWS = 32
DA_T = 512
DA_VROWS = DA_VDIM + 16
DA_FAST_BOUND = 60.0
SW_TQ = 256


def _cparams(*sem):
    return pltpu.CompilerParams(dimension_semantics=sem, vmem_limit_bytes=VMEM_LIMIT)


def _t5_bucket_np(dist):
    max_exact = NUM_BUCKETS // 2
    d = np.maximum(dist, 0)
    df = np.maximum(d, 1).astype(np.float64)
    large = max_exact + (np.log(df / max_exact) / math.log(MAX_DISTANCE / max_exact)
                         * (NUM_BUCKETS - max_exact)).astype(np.int32)
    large = np.minimum(large, NUM_BUCKETS - 1)
    return np.where(d < max_exact, d, large).astype(np.int32)


def _da_bias_kernel(tab_ref, bucket_ref, out_ref, *, t):
    h = pl.program_id(0)
    for off in range(2):
        bk = bucket_ref[off]
        for m in range(2):
            col = 2 * h + m
            far = tab_ref[NUM_BUCKETS - 1, col]
            acc = jnp.zeros(bk.shape, F32)
            for b in range(NUM_BUCKETS - 1):
                acc = jnp.where(bk == b, (tab_ref[b, col] - far) * LOG2E, acc)
            out_ref[0, off, :, m * t:(m + 1) * t] = jnp.where(bk < 0, NEG, acc)


def _da_bias(rel_table, t):
    key = np.arange(t)[:, None]
    qry = np.arange(t)[None, :]
    d0 = qry - key
    d1 = t + qry - key
    bucket = np.stack([np.where(d0 >= 0, _t5_bucket_np(d0), -1), _t5_bucket_np(d1)]).astype(np.int32)
    return pl.pallas_call(
        functools.partial(_da_bias_kernel, t=t),
        grid=(DA_HEADS,),
        in_specs=[pl.BlockSpec(memory_space=pltpu.SMEM),
                  pl.BlockSpec((2, t, t), lambda h: (0, 0, 0))],
        out_specs=pl.BlockSpec((1, 2, t, 2 * t), lambda h: (h, 0, 0, 0)),
        out_shape=jax.ShapeDtypeStruct((DA_HEADS, 2, t, 2 * t), F32),
        compiler_params=_cparams("arbitrary"),
        name="da_bias",
    )(rel_table, jnp.asarray(bucket))


def _sw_bias_kernel(tab_ref, bucket_ref, out_ref, *, tq):
    p = pl.program_id(0)
    bk = bucket_ref[...]
    for m in range(2):
        col = 2 * p + m
        acc = jnp.zeros(bk.shape, F32)
        for b in range(NUM_BUCKETS):
            acc = jnp.where(bk == b, tab_ref[b, col], acc)
        out_ref[0, :, m * tq:(m + 1) * tq] = jnp.where(bk < 0, NEG, acc)


def _sw_bias(rel_table, tq):
    band = SW_WINDOW + tq
    key = np.arange(band)[:, None]
    qry = np.arange(tq)[None, :]
    dist = qry + SW_WINDOW - key
    bucket = np.where((dist >= 0) & (dist < SW_WINDOW), _t5_bucket_np(dist), -1).astype(np.int32)
    return pl.pallas_call(
        functools.partial(_sw_bias_kernel, tq=tq),
        grid=(SW_Q_HEADS // 2,),
        in_specs=[pl.BlockSpec(memory_space=pltpu.SMEM),
                  pl.BlockSpec((band, tq), lambda p: (0, 0))],
        out_specs=pl.BlockSpec((1, band, 2 * tq), lambda p: (p, 0, 0)),
        out_shape=jax.ShapeDtypeStruct((SW_Q_HEADS // 2, band, 2 * tq), F32),
        compiler_params=_cparams("arbitrary"),
        name="sw_bias",
    )(rel_table, jnp.asarray(bucket))


def _rms_rows(x, g):
    ms = jnp.mean(x * x, axis=-1, keepdims=True)
    return x * lax.rsqrt(ms + EPS) * g


def _norm_proj_kernel(x_ref, g_ref, w_ref, flag_ref, gain_ref, o_ref, xn_ref, y0_ref, y1_ref):
    i = pl.program_id(0)
    j = pl.program_id(1)
    tm, tn = y0_ref.shape

    @pl.when((i == 0) & (j == 0))
    def _():
        y1_ref[...] = jnp.zeros(y1_ref.shape, F32)

    @pl.when(j == 0)
    def _():
        xn_ref[...] = _rms_rows(x_ref[...], g_ref[...]).astype(BF16)

    def run(parity):
        y_cur, y_prev = (y0_ref, y1_ref)[::1 - 2 * parity]
        lane = lax.broadcasted_iota(jnp.int32, (1, LANES), 1)
        low = lane < HEAD_DIM
        for r in range(tm // PROJ_ROWS):
            rows = slice(r * PROJ_ROWS, (r + 1) * PROJ_ROWS)
            for c in range(tn // LANES):
                sl = slice(c * LANES, (c + 1) * LANES)
                yc = y_prev[rows, sl]
                y2 = yc * yc
                ms_lo = jnp.sum(jnp.where(low, y2, 0.0), axis=-1, keepdims=True) * (1.0 / HEAD_DIM)
                ms_hi = jnp.sum(jnp.where(low, 0.0, y2), axis=-1, keepdims=True) * (1.0 / HEAD_DIM)
                inv = jnp.where(low, lax.rsqrt(ms_lo + EPS), lax.rsqrt(ms_hi + EPS))
                scale = jnp.where(flag_ref[:, sl] > 0.0, inv, 1.0) * gain_ref[:, sl]
                o_ref[rows, sl] = (yc * scale).astype(BF16)
            y_cur[rows, :] = jnp.dot(xn_ref[rows, :], w_ref[...], preferred_element_type=F32)

    even = (j % 2) == 0
    pl.when(even)(functools.partial(run, 0))
    pl.when(jnp.logical_not(even))(functools.partial(run, 1))


def _norm_proj(x, g, w, flag, gain):
    t, d = x.shape
    n = w.shape[1]
    tm, tn = PROJ_TM, PROJ_TN
    assert t % tm == 0 and n % tn == 0 and tm % PROJ_ROWS == 0
    nj = n // tn

    def tile(j):
        return jnp.clip(j, 0, nj - 1)

    return pl.pallas_call(
        _norm_proj_kernel,
        grid=(t // tm, nj + 1),
        in_specs=[pl.BlockSpec((tm, d), lambda i, j: (i, 0)),
                  pl.BlockSpec((1, d), lambda i, j: (0, 0)),
                  pl.BlockSpec((d, tn), lambda i, j: (0, tile(j))),
                  pl.BlockSpec((1, tn), lambda i, j: (0, tile(j - 1))),
                  pl.BlockSpec((1, tn), lambda i, j: (0, tile(j - 1)))],
        out_specs=pl.BlockSpec((tm, tn), lambda i, j: (i, tile(j - 1))),
        out_shape=jax.ShapeDtypeStruct((t, n), BF16),
        scratch_shapes=[pltpu.VMEM((tm, d), BF16), pltpu.VMEM((tm, tn), F32), pltpu.VMEM((tm, tn), F32)],
        compiler_params=_cparams("arbitrary", "arbitrary"),
        name="norm_proj",
    )(x, g, w, flag, gain)


def _out_proj_kernel(a_ref, w_ref, h_ref, o_ref):
    o_ref[...] = h_ref[...] + jnp.dot(a_ref[...], w_ref[...], preferred_element_type=F32)


def _out_proj(a, w, h):
    t, k = a.shape
    n = w.shape[1]
    tm, tn = OUT_TM, OUT_TN
    assert t % tm == 0 and n % tn == 0
    return pl.pallas_call(
        _out_proj_kernel,
        grid=(t // tm, n // tn),
        in_specs=[pl.BlockSpec((tm, k), lambda i, j: (i, 0)),
                  pl.BlockSpec((k, tn), lambda i, j: (0, j)),
                  pl.BlockSpec((tm, tn), lambda i, j: (i, j))],
        out_specs=pl.BlockSpec((tm, tn), lambda i, j: (i, j)),
        out_shape=jax.ShapeDtypeStruct((t, n), F32),
        compiler_params=_cparams("arbitrary", "arbitrary"),
        name="out_proj",
    )(a, w, h)


def _stack_maps(q):
    lane = lax.broadcasted_iota(jnp.int32, q.shape, 1)
    qf = q.astype(F32)
    return jnp.concatenate([jnp.where(lane < HEAD_DIM, qf, 0.0), jnp.where(lane < HEAD_DIM, 0.0, qf)],
                           axis=0).astype(BF16)


def _da_attn_kernel(fast_ref, q_ref, k_ref, vt_ref, bias_ref, lamv_ref, sg_ref, o_ref, acc_ref, m_ref, *, t, lam_init):
    i = pl.program_id(2)
    qt = q_ref[...].astype(F32).T
    row = lax.broadcasted_iota(jnp.int32, qt.shape, 0)
    qqt = jnp.concatenate([jnp.where(row < HEAD_DIM, qt, 0.0), jnp.where(row < HEAD_DIM, 0.0, qt)],
                          axis=1).astype(BF16)
    acc_ref[...] = jnp.zeros(acc_ref.shape, F32)

    def scores(u, bias):
        k = k_ref[pl.ds(pl.multiple_of(u * t, t), t), :]
        st = jnp.dot(k, qqt, preferred_element_type=F32)
        return st if bias is None else st + bias

    def chunk(u0, biases):
        pv = None
        for n, bias in enumerate(biases):
            p = jnp.exp2(scores(u0 + n, bias)).astype(BF16)
            d = jnp.dot(vt_ref[u0 + n], p, preferred_element_type=F32)
            pv = d if pv is None else pv + d
        acc_ref[...] += pv

    def unit_running_max(u, bias):
        st = scores(u, bias)
        m_prev = m_ref[...]
        m_new = jnp.maximum(m_prev, jnp.max(st, axis=0, keepdims=True))
        p = jnp.exp2(st - m_new).astype(BF16)
        acc_ref[...] = jnp.exp2(m_prev - m_new) * acc_ref[...] + jnp.dot(vt_ref[u], p, preferred_element_type=F32)
        m_ref[...] = m_new

    fast = fast_ref[0] == 1
    odd = (i % 2) == 1

    @pl.when(fast)
    def _():
        def far_pair(c, carry):
            chunk(2 * c, (None, None))
            return carry

        lax.fori_loop(0, jnp.maximum(i - 1, 0) // 2, far_pair, 0)

    @pl.when(fast & (i == 0))
    def _():
        chunk(0, (bias_ref[0, 0],))

    @pl.when(fast & odd)
    def _():
        chunk(i - 1, (bias_ref[0, 1], bias_ref[0, 0]))

    @pl.when(fast & jnp.logical_not(odd) & (i > 0))
    def _():
        chunk(i - 2, (None, bias_ref[0, 1], bias_ref[0, 0]))

    @pl.when(jnp.logical_not(fast))
    def _():
        m_ref[...] = jnp.full(m_ref.shape, NEG, F32)

        def far_unit(u, carry):
            unit_running_max(u, None)
            return carry

        lax.fori_loop(0, jnp.maximum(i - 1, 0), far_unit, 0)

        @pl.when(i >= 1)
        def _():
            unit_running_max(i - 1, bias_ref[0, 1])

        unit_running_max(i, bias_ref[0, 0])

    lv = lamv_ref[...]
    lam = (jnp.exp(jnp.sum(lv[0:1] * lv[1:2], axis=-1, keepdims=True))
           - jnp.exp(jnp.sum(lv[2:3] * lv[3:4], axis=-1, keepdims=True)) + lam_init)
    acc = acc_ref[...]
    o = acc[:DA_VDIM] * (1.0 / acc[DA_VDIM:DA_VDIM + 1])
    ot = o[:, :t] - lam * o[:, t:]
    ms = jnp.mean(ot * ot, axis=0, keepdims=True)
    ot = ot * lax.rsqrt(ms + EPS) * (sg_ref[...] * (1.0 - lam_init))
    o_ref[...] = ot.T.astype(BF16)


def _da_attn(fast, qkv, vt, bias, lamv, sub_gain, batch, seq, lam_init):
    t = DA_T
    nq = seq // t
    return pl.pallas_call(
        functools.partial(_da_attn_kernel, t=t, lam_init=lam_init),
        grid=(batch, DA_HEADS, nq),
        in_specs=[pl.BlockSpec(memory_space=pltpu.SMEM),
                  pl.BlockSpec((t, LANES), lambda b, h, i: (b * nq + i, h)),
                  pl.BlockSpec((seq, LANES), lambda b, h, i: (b, DA_HEADS + h)),
                  pl.BlockSpec((None, None, nq, DA_VROWS, t), lambda b, h, i: (b, h, 0, 0, 0)),
                  pl.BlockSpec((1, 2, t, 2 * t), lambda b, h, i: (h, 0, 0, 0)),
                  pl.BlockSpec((4, HEAD_DIM), lambda b, h, i: (0, 0)),
                  pl.BlockSpec((DA_VDIM, 1), lambda b, h, i: (0, 0))],
        out_specs=pl.BlockSpec((t, LANES), lambda b, h, i: (b * nq + i, h)),
        out_shape=jax.ShapeDtypeStruct((batch * seq, DA_HEADS * DA_VDIM), BF16),
        scratch_shapes=[pltpu.VMEM((DA_VROWS, 2 * t), F32), pltpu.VMEM((1, 2 * t), F32)],
        compiler_params=_cparams("arbitrary", "arbitrary", "arbitrary"),
        name="da_attn",
    )(fast, qkv, qkv, vt, bias, lamv, sub_gain)


def _sw_attn_kernel(sink_ref, q_ref, kp_ref, kc_ref, vp_ref, vc_ref, bias_ref, o_ref, *, tq):
    g = pl.program_id(0)
    i = pl.program_id(2)
    band = SW_WINDOW + tq
    kb = jnp.concatenate([kp_ref[...], kc_ref[...]], axis=0)
    vtb = jnp.concatenate([vp_ref[...], vc_ref[...]], axis=1)
    row = lax.broadcasted_iota(jnp.int32, (band, 1), 0)
    before_start = row < jnp.where(i == 0, SW_WINDOW, 0)
    col = lax.broadcasted_iota(jnp.int32, (1, 2 * tq), 1)
    for pp in range(SW_GROUP // 2):
        sl = slice(pp * LANES, (pp + 1) * LANES)
        head = g * SW_GROUP + 2 * pp
        sink = jnp.where(col < tq, sink_ref[head], sink_ref[head + 1])
        qq = _stack_maps(q_ref[:, sl])
        st = lax.dot_general(kb, qq, (((1,), (1,)), ((), ())), preferred_element_type=F32)
        st = jnp.where(before_start, NEG, st + bias_ref[pp])
        m = jnp.maximum(jnp.max(st, axis=0, keepdims=True), sink)
        e = jnp.exp(st - m)
        denom = jnp.sum(e, axis=0, keepdims=True) + jnp.exp(sink - m)
        ot = jnp.dot(vtb, e.astype(BF16), preferred_element_type=F32) * (1.0 / denom)
        o_ref[:, sl] = jnp.concatenate([ot[:, :tq], ot[:, tq:]], axis=0).T.astype(BF16)


def _sw_attn(q, kdup, vt, bias, sinks, batch, seq):
    tq = SW_TQ
    nq = seq // tq
    r = tq // SW_WINDOW
    band = SW_WINDOW + tq
    gw = SW_GROUP * HEAD_DIM
    return pl.pallas_call(
        functools.partial(_sw_attn_kernel, tq=tq),
        grid=(SW_KV_HEADS, batch, nq),
        in_specs=[pl.BlockSpec(memory_space=pltpu.SMEM),
                  pl.BlockSpec((tq, gw), lambda g, b, i: (b * nq + i, g)),
                  pl.BlockSpec((SW_WINDOW, LANES), lambda g, b, i: (jnp.maximum((b * nq + i) * r - 1, 0), g)),
                  pl.BlockSpec((tq, LANES), lambda g, b, i: (b * nq + i, g)),
                  pl.BlockSpec((None, None, HEAD_DIM, SW_WINDOW), lambda g, b, i: (b, g, 0, jnp.maximum(i * r - 1, 0))),
                  pl.BlockSpec((None, None, HEAD_DIM, tq), lambda g, b, i: (b, g, 0, i)),
                  pl.BlockSpec((SW_GROUP // 2, band, 2 * tq), lambda g, b, i: (g, 0, 0))],
        out_specs=pl.BlockSpec((tq, gw), lambda g, b, i: (b * nq + i, g)),
        out_shape=jax.ShapeDtypeStruct((batch * seq, SW_Q_HEADS * HEAD_DIM), BF16),
        compiler_params=_cparams("arbitrary", "arbitrary", "arbitrary"),
        name="sw_attn",
    )(sinks, q, kdup, kdup, vt, vt, bias)


def _causal_conv(u, halo, cw_ref, cb_ref):
    row = lax.broadcasted_iota(jnp.int32, (u.shape[0], 1), 0)
    u1 = jnp.where(row == 0, halo[7:8], pltpu.roll(u, 1, axis=0))
    u2 = jnp.where(row == 0, halo[6:7], jnp.where(row == 1, halo[7:8], pltpu.roll(u, 2, axis=0)))
    return cw_ref[0:1, :] * u2 + cw_ref[1:2, :] * u1 + cw_ref[2:3, :] * u + cb_ref[...]


def _ffn_kernel(h_ref, g_ref, wg_ref, wu_ref, cwg_ref, cwu_ref, cbg_ref, cbu_ref, wo_ref, o_ref,
                f_ref, ug0_ref, ug1_ref, uu0_ref, uu1_ref, act_ref, halo_g_ref, halo_u_ref, *, tiles_per_seq, nj):
    i = pl.program_id(0)
    j = pl.program_id(1)
    tm = h_ref.shape[0]

    @pl.when((i == 0) & (j == 0))
    def _():
        ug1_ref[...] = jnp.zeros(ug1_ref.shape, F32)
        uu1_ref[...] = jnp.zeros(uu1_ref.shape, F32)

    @pl.when(j == 0)
    def _():
        x = h_ref[...]
        f_ref[...] = _rms_rows(x, g_ref[...]).astype(BF16)
        o_ref[...] = x

    @pl.when((i % tiles_per_seq) == 0)
    def _():
        halo_g_ref[j] = jnp.zeros(halo_g_ref.shape[1:], F32)
        halo_u_ref[j] = jnp.zeros(halo_u_ref.shape[1:], F32)

    def up_and_gate(parity):
        ug_cur, ug_prev = (ug0_ref, ug1_ref)[::1 - 2 * parity]
        uu_cur, uu_prev = (uu0_ref, uu1_ref)[::1 - 2 * parity]

        def conv(u_ref, cw_ref, cb_ref, rows, halo):
            u = u_ref[rows, :]
            return _causal_conv(u, halo, cw_ref, cb_ref), u[FFN_GATE_ROWS - 8:, :]

        halo_g, halo_u = halo_g_ref[j], halo_u_ref[j]
        for r in range(tm // FFN_ROWS):
            rows = slice(r * FFN_ROWS, (r + 1) * FFN_ROWS)
            for q in range(r * FFN_ROWS, (r + 1) * FFN_ROWS, FFN_GATE_ROWS):
                piece = slice(q, q + FFN_GATE_ROWS)
                gate, halo_g = conv(ug_prev, cwg_ref, cbg_ref, piece, halo_g)
                up, halo_u = conv(uu_prev, cwu_ref, cbu_ref, piece, halo_u)
                act_ref[piece, :] = (gate * (1.0 / (1.0 + jnp.exp(-gate))) * up).astype(BF16)
            f = f_ref[rows, :]
            ug_cur[rows, :] = jnp.dot(f, wg_ref[...], preferred_element_type=F32)
            uu_cur[rows, :] = jnp.dot(f, wu_ref[...], preferred_element_type=F32)
        halo_g_ref[j] = halo_g
        halo_u_ref[j] = halo_u

    even = (j % 2) == 0
    pl.when(even)(functools.partial(up_and_gate, 0))
    pl.when(jnp.logical_not(even))(functools.partial(up_and_gate, 1))

    @pl.when(j >= 1)
    def _():
        o_ref[...] += jnp.dot(act_ref[...], wo_ref[...], preferred_element_type=F32)


def _ffn(h, g, w_in, conv_w, conv_b, w_out, seq):
    t, d = h.shape
    dff = w_out.shape[0]
    tm, tf = FFN_TM, FFN_TF
    assert t % tm == 0 and seq % tm == 0 and dff % tf == 0
    nj = dff // tf

    def tile(j):
        return jnp.clip(j, 0, nj - 1)

    return pl.pallas_call(
        functools.partial(_ffn_kernel, tiles_per_seq=seq // tm, nj=nj),
        grid=(t // tm, nj + 1),
        in_specs=[pl.BlockSpec((tm, d), lambda i, j: (i, 0)),
                  pl.BlockSpec((1, d), lambda i, j: (0, 0)),
                  pl.BlockSpec((d, tf), lambda i, j: (0, tile(j))),
                  pl.BlockSpec((d, tf), lambda i, j: (0, nj + tile(j))),
                  pl.BlockSpec((CONV_WIDTH, tf), lambda i, j: (0, tile(j - 1))),
                  pl.BlockSpec((CONV_WIDTH, tf), lambda i, j: (0, nj + tile(j - 1))),
                  pl.BlockSpec((1, tf), lambda i, j: (0, tile(j - 1))),
                  pl.BlockSpec((1, tf), lambda i, j: (0, nj + tile(j - 1))),
                  pl.BlockSpec((tf, d), lambda i, j: (tile(j - 1), 0))],
        out_specs=pl.BlockSpec((tm, d), lambda i, j: (i, 0)),
        out_shape=jax.ShapeDtypeStruct((t, d), F32),
        scratch_shapes=[pltpu.VMEM((tm, d), BF16),
                        pltpu.VMEM((tm, tf), F32), pltpu.VMEM((tm, tf), F32),
                        pltpu.VMEM((tm, tf), F32), pltpu.VMEM((tm, tf), F32),
                        pltpu.VMEM((tm, tf), BF16),
                        pltpu.VMEM((nj + 1, 8, tf), F32),
                        pltpu.VMEM((nj + 1, 8, tf), F32)],
        compiler_params=_cparams("arbitrary", "arbitrary"),
        name="conv_ffn",
    )(h, g, w_in, w_in, conv_w, conv_w, conv_b, conv_b, w_out)


def _row(v):
    return v.reshape(1, -1).astype(F32)


def kernel(x, rel_table, attn_norm_g, ffn_norm_g, da_w_qkv, da_w_o, da_q_gain, da_k_gain, da_lam_q1, da_lam_k1, da_lam_q2, da_lam_k2, da_sub_gain, sw_w_qkv, sw_w_o, sw_q_gain, sw_k_gain, sw_sinks, ffn_w_in, ffn_conv_w, ffn_conv_b, ffn_w_out):
    batch, seq, d = x.shape
    depth = attn_norm_g.shape[0]
    t = batch * seq
    h = x.reshape(t, d)
    q_scale = HEAD_DIM ** -0.5

    da_bias = _da_bias(rel_table, DA_T)
    sw_bias = _sw_bias(rel_table, SW_TQ)

    da_qk = 2 * DA_HEADS * HEAD_DIM
    da_flag = jnp.concatenate([jnp.ones((1, 2 * da_qk), F32), jnp.zeros((1, DA_HEADS * DA_VDIM), F32)], axis=1)
    sw_qw = SW_Q_HEADS * HEAD_DIM
    sw_kw = SW_KV_HEADS * HEAD_DIM
    sw_flag = jnp.concatenate([jnp.ones((1, sw_qw + sw_kw), F32), jnp.zeros((1, sw_kw), F32)], axis=1)

    for i in range(depth):
        j = i // N_MIXERS
        if i % N_MIXERS == 0:
            gain = jnp.concatenate([jnp.tile(da_q_gain[j], da_qk // HEAD_DIM) * (q_scale * LOG2E),
                                    jnp.tile(da_k_gain[j], da_qk // HEAD_DIM),
                                    jnp.ones((DA_HEADS * DA_VDIM,), F32)])
            qkv = _norm_proj(h, _row(attn_norm_g[i]), da_w_qkv[j].astype(BF16), da_flag, _row(gain))
            nk = seq // DA_T
            vt = qkv[:, 2 * da_qk:].reshape(batch, nk, DA_T, DA_HEADS, DA_VDIM).transpose(0, 3, 1, 4, 2)
            ones_rows = jnp.zeros((DA_VROWS - DA_VDIM, DA_T), BF16).at[0].set(1.0)
            vt = jnp.concatenate([vt, jnp.broadcast_to(ones_rows, vt.shape[:3] + ones_rows.shape)], axis=3)
            score_bound = (HEAD_DIM * q_scale * jnp.max(jnp.abs(da_q_gain[j])) * jnp.max(jnp.abs(da_k_gain[j]))
                           + jnp.max(jnp.abs(rel_table - rel_table[NUM_BUCKETS - 1]))) * LOG2E
            fast = (score_bound <= DA_FAST_BOUND).astype(jnp.int32).reshape(1)
            lamv = jnp.stack([da_lam_q1[j], da_lam_k1[j], da_lam_q2[j], da_lam_k2[j]]).astype(F32)
            lam_init = 0.8 - 0.6 * math.exp(-0.3 * i)
            a = _da_attn(fast, qkv, vt, da_bias, lamv, da_sub_gain[j].reshape(DA_VDIM, 1).astype(F32), batch, seq,
                         lam_init)
            h = _out_proj(a, da_w_o[j].astype(BF16), h)
        else:
            gain = jnp.concatenate([jnp.tile(sw_q_gain[j], SW_Q_HEADS) * q_scale,
                                    jnp.tile(sw_k_gain[j], SW_KV_HEADS),
                                    jnp.ones((sw_kw,), F32)])
            qkv = _norm_proj(h, _row(attn_norm_g[i]), sw_w_qkv[j].astype(BF16), sw_flag, _row(gain))
            k = qkv[:, sw_qw:sw_qw + sw_kw].reshape(t, SW_KV_HEADS, 1, HEAD_DIM)
            kdup = jnp.broadcast_to(k, (t, SW_KV_HEADS, 2, HEAD_DIM)).reshape(t, SW_KV_HEADS * LANES)
            vt = qkv[:, sw_qw + sw_kw:].reshape(batch, seq, SW_KV_HEADS, HEAD_DIM).transpose(0, 2, 3, 1)
            a = _sw_attn(qkv, kdup, vt, sw_bias, sw_sinks[j].astype(F32), batch, seq)
            h = _out_proj(a, sw_w_o[j].astype(BF16), h)
        h = _ffn(h, _row(ffn_norm_g[i]), ffn_w_in[i].astype(BF16), ffn_conv_w[i].astype(F32),
                 _row(ffn_conv_b[i]), ffn_w_out[i].astype(BF16), seq)
    return h.reshape(batch, seq, d)
```

```python
import functools
import math

import numpy as np
import jax
import jax.numpy as jnp
from jax import lax
from jax.experimental import pallas as pl
from jax.experimental.pallas import tpu as pltpu

F32 = jnp.float32
BF16 = jnp.bfloat16

HEAD_DIM = 64
N_MAPS = 32
DA_HEADS = 16
DA_VDIM = 128
SW_Q_HEADS = 32
SW_KV_HEADS = 4
SW_GROUP = SW_Q_HEADS // SW_KV_HEADS
SW_WINDOW = 128
NUM_BUCKETS = 32
MAX_DISTANCE = 128
CONV_WIDTH = 3
EPS = 1e-6
N_MIXERS = 2

LANES = 128
NEG = -1e30
LOG2E = math.log2(math.e)
VMEM_LIMIT = 56 * 1024 * 1024

PROJ_TM = 1024
PROJ_TN = 512
PROJ_ROWS = 256
OUT_TM = 1024
OUT_TN = 1024
FFN_TM = 512
FFN_TF = 512
DA_T = 512
DA_VROWS = DA_VDIM + 16
FAST_BOUND = 60.0
SW_TQ = 256
SW_VROWS = HEAD_DIM + 16


def _cparams(*sem):
    return pltpu.CompilerParams(dimension_semantics=sem, vmem_limit_bytes=VMEM_LIMIT)


def _t5_bucket_np(dist):
    max_exact = NUM_BUCKETS // 2
    d = np.maximum(dist, 0)
    df = np.maximum(d, 1).astype(np.float64)
    large = max_exact + (np.log(df / max_exact) / math.log(MAX_DISTANCE / max_exact)
                         * (NUM_BUCKETS - max_exact)).astype(np.int32)
    large = np.minimum(large, NUM_BUCKETS - 1)
    return np.where(d < max_exact, d, large).astype(np.int32)


def _da_bias_kernel(tab_ref, bucket_ref, out_ref, *, t):
    h = pl.program_id(0)
    s = MAX_DISTANCE
    for off in range(2):
        for m in range(2):
            col = 2 * h + m
            far = tab_ref[NUM_BUCKETS - 1, col]
            for kr in range(t // s):
                for qc in range(t // s):
                    base = off * t + (qc - kr) * s
                    dst = (0, off, slice(kr * s, (kr + 1) * s), slice(m * t + qc * s, m * t + (qc + 1) * s))
                    if base - (s - 1) >= MAX_DISTANCE:
                        out_ref[dst] = jnp.zeros((s, s), F32)
                    elif base + (s - 1) < 0:
                        out_ref[dst] = jnp.full((s, s), NEG, F32)
                    else:
                        bk = bucket_ref[off, kr * s:(kr + 1) * s, qc * s:(qc + 1) * s]
                        acc = jnp.zeros(bk.shape, F32)
                        for b in range(NUM_BUCKETS - 1):
                            acc = jnp.where(bk == b, (tab_ref[b, col] - far) * LOG2E, acc)
                        out_ref[dst] = jnp.where(bk < 0, NEG, acc)


def _da_bias(rel_table, t):
    key = np.arange(t)[:, None]
    qry = np.arange(t)[None, :]
    d0 = qry - key
    d1 = t + qry - key
    bucket = np.stack([np.where(d0 >= 0, _t5_bucket_np(d0), -1), _t5_bucket_np(d1)]).astype(np.int32)
    return pl.pallas_call(
        functools.partial(_da_bias_kernel, t=t),
        grid=(DA_HEADS,),
        in_specs=[pl.BlockSpec(memory_space=pltpu.SMEM),
                  pl.BlockSpec((2, t, t), lambda h: (0, 0, 0))],
        out_specs=pl.BlockSpec((1, 2, t, 2 * t), lambda h: (h, 0, 0, 0)),
        out_shape=jax.ShapeDtypeStruct((DA_HEADS, 2, t, 2 * t), F32),
        compiler_params=_cparams("arbitrary"),
        name="da_bias",
    )(rel_table, jnp.asarray(bucket))


def _sw_bias_kernel(tab_ref, bucket_ref, out_ref, *, tq):
    p = pl.program_id(1)
    bk = bucket_ref[0]
    for m in range(2):
        col = 2 * p + m
        acc = jnp.zeros(bk.shape, F32)
        for b in range(NUM_BUCKETS):
            acc = jnp.where(bk == b, tab_ref[b, col] * LOG2E, acc)
        out_ref[0, 0, :, m * tq:(m + 1) * tq] = jnp.where(bk < 0, NEG, acc)


def _sw_bias(rel_table, tq):
    band = SW_WINDOW + tq
    key = np.arange(band)[:, None]
    qry = np.arange(tq)[None, :]
    dist = qry + SW_WINDOW - key
    bucket = np.where((dist >= 0) & (dist < SW_WINDOW), _t5_bucket_np(dist), -1).astype(np.int32)
    first = np.where(key < SW_WINDOW, -1, bucket).astype(np.int32)
    return pl.pallas_call(
        functools.partial(_sw_bias_kernel, tq=tq),
        grid=(2, SW_Q_HEADS // 2),
        in_specs=[pl.BlockSpec(memory_space=pltpu.SMEM),
                  pl.BlockSpec((1, band, tq), lambda v, p: (v, 0, 0))],
        out_specs=pl.BlockSpec((1, 1, band, 2 * tq), lambda v, p: (v, p, 0, 0)),
        out_shape=jax.ShapeDtypeStruct((2, SW_Q_HEADS // 2, band, 2 * tq), F32),
        compiler_params=_cparams("arbitrary", "arbitrary"),
        name="sw_bias",
    )(rel_table, jnp.asarray(np.stack([bucket, first])))


def _rms_rows(x, g):
    ms = jnp.mean(x * x, axis=-1, keepdims=True)
    return x * lax.rsqrt(ms + EPS) * g


def _norm_proj_kernel(x_ref, g_ref, w_ref, flag_ref, gain_ref, o_ref, xn_ref, y0_ref, y1_ref):
    i = pl.program_id(0)
    j = pl.program_id(1)
    tm, tn = y0_ref.shape

    @pl.when((i == 0) & (j == 0))
    def _():
        y1_ref[...] = jnp.zeros(y1_ref.shape, F32)

    @pl.when(j == 0)
    def _():
        xn_ref[...] = _rms_rows(x_ref[...], g_ref[...]).astype(BF16)

    def run(parity):
        y_cur, y_prev = (y0_ref, y1_ref)[::1 - 2 * parity]
        lane = lax.broadcasted_iota(jnp.int32, (1, LANES), 1)
        low = lane < HEAD_DIM
        for r in range(tm // PROJ_ROWS):
            rows = slice(r * PROJ_ROWS, (r + 1) * PROJ_ROWS)
            for c in range(tn // LANES):
                sl = slice(c * LANES, (c + 1) * LANES)
                yc = y_prev[rows, sl]
                y2 = yc * yc
                ms_lo = jnp.sum(jnp.where(low, y2, 0.0), axis=-1, keepdims=True) * (1.0 / HEAD_DIM)
                ms_hi = jnp.sum(jnp.where(low, 0.0, y2), axis=-1, keepdims=True) * (1.0 / HEAD_DIM)
                inv = jnp.where(low, lax.rsqrt(ms_lo + EPS), lax.rsqrt(ms_hi + EPS))
                scale = jnp.where(flag_ref[:, sl] > 0.0, inv, 1.0) * gain_ref[:, sl]
                o_ref[rows, sl] = (yc * scale).astype(BF16)
            y_cur[rows, :] = jnp.dot(xn_ref[rows, :], w_ref[...], preferred_element_type=F32)

    even = (j % 2) == 0
    pl.when(even)(functools.partial(run, 0))
    pl.when(jnp.logical_not(even))(functools.partial(run, 1))


def _norm_proj(x, g, w, flag, gain):
    t, d = x.shape
    n = w.shape[1]
    tm, tn = PROJ_TM, PROJ_TN
    assert t % tm == 0 and n % tn == 0 and tm % PROJ_ROWS == 0
    nj = n // tn

    def tile(j):
        return jnp.clip(j, 0, nj - 1)

    return pl.pallas_call(
        _norm_proj_kernel,
        grid=(t // tm, nj + 1),
        in_specs=[pl.BlockSpec((tm, d), lambda i, j: (i, 0)),
                  pl.BlockSpec((1, d), lambda i, j: (0, 0)),
                  pl.BlockSpec((d, tn), lambda i, j: (0, tile(j))),
                  pl.BlockSpec((1, tn), lambda i, j: (0, tile(j - 1))),
                  pl.BlockSpec((1, tn), lambda i, j: (0, tile(j - 1)))],
        out_specs=pl.BlockSpec((tm, tn), lambda i, j: (i, tile(j - 1))),
        out_shape=jax.ShapeDtypeStruct((t, n), BF16),
        scratch_shapes=[pltpu.VMEM((tm, d), BF16), pltpu.VMEM((tm, tn), F32), pltpu.VMEM((tm, tn), F32)],
        compiler_params=_cparams("arbitrary", "arbitrary"),
        name="norm_proj",
    )(x, g, w, flag, gain)


def _out_proj_kernel(a_ref, w_ref, h_ref, o_ref):
    o_ref[...] = h_ref[...] + jnp.dot(a_ref[...], w_ref[...], preferred_element_type=F32)


def _out_proj(a, w, h):
    t, k = a.shape
    n = w.shape[1]
    tm, tn = OUT_TM, OUT_TN
    assert t % tm == 0 and n % tn == 0
    return pl.pallas_call(
        _out_proj_kernel,
        grid=(t // tm, n // tn),
        in_specs=[pl.BlockSpec((tm, k), lambda i, j: (i, 0)),
                  pl.BlockSpec((k, tn), lambda i, j: (0, j)),
                  pl.BlockSpec((tm, tn), lambda i, j: (i, j))],
        out_specs=pl.BlockSpec((tm, tn), lambda i, j: (i, j)),
        out_shape=jax.ShapeDtypeStruct((t, n), F32),
        compiler_params=_cparams("arbitrary", "arbitrary"),
        name="out_proj",
    )(a, w, h)


def _stack_maps(q):
    lane = lax.broadcasted_iota(jnp.int32, q.shape, 1)
    qf = q.astype(F32)
    return jnp.concatenate([jnp.where(lane < HEAD_DIM, qf, 0.0), jnp.where(lane < HEAD_DIM, 0.0, qf)],
                           axis=0).astype(BF16)


def _da_attn_kernel(fast_ref, q_ref, k_ref, vt_ref, bias_ref, lamv_ref, sg_ref, o_ref, acc_ref, m_ref, *, t, lam_init):
    i = pl.program_id(2)
    qt = q_ref[...].astype(F32).T
    row = lax.broadcasted_iota(jnp.int32, qt.shape, 0)
    qqt = jnp.concatenate([jnp.where(row < HEAD_DIM, qt, 0.0), jnp.where(row < HEAD_DIM, 0.0, qt)],
                          axis=1).astype(BF16)
    acc_ref[...] = jnp.zeros(acc_ref.shape, F32)

    def scores(u, bias):
        k = k_ref[pl.ds(pl.multiple_of(u * t, t), t), :]
        st = jnp.dot(k, qqt, preferred_element_type=F32)
        return st if bias is None else st + bias

    def chunk(u0, biases):
        pv = None
        for n, bias in enumerate(biases):
            p = jnp.exp2(scores(u0 + n, bias)).astype(BF16)
            d = jnp.dot(vt_ref[u0 + n], p, preferred_element_type=F32)
            pv = d if pv is None else pv + d
        acc_ref[...] += pv

    def unit_running_max(u, bias):
        st = scores(u, bias)
        m_prev = m_ref[...]
        m_new = jnp.maximum(m_prev, jnp.max(st, axis=0, keepdims=True))
        p = jnp.exp2(st - m_new).astype(BF16)
        acc_ref[...] = jnp.exp2(m_prev - m_new) * acc_ref[...] + jnp.dot(vt_ref[u], p, preferred_element_type=F32)
        m_ref[...] = m_new

    fast = fast_ref[0] == 1
    odd = (i % 2) == 1

    @pl.when(fast)
    def _():
        def far_pair(c, carry):
            chunk(2 * c, (None, None))
            return carry

        lax.fori_loop(0, jnp.maximum(i - 1, 0) // 2, far_pair, 0)

    @pl.when(fast & (i == 0))
    def _():
        chunk(0, (bias_ref[0, 0],))

    @pl.when(fast & odd)
    def _():
        chunk(i - 1, (bias_ref[0, 1], bias_ref[0, 0]))

    @pl.when(fast & jnp.logical_not(odd) & (i > 0))
    def _():
        chunk(i - 2, (None, bias_ref[0, 1], bias_ref[0, 0]))

    @pl.when(jnp.logical_not(fast))
    def _():
        m_ref[...] = jnp.full(m_ref.shape, NEG, F32)

        def far_unit(u, carry):
            unit_running_max(u, None)
            return carry

        lax.fori_loop(0, jnp.maximum(i - 1, 0), far_unit, 0)

        @pl.when(i >= 1)
        def _():
            unit_running_max(i - 1, bias_ref[0, 1])

        unit_running_max(i, bias_ref[0, 0])

    lv = lamv_ref[...]
    lam = (jnp.exp(jnp.sum(lv[0:1] * lv[1:2], axis=-1, keepdims=True))
           - jnp.exp(jnp.sum(lv[2:3] * lv[3:4], axis=-1, keepdims=True)) + lam_init)
    acc = acc_ref[...]
    o = acc[:DA_VDIM] * (1.0 / acc[DA_VDIM:DA_VDIM + 1])
    ot = o[:, :t] - lam * o[:, t:]
    ms = jnp.mean(ot * ot, axis=0, keepdims=True)
    ot = ot * lax.rsqrt(ms + EPS) * (sg_ref[...] * (1.0 - lam_init))
    o_ref[...] = ot.T.astype(BF16)


def _da_attn(fast, qkv, vt, bias, lamv, sub_gain, batch, seq, lam_init):
    t = DA_T
    nq = seq // t
    return pl.pallas_call(
        functools.partial(_da_attn_kernel, t=t, lam_init=lam_init),
        grid=(batch, DA_HEADS, nq),
        in_specs=[pl.BlockSpec(memory_space=pltpu.SMEM),
                  pl.BlockSpec((t, LANES), lambda b, h, i: (b * nq + i, h)),
                  pl.BlockSpec((seq, LANES), lambda b, h, i: (b, DA_HEADS + h)),
                  pl.BlockSpec((None, None, nq, DA_VROWS, t), lambda b, h, i: (b, h, 0, 0, 0)),
                  pl.BlockSpec((1, 2, t, 2 * t), lambda b, h, i: (h, 0, 0, 0)),
                  pl.BlockSpec((4, HEAD_DIM), lambda b, h, i: (0, 0)),
                  pl.BlockSpec((DA_VDIM, 1), lambda b, h, i: (0, 0))],
        out_specs=pl.BlockSpec((t, LANES), lambda b, h, i: (b * nq + i, h)),
        out_shape=jax.ShapeDtypeStruct((batch * seq, DA_HEADS * DA_VDIM), BF16),
        scratch_shapes=[pltpu.VMEM((DA_VROWS, 2 * t), F32), pltpu.VMEM((1, 2 * t), F32)],
        compiler_params=_cparams("arbitrary", "arbitrary", "arbitrary"),
        name="da_attn",
    )(fast, qkv, qkv, vt, bias, lamv, sub_gain)


def _sw_attn_kernel(fast_ref, sink_ref, q_ref, kp_ref, kc_ref, vp_ref, vc_ref, bias_ref, o_ref, *, tq):
    g = pl.program_id(0)
    kb = jnp.concatenate([kp_ref[...], kc_ref[...]], axis=0)
    vtb = jnp.concatenate([vp_ref[...], vc_ref[...]], axis=1)
    col = lax.broadcasted_iota(jnp.int32, (1, 2 * tq), 1)

    def head_pair(pp, fast):
        sl = slice(pp * LANES, (pp + 1) * LANES)
        head = g * SW_GROUP + 2 * pp
        sink = jnp.where(col < tq, sink_ref[head], sink_ref[head + 1]) * LOG2E
        qt = q_ref[:, sl].astype(F32).T
        row = lax.broadcasted_iota(jnp.int32, qt.shape, 0)
        qqt = jnp.concatenate([jnp.where(row < HEAD_DIM, qt, 0.0), jnp.where(row < HEAD_DIM, 0.0, qt)],
                              axis=1).astype(BF16)
        st = jnp.dot(kb, qqt, preferred_element_type=F32) + bias_ref[0, pp]
        if fast:
            ot = jnp.dot(vtb, jnp.exp2(st).astype(BF16), preferred_element_type=F32)
            denom = ot[HEAD_DIM:HEAD_DIM + 1] + jnp.exp2(sink)
        else:
            m = jnp.maximum(jnp.max(st, axis=0, keepdims=True), sink)
            ot = jnp.dot(vtb, jnp.exp2(st - m).astype(BF16), preferred_element_type=F32)
            denom = ot[HEAD_DIM:HEAD_DIM + 1] + jnp.exp2(sink - m)
        ot = ot[:HEAD_DIM] * (1.0 / denom)
        o_ref[:, sl] = jnp.concatenate([ot[:, :tq], ot[:, tq:]], axis=0).T.astype(BF16)

    @pl.when(fast_ref[0] == 1)
    def _():
        for pp in range(SW_GROUP // 2):
            head_pair(pp, True)

    @pl.when(fast_ref[0] != 1)
    def _():
        for pp in range(SW_GROUP // 2):
            head_pair(pp, False)


def _sw_attn(fast, q, kdup, vt, bias, sinks, batch, seq):
    tq = SW_TQ
    nq = seq // tq
    r = tq // SW_WINDOW
    band = SW_WINDOW + tq
    gw = SW_GROUP * HEAD_DIM
    return pl.pallas_call(
        functools.partial(_sw_attn_kernel, tq=tq),
        grid=(SW_KV_HEADS, batch, nq),
        in_specs=[pl.BlockSpec(memory_space=pltpu.SMEM),
                  pl.BlockSpec(memory_space=pltpu.SMEM),
                  pl.BlockSpec((tq, gw), lambda g, b, i: (b * nq + i, g)),
                  pl.BlockSpec((SW_WINDOW, LANES), lambda g, b, i: (jnp.maximum((b * nq + i) * r - 1, 0), g)),
                  pl.BlockSpec((tq, LANES), lambda g, b, i: (b * nq + i, g)),
                  pl.BlockSpec((None, None, SW_VROWS, SW_WINDOW), lambda g, b, i: (b, g, 0, jnp.maximum(i * r - 1, 0))),
                  pl.BlockSpec((None, None, SW_VROWS, tq), lambda g, b, i: (b, g, 0, i)),
                  pl.BlockSpec((1, SW_GROUP // 2, band, 2 * tq), lambda g, b, i: (jnp.where(i == 0, 1, 0), g, 0, 0))],
        out_specs=pl.BlockSpec((tq, gw), lambda g, b, i: (b * nq + i, g)),
        out_shape=jax.ShapeDtypeStruct((batch * seq, SW_Q_HEADS * HEAD_DIM), BF16),
        compiler_params=_cparams("arbitrary", "arbitrary", "arbitrary"),
        name="sw_attn",
    )(fast, sinks, q, kdup, kdup, vt, vt, bias)


def _causal_conv(u, halo, cw_ref, cb_ref):
    row = lax.broadcasted_iota(jnp.int32, (u.shape[0], 1), 0)
    u1 = jnp.where(row == 0, halo[7:8], pltpu.roll(u, 1, axis=0))
    u2 = jnp.where(row == 0, halo[6:7], jnp.where(row == 1, halo[7:8], pltpu.roll(u, 2, axis=0)))
    return cw_ref[0:1, :] * u2 + cw_ref[1:2, :] * u1 + cw_ref[2:3, :] * u + cb_ref[...]


def _ffn_kernel(h_ref, g_ref, wg_ref, wu_ref, cwg_ref, cwu_ref, cbg_ref, cbu_ref, wo_ref, o_ref,
                f_ref, halo_g_ref, halo_u_ref, *, tiles_per_seq):
    i = pl.program_id(0)
    j = pl.program_id(1)
    tm = h_ref.shape[0]

    @pl.when(j == 0)
    def _():
        x = h_ref[...]
        f_ref[...] = _rms_rows(x, g_ref[...]).astype(BF16)
        o_ref[...] = x

    f = f_ref[...]
    seq_start = (i % tiles_per_seq) == 0

    def branch(w_ref, cw_ref, cb_ref, halo_ref):
        u = jnp.dot(f, w_ref[...], preferred_element_type=F32)

        @pl.when(seq_start)
        def _():
            halo_ref[j] = jnp.zeros(halo_ref.shape[1:], F32)

        halo = halo_ref[j]
        halo_ref[j] = u[tm - 8:, :]
        return _causal_conv(u, halo, cw_ref, cb_ref)

    gate = branch(wg_ref, cwg_ref, cbg_ref, halo_g_ref)
    up = branch(wu_ref, cwu_ref, cbu_ref, halo_u_ref)
    act = (gate * (1.0 / (1.0 + jnp.exp(-gate))) * up).astype(BF16)
    o_ref[...] += jnp.dot(act, wo_ref[...], preferred_element_type=F32)


def _ffn(h, g, w_in, conv_w, conv_b, w_out, seq):
    t, d = h.shape
    dff = w_out.shape[0]
    tm, tf = FFN_TM, FFN_TF
    assert t % tm == 0 and seq % tm == 0 and dff % tf == 0
    nj = dff // tf
    return pl.pallas_call(
        functools.partial(_ffn_kernel, tiles_per_seq=seq // tm),
        grid=(t // tm, nj),
        in_specs=[pl.BlockSpec((tm, d), lambda i, j: (i, 0)),
                  pl.BlockSpec((1, d), lambda i, j: (0, 0)),
                  pl.BlockSpec((d, tf), lambda i, j: (0, j)),
                  pl.BlockSpec((d, tf), lambda i, j: (0, nj + j)),
                  pl.BlockSpec((CONV_WIDTH, tf), lambda i, j: (0, j)),
                  pl.BlockSpec((CONV_WIDTH, tf), lambda i, j: (0, nj + j)),
                  pl.BlockSpec((1, tf), lambda i, j: (0, j)),
                  pl.BlockSpec((1, tf), lambda i, j: (0, nj + j)),
                  pl.BlockSpec((tf, d), lambda i, j: (j, 0))],
        out_specs=pl.BlockSpec((tm, d), lambda i, j: (i, 0)),
        out_shape=jax.ShapeDtypeStruct((t, d), F32),
        scratch_shapes=[pltpu.VMEM((tm, d), BF16),
                        pltpu.VMEM((nj, 8, tf), F32),
                        pltpu.VMEM((nj, 8, tf), F32)],
        compiler_params=_cparams("arbitrary", "arbitrary"),
        name="conv_ffn",
    )(h, g, w_in, w_in, conv_w, conv_w, conv_b, conv_b, w_out)


def _row(v):
    return v.reshape(1, -1).astype(F32)


def kernel(x, rel_table, attn_norm_g, ffn_norm_g, da_w_qkv, da_w_o, da_q_gain, da_k_gain, da_lam_q1, da_lam_k1, da_lam_q2, da_lam_k2, da_sub_gain, sw_w_qkv, sw_w_o, sw_q_gain, sw_k_gain, sw_sinks, ffn_w_in, ffn_conv_w, ffn_conv_b, ffn_w_out):
    batch, seq, d = x.shape
    depth = attn_norm_g.shape[0]
    t = batch * seq
    h = x.reshape(t, d)
    q_scale = HEAD_DIM ** -0.5

    da_bias = _da_bias(rel_table, DA_T)
    sw_bias = _sw_bias(rel_table, SW_TQ)

    da_qk = 2 * DA_HEADS * HEAD_DIM
    da_flag = jnp.concatenate([jnp.ones((1, 2 * da_qk), F32), jnp.zeros((1, DA_HEADS * DA_VDIM), F32)], axis=1)
    sw_qw = SW_Q_HEADS * HEAD_DIM
    sw_kw = SW_KV_HEADS * HEAD_DIM
    sw_flag = jnp.concatenate([jnp.ones((1, sw_qw + sw_kw), F32), jnp.zeros((1, sw_kw), F32)], axis=1)

    for i in range(depth):
        j = i // N_MIXERS
        if i % N_MIXERS == 0:
            gain = jnp.concatenate([jnp.tile(da_q_gain[j], da_qk // HEAD_DIM) * (q_scale * LOG2E),
                                    jnp.tile(da_k_gain[j], da_qk // HEAD_DIM),
                                    jnp.ones((DA_HEADS * DA_VDIM,), F32)])
            qkv = _norm_proj(h, _row(attn_norm_g[i]), da_w_qkv[j].astype(BF16), da_flag, _row(gain))
            nk = seq // DA_T
            vt = qkv[:, 2 * da_qk:].reshape(batch, nk, DA_T, DA_HEADS, DA_VDIM).transpose(0, 3, 1, 4, 2)
            ones_rows = jnp.zeros((DA_VROWS - DA_VDIM, DA_T), BF16).at[0].set(1.0)
            vt = jnp.concatenate([vt, jnp.broadcast_to(ones_rows, vt.shape[:3] + ones_rows.shape)], axis=3)
            score_bound = (HEAD_DIM * q_scale * jnp.max(jnp.abs(da_q_gain[j])) * jnp.max(jnp.abs(da_k_gain[j]))
                           + jnp.max(jnp.abs(rel_table - rel_table[NUM_BUCKETS - 1]))) * LOG2E
            fast = (score_bound <= FAST_BOUND).astype(jnp.int32).reshape(1)
            lamv = jnp.stack([da_lam_q1[j], da_lam_k1[j], da_lam_q2[j], da_lam_k2[j]]).astype(F32)
            lam_init = 0.8 - 0.6 * math.exp(-0.3 * i)
            a = _da_attn(fast, qkv, vt, da_bias, lamv, da_sub_gain[j].reshape(DA_VDIM, 1).astype(F32), batch, seq,
                         lam_init)
            h = _out_proj(a, da_w_o[j].astype(BF16), h)
        else:
            gain = jnp.concatenate([jnp.tile(sw_q_gain[j], SW_Q_HEADS) * (q_scale * LOG2E),
                                    jnp.tile(sw_k_gain[j], SW_KV_HEADS),
                                    jnp.ones((sw_kw,), F32)])
            qkv = _norm_proj(h, _row(attn_norm_g[i]), sw_w_qkv[j].astype(BF16), sw_flag, _row(gain))
            k = qkv[:, sw_qw:sw_qw + sw_kw].reshape(t, SW_KV_HEADS, 1, HEAD_DIM)
            kdup = jnp.broadcast_to(k, (t, SW_KV_HEADS, 2, HEAD_DIM)).reshape(t, SW_KV_HEADS * LANES)
            vt = qkv[:, sw_qw + sw_kw:].reshape(batch, seq, SW_KV_HEADS, HEAD_DIM).transpose(0, 2, 3, 1)
            ones_rows = jnp.zeros((SW_VROWS - HEAD_DIM, seq), BF16).at[0].set(1.0)
            vt = jnp.concatenate([vt, jnp.broadcast_to(ones_rows, vt.shape[:2] + ones_rows.shape)], axis=2)
            score_bound = (HEAD_DIM * q_scale * jnp.max(jnp.abs(sw_q_gain[j])) * jnp.max(jnp.abs(sw_k_gain[j]))
                           + jnp.max(jnp.abs(rel_table)) + jnp.max(jnp.abs(sw_sinks[j]))) * LOG2E
            fast = (score_bound <= FAST_BOUND).astype(jnp.int32).reshape(1)
            a = _sw_attn(fast, qkv, kdup, vt, sw_bias, sw_sinks[j].astype(F32), batch, seq)
            h = _out_proj(a, sw_w_o[j].astype(BF16), h)
        h = _ffn(h, _row(ffn_norm_g[i]), ffn_w_in[i].astype(BF16), ffn_conv_w[i].astype(F32),
                 _row(ffn_conv_b[i]), ffn_w_out[i].astype(BF16), seq)
    return h.reshape(batch, seq, d)
```

```python
import functools
import math

import numpy as np
import jax
import jax.numpy as jnp
from jax import lax
from jax.experimental import pallas as pl
from jax.experimental.pallas import tpu as pltpu

F32 = jnp.float32
BF16 = jnp.bfloat16

HEAD_DIM = 64
N_MAPS = 32
DA_HEADS = 16
DA_VDIM = 128
SW_Q_HEADS = 32
SW_KV_HEADS = 4
SW_GROUP = SW_Q_HEADS // SW_KV_HEADS
SW_WINDOW = 128
NUM_BUCKETS = 32
MAX_DISTANCE = 128
CONV_WIDTH = 3
EPS = 1e-6
N_MIXERS = 2

LANES = 128
NEG = -1e30
LOG2E = math.log2(math.e)
VMEM_LIMIT = 56 * 1024 * 1024

PROJ_TM = 1024
PROJ_TN = 512
PROJ_ROWS = 256
OUT_TM = 1024
OUT_TN = 1024
FFN_TM = 512
FFN_TF = 512
DA_T = 512
DA_VROWS = DA_VDIM + 16
FAST_BOUND = 60.0
SW_TQ = 256
SW_VROWS = HEAD_DIM + 16


def _cparams(*sem):
    return pltpu.CompilerParams(dimension_semantics=sem, vmem_limit_bytes=VMEM_LIMIT)


def _t5_bucket_np(dist):
    max_exact = NUM_BUCKETS // 2
    d = np.maximum(dist, 0)
    df = np.maximum(d, 1).astype(np.float64)
    large = max_exact + (np.log(df / max_exact) / math.log(MAX_DISTANCE / max_exact)
                         * (NUM_BUCKETS - max_exact)).astype(np.int32)
    large = np.minimum(large, NUM_BUCKETS - 1)
    return np.where(d < max_exact, d, large).astype(np.int32)


def _da_bias_kernel(tab_ref, bucket_ref, out_ref, *, t):
    h = pl.program_id(0)
    s = MAX_DISTANCE
    for off in range(2):
        for m in range(2):
            col = 2 * h + m
            far = tab_ref[NUM_BUCKETS - 1, col]
            for kr in range(t // s):
                for qc in range(t // s):
                    base = off * t + (qc - kr) * s
                    dst = (0, off, slice(kr * s, (kr + 1) * s), slice(m * t + qc * s, m * t + (qc + 1) * s))
                    if base - (s - 1) >= MAX_DISTANCE:
                        out_ref[dst] = jnp.zeros((s, s), F32)
                    elif base + (s - 1) < 0:
                        out_ref[dst] = jnp.full((s, s), NEG, F32)
                    else:
                        bk = bucket_ref[off, kr * s:(kr + 1) * s, qc * s:(qc + 1) * s]
                        acc = jnp.zeros(bk.shape, F32)
                        for b in range(NUM_BUCKETS - 1):
                            acc = jnp.where(bk == b, (tab_ref[b, col] - far) * LOG2E, acc)
                        out_ref[dst] = jnp.where(bk < 0, NEG, acc)


def _da_bias(rel_table, t):
    key = np.arange(t)[:, None]
    qry = np.arange(t)[None, :]
    d0 = qry - key
    d1 = t + qry - key
    bucket = np.stack([np.where(d0 >= 0, _t5_bucket_np(d0), -1), _t5_bucket_np(d1)]).astype(np.int32)
    return pl.pallas_call(
        functools.partial(_da_bias_kernel, t=t),
        grid=(DA_HEADS,),
        in_specs=[pl.BlockSpec(memory_space=pltpu.SMEM),
                  pl.BlockSpec((2, t, t), lambda h: (0, 0, 0))],
        out_specs=pl.BlockSpec((1, 2, t, 2 * t), lambda h: (h, 0, 0, 0)),
        out_shape=jax.ShapeDtypeStruct((DA_HEADS, 2, t, 2 * t), F32),
        compiler_params=_cparams("arbitrary"),
        name="da_bias",
    )(rel_table, jnp.asarray(bucket))


def _sw_bias_kernel(tab_ref, bucket_ref, out_ref, *, tq):
    p = pl.program_id(1)
    bk = bucket_ref[0]
    for m in range(2):
        col = 2 * p + m
        acc = jnp.zeros(bk.shape, F32)
        for b in range(NUM_BUCKETS):
            acc = jnp.where(bk == b, tab_ref[b, col] * LOG2E, acc)
        out_ref[0, 0, :, m * tq:(m + 1) * tq] = jnp.where(bk < 0, NEG, acc)


def _sw_bias(rel_table, tq):
    band = SW_WINDOW + tq
    key = np.arange(band)[:, None]
    qry = np.arange(tq)[None, :]
    dist = qry + SW_WINDOW - key
    bucket = np.where((dist >= 0) & (dist < SW_WINDOW), _t5_bucket_np(dist), -1).astype(np.int32)
    first = np.where(key < SW_WINDOW, -1, bucket).astype(np.int32)
    return pl.pallas_call(
        functools.partial(_sw_bias_kernel, tq=tq),
        grid=(2, SW_Q_HEADS // 2),
        in_specs=[pl.BlockSpec(memory_space=pltpu.SMEM),
                  pl.BlockSpec((1, band, tq), lambda v, p: (v, 0, 0))],
        out_specs=pl.BlockSpec((1, 1, band, 2 * tq), lambda v, p: (v, p, 0, 0)),
        out_shape=jax.ShapeDtypeStruct((2, SW_Q_HEADS // 2, band, 2 * tq), F32),
        compiler_params=_cparams("arbitrary", "arbitrary"),
        name="sw_bias",
    )(rel_table, jnp.asarray(np.stack([bucket, first])))


def _rms_rows(x, g):
    ms = jnp.mean(x * x, axis=-1, keepdims=True)
    return x * lax.rsqrt(ms + EPS) * g


def _norm_proj_kernel(x_ref, g_ref, w_ref, flag_ref, gain_ref, o_ref, xn_ref, y0_ref, y1_ref):
    i = pl.program_id(0)
    j = pl.program_id(1)
    tm, tn = y0_ref.shape

    @pl.when((i == 0) & (j == 0))
    def _():
        y1_ref[...] = jnp.zeros(y1_ref.shape, F32)

    @pl.when(j == 0)
    def _():
        xn_ref[...] = _rms_rows(x_ref[...], g_ref[...]).astype(BF16)

    def run(parity):
        y_cur, y_prev = (y0_ref, y1_ref)[::1 - 2 * parity]
        lane = lax.broadcasted_iota(jnp.int32, (1, LANES), 1)
        low = lane < HEAD_DIM
        for r in range(tm // PROJ_ROWS):
            rows = slice(r * PROJ_ROWS, (r + 1) * PROJ_ROWS)
            for c in range(tn // LANES):
                sl = slice(c * LANES, (c + 1) * LANES)
                yc = y_prev[rows, sl]
                y2 = yc * yc
                ms_lo = jnp.sum(jnp.where(low, y2, 0.0), axis=-1, keepdims=True) * (1.0 / HEAD_DIM)
                ms_hi = jnp.sum(jnp.where(low, 0.0, y2), axis=-1, keepdims=True) * (1.0 / HEAD_DIM)
                inv = jnp.where(low, lax.rsqrt(ms_lo + EPS), lax.rsqrt(ms_hi + EPS))
                scale = jnp.where(flag_ref[:, sl] > 0.0, inv, 1.0) * gain_ref[:, sl]
                o_ref[rows, sl] = (yc * scale).astype(BF16)
            y_cur[rows, :] = jnp.dot(xn_ref[rows, :], w_ref[...], preferred_element_type=F32)

    even = (j % 2) == 0
    pl.when(even)(functools.partial(run, 0))
    pl.when(jnp.logical_not(even))(functools.partial(run, 1))


def _norm_proj(x, g, w, flag, gain):
    t, d = x.shape
    n = w.shape[1]
    tm, tn = PROJ_TM, PROJ_TN
    assert t % tm == 0 and n % tn == 0 and tm % PROJ_ROWS == 0
    nj = n // tn

    def tile(j):
        return jnp.clip(j, 0, nj - 1)

    return pl.pallas_call(
        _norm_proj_kernel,
        grid=(t // tm, nj + 1),
        in_specs=[pl.BlockSpec((tm, d), lambda i, j: (i, 0)),
                  pl.BlockSpec((1, d), lambda i, j: (0, 0)),
                  pl.BlockSpec((d, tn), lambda i, j: (0, tile(j))),
                  pl.BlockSpec((1, tn), lambda i, j: (0, tile(j - 1))),
                  pl.BlockSpec((1, tn), lambda i, j: (0, tile(j - 1)))],
        out_specs=pl.BlockSpec((tm, tn), lambda i, j: (i, tile(j - 1))),
        out_shape=jax.ShapeDtypeStruct((t, n), BF16),
        scratch_shapes=[pltpu.VMEM((tm, d), BF16), pltpu.VMEM((tm, tn), F32), pltpu.VMEM((tm, tn), F32)],
        compiler_params=_cparams("arbitrary", "arbitrary"),
        name="norm_proj",
    )(x, g, w, flag, gain)


def _out_proj_kernel(a_ref, w_ref, h_ref, o_ref):
    o_ref[...] = h_ref[...] + jnp.dot(a_ref[...], w_ref[...], preferred_element_type=F32)


def _out_proj(a, w, h):
    t, k = a.shape
    n = w.shape[1]
    tm, tn = OUT_TM, OUT_TN
    assert t % tm == 0 and n % tn == 0
    return pl.pallas_call(
        _out_proj_kernel,
        grid=(t // tm, n // tn),
        in_specs=[pl.BlockSpec((tm, k), lambda i, j: (i, 0)),
                  pl.BlockSpec((k, tn), lambda i, j: (0, j)),
                  pl.BlockSpec((tm, tn), lambda i, j: (i, j))],
        out_specs=pl.BlockSpec((tm, tn), lambda i, j: (i, j)),
        out_shape=jax.ShapeDtypeStruct((t, n), F32),
        compiler_params=_cparams("arbitrary", "arbitrary"),
        name="out_proj",
    )(a, w, h)


def _stack_maps(q):
    lane = lax.broadcasted_iota(jnp.int32, q.shape, 1)
    qf = q.astype(F32)
    return jnp.concatenate([jnp.where(lane < HEAD_DIM, qf, 0.0), jnp.where(lane < HEAD_DIM, 0.0, qf)],
                           axis=0).astype(BF16)


def _da_attn_kernel(fast_ref, q_ref, k_ref, vt_ref, bias_ref, lamv_ref, sg_ref, o_ref, acc_ref, m_ref, *, t, lam_init):
    i = pl.program_id(2)
    qt = q_ref[...].astype(F32).T
    row = lax.broadcasted_iota(jnp.int32, qt.shape, 0)
    qqt = jnp.concatenate([jnp.where(row < HEAD_DIM, qt, 0.0), jnp.where(row < HEAD_DIM, 0.0, qt)],
                          axis=1).astype(BF16)
    acc_ref[...] = jnp.zeros(acc_ref.shape, F32)

    def scores(u, bias):
        k = k_ref[pl.ds(pl.multiple_of(u * t, t), t), :]
        st = jnp.dot(k, qqt, preferred_element_type=F32)
        return st if bias is None else st + bias

    def chunk(u0, biases):
        pv = None
        for n, bias in enumerate(biases):
            p = jnp.exp2(scores(u0 + n, bias)).astype(BF16)
            d = jnp.dot(vt_ref[u0 + n], p, preferred_element_type=F32)
            pv = d if pv is None else pv + d
        acc_ref[...] += pv

    def unit_running_max(u, bias):
        st = scores(u, bias)
        m_prev = m_ref[...]
        m_new = jnp.maximum(m_prev, jnp.max(st, axis=0, keepdims=True))
        p = jnp.exp2(st - m_new).astype(BF16)
        acc_ref[...] = jnp.exp2(m_prev - m_new) * acc_ref[...] + jnp.dot(vt_ref[u], p, preferred_element_type=F32)
        m_ref[...] = m_new

    fast = fast_ref[0] == 1
    odd = (i % 2) == 1

    @pl.when(fast)
    def _():
        def far_pair(c, carry):
            chunk(2 * c, (None, None))
            return carry

        lax.fori_loop(0, jnp.maximum(i - 1, 0) // 2, far_pair, 0)

    @pl.when(fast & (i == 0))
    def _():
        chunk(0, (bias_ref[0, 0],))

    @pl.when(fast & odd)
    def _():
        chunk(i - 1, (bias_ref[0, 1], bias_ref[0, 0]))

    @pl.when(fast & jnp.logical_not(odd) & (i > 0))
    def _():
        chunk(i - 2, (None, bias_ref[0, 1], bias_ref[0, 0]))

    @pl.when(jnp.logical_not(fast))
    def _():
        m_ref[...] = jnp.full(m_ref.shape, NEG, F32)

        def far_unit(u, carry):
            unit_running_max(u, None)
            return carry

        lax.fori_loop(0, jnp.maximum(i - 1, 0), far_unit, 0)

        @pl.when(i >= 1)
        def _():
            unit_running_max(i - 1, bias_ref[0, 1])

        unit_running_max(i, bias_ref[0, 0])

    lv = lamv_ref[...]
    lam = (jnp.exp(jnp.sum(lv[0:1] * lv[1:2], axis=-1, keepdims=True))
           - jnp.exp(jnp.sum(lv[2:3] * lv[3:4], axis=-1, keepdims=True)) + lam_init)
    acc = acc_ref[...]
    o = acc[:DA_VDIM] * (1.0 / acc[DA_VDIM:DA_VDIM + 1])
    ot = o[:, :t] - lam * o[:, t:]
    ms = jnp.mean(ot * ot, axis=0, keepdims=True)
    ot = ot * lax.rsqrt(ms + EPS) * (sg_ref[...] * (1.0 - lam_init))
    o_ref[...] = ot.T.astype(BF16)


def _da_attn(fast, qkv, vt, bias, lamv, sub_gain, batch, seq, lam_init):
    t = DA_T
    nq = seq // t
    return pl.pallas_call(
        functools.partial(_da_attn_kernel, t=t, lam_init=lam_init),
        grid=(batch, DA_HEADS, nq),
        in_specs=[pl.BlockSpec(memory_space=pltpu.SMEM),
                  pl.BlockSpec((t, LANES), lambda b, h, i: (b * nq + i, h)),
                  pl.BlockSpec((seq, LANES), lambda b, h, i: (b, DA_HEADS + h)),
                  pl.BlockSpec((None, None, nq, DA_VROWS, t), lambda b, h, i: (b, h, 0, 0, 0)),
                  pl.BlockSpec((1, 2, t, 2 * t), lambda b, h, i: (h, 0, 0, 0)),
                  pl.BlockSpec((4, HEAD_DIM), lambda b, h, i: (0, 0)),
                  pl.BlockSpec((DA_VDIM, 1), lambda b, h, i: (0, 0))],
        out_specs=pl.BlockSpec((t, LANES), lambda b, h, i: (b * nq + i, h)),
        out_shape=jax.ShapeDtypeStruct((batch * seq, DA_HEADS * DA_VDIM), BF16),
        scratch_shapes=[pltpu.VMEM((DA_VROWS, 2 * t), F32), pltpu.VMEM((1, 2 * t), F32)],
        compiler_params=_cparams("arbitrary", "arbitrary", "arbitrary"),
        name="da_attn",
    )(fast, qkv, qkv, vt, bias, lamv, sub_gain)


def _sw_attn_kernel(fast_ref, sink_ref, q_ref, kp_ref, kc_ref, vp_ref, vc_ref, bias_ref, o_ref, *, tq):
    g = pl.program_id(0)
    kb = jnp.concatenate([kp_ref[...], kc_ref[...]], axis=0)
    vtb = jnp.concatenate([vp_ref[...], vc_ref[...]], axis=1)
    col = lax.broadcasted_iota(jnp.int32, (1, 2 * tq), 1)

    def head_pair(pp, fast):
        sl = slice(pp * LANES, (pp + 1) * LANES)
        head = g * SW_GROUP + 2 * pp
        sink = jnp.where(col < tq, sink_ref[head], sink_ref[head + 1]) * LOG2E
        qt = q_ref[:, sl].astype(F32).T
        row = lax.broadcasted_iota(jnp.int32, qt.shape, 0)
        qqt = jnp.concatenate([jnp.where(row < HEAD_DIM, qt, 0.0), jnp.where(row < HEAD_DIM, 0.0, qt)],
                              axis=1).astype(BF16)
        st = jnp.dot(kb, qqt, preferred_element_type=F32) + bias_ref[0, pp]
        if fast:
            ot = jnp.dot(vtb, jnp.exp2(st).astype(BF16), preferred_element_type=F32)
            denom = ot[HEAD_DIM:HEAD_DIM + 1] + jnp.exp2(sink)
        else:
            m = jnp.maximum(jnp.max(st, axis=0, keepdims=True), sink)
            ot = jnp.dot(vtb, jnp.exp2(st - m).astype(BF16), preferred_element_type=F32)
            denom = ot[HEAD_DIM:HEAD_DIM + 1] + jnp.exp2(sink - m)
        ot = ot[:HEAD_DIM] * (1.0 / denom)
        o_ref[:, sl] = jnp.concatenate([ot[:, :tq], ot[:, tq:]], axis=0).T.astype(BF16)

    @pl.when(fast_ref[0] == 1)
    def _():
        for pp in range(SW_GROUP // 2):
            head_pair(pp, True)

    @pl.when(fast_ref[0] != 1)
    def _():
        for pp in range(SW_GROUP // 2):
            head_pair(pp, False)


def _sw_attn(fast, q, kdup, vt, bias, sinks, batch, seq):
    tq = SW_TQ
    nq = seq // tq
    r = tq // SW_WINDOW
    band = SW_WINDOW + tq
    gw = SW_GROUP * HEAD_DIM
    return pl.pallas_call(
        functools.partial(_sw_attn_kernel, tq=tq),
        grid=(SW_KV_HEADS, batch, nq),
        in_specs=[pl.BlockSpec(memory_space=pltpu.SMEM),
                  pl.BlockSpec(memory_space=pltpu.SMEM),
                  pl.BlockSpec((tq, gw), lambda g, b, i: (b * nq + i, g)),
                  pl.BlockSpec((SW_WINDOW, LANES), lambda g, b, i: (jnp.maximum((b * nq + i) * r - 1, 0), g)),
                  pl.BlockSpec((tq, LANES), lambda g, b, i: (b * nq + i, g)),
                  pl.BlockSpec((None, None, SW_VROWS, SW_WINDOW), lambda g, b, i: (b, g, 0, jnp.maximum(i * r - 1, 0))),
                  pl.BlockSpec((None, None, SW_VROWS, tq), lambda g, b, i: (b, g, 0, i)),
                  pl.BlockSpec((1, SW_GROUP // 2, band, 2 * tq), lambda g, b, i: (jnp.where(i == 0, 1, 0), g, 0, 0))],
        out_specs=pl.BlockSpec((tq, gw), lambda g, b, i: (b * nq + i, g)),
        out_shape=jax.ShapeDtypeStruct((batch * seq, SW_Q_HEADS * HEAD_DIM), BF16),
        compiler_params=_cparams("arbitrary", "arbitrary", "arbitrary"),
        name="sw_attn",
    )(fast, sinks, q, kdup, kdup, vt, vt, bias)


def _causal_conv(u, halo, cw_ref, cb_ref):
    row = lax.broadcasted_iota(jnp.int32, (u.shape[0], 1), 0)
    u1 = jnp.where(row == 0, halo[7:8], pltpu.roll(u, 1, axis=0))
    u2 = jnp.where(row == 0, halo[6:7], jnp.where(row == 1, halo[7:8], pltpu.roll(u, 2, axis=0)))
    return cw_ref[0:1, :] * u2 + cw_ref[1:2, :] * u1 + cw_ref[2:3, :] * u + cb_ref[...]


def _ffn_kernel(h_ref, g_ref, wg_ref, wu_ref, cwg_ref, cwu_ref, cbg_ref, cbu_ref, wo_ref, o_ref,
                f_ref, act_ref, halo_g_ref, halo_u_ref, *, tiles_per_seq, nj):
    i = pl.program_id(0)
    j = pl.program_id(1)
    tm = h_ref.shape[0]

    @pl.when(j == 0)
    def _():
        x = h_ref[...]
        f_ref[...] = _rms_rows(x, g_ref[...]).astype(BF16)
        o_ref[...] = x

    @pl.when(((i % tiles_per_seq) == 0) & (j < nj))
    def _():
        halo_g_ref[j] = jnp.zeros(halo_g_ref.shape[1:], F32)
        halo_u_ref[j] = jnp.zeros(halo_u_ref.shape[1:], F32)

    def run(do_up, do_down):
        if do_up:
            f = f_ref[...]
            ug = jnp.dot(f, wg_ref[...], preferred_element_type=F32)
            uu = jnp.dot(f, wu_ref[...], preferred_element_type=F32)
        if do_down:
            o_ref[...] += jnp.dot(act_ref[(j + 1) % 2], wo_ref[...], preferred_element_type=F32)
        if do_up:
            def conv(u, cw_ref, cb_ref, halo_ref):
                halo = halo_ref[j]
                halo_ref[j] = u[tm - 8:, :]
                return _causal_conv(u, halo, cw_ref, cb_ref)

            gate = conv(ug, cwg_ref, cbg_ref, halo_g_ref)
            up = conv(uu, cwu_ref, cbu_ref, halo_u_ref)
            act_ref[j % 2] = (gate * (1.0 / (1.0 + jnp.exp(-gate))) * up).astype(BF16)

    pl.when(j == 0)(functools.partial(run, True, False))
    pl.when((j > 0) & (j < nj))(functools.partial(run, True, True))
    pl.when(j == nj)(functools.partial(run, False, True))


def _ffn(h, g, w_in, conv_w, conv_b, w_out, seq):
    t, d = h.shape
    dff = w_out.shape[0]
    tm, tf = FFN_TM, FFN_TF
    assert t % tm == 0 and seq % tm == 0 and dff % tf == 0
    nj = dff // tf

    def up_tile(j):
        return jnp.minimum(j, nj - 1)

    return pl.pallas_call(
        functools.partial(_ffn_kernel, tiles_per_seq=seq // tm, nj=nj),
        grid=(t // tm, nj + 1),
        in_specs=[pl.BlockSpec((tm, d), lambda i, j: (i, 0)),
                  pl.BlockSpec((1, d), lambda i, j: (0, 0)),
                  pl.BlockSpec((d, tf), lambda i, j: (0, up_tile(j))),
                  pl.BlockSpec((d, tf), lambda i, j: (0, nj + up_tile(j))),
                  pl.BlockSpec((CONV_WIDTH, tf), lambda i, j: (0, up_tile(j))),
                  pl.BlockSpec((CONV_WIDTH, tf), lambda i, j: (0, nj + up_tile(j))),
                  pl.BlockSpec((1, tf), lambda i, j: (0, up_tile(j))),
                  pl.BlockSpec((1, tf), lambda i, j: (0, nj + up_tile(j))),
                  pl.BlockSpec((tf, d), lambda i, j: (jnp.maximum(j - 1, 0), 0))],
        out_specs=pl.BlockSpec((tm, d), lambda i, j: (i, 0)),
        out_shape=jax.ShapeDtypeStruct((t, d), F32),
        scratch_shapes=[pltpu.VMEM((tm, d), BF16),
                        pltpu.VMEM((2, tm, tf), BF16),
                        pltpu.VMEM((nj, 8, tf), F32),
                        pltpu.VMEM((nj, 8, tf), F32)],
        compiler_params=_cparams("arbitrary", "arbitrary"),
        name="conv_ffn",
    )(h, g, w_in, w_in, conv_w, conv_w, conv_b, conv_b, w_out)


def _row(v):
    return v.reshape(1, -1).astype(F32)


def kernel(x, rel_table, attn_norm_g, ffn_norm_g, da_w_qkv, da_w_o, da_q_gain, da_k_gain, da_lam_q1, da_lam_k1, da_lam_q2, da_lam_k2, da_sub_gain, sw_w_qkv, sw_w_o, sw_q_gain, sw_k_gain, sw_sinks, ffn_w_in, ffn_conv_w, ffn_conv_b, ffn_w_out):
    batch, seq, d = x.shape
    depth = attn_norm_g.shape[0]
    t = batch * seq
    h = x.reshape(t, d)
    q_scale = HEAD_DIM ** -0.5

    da_bias = _da_bias(rel_table, DA_T)
    sw_bias = _sw_bias(rel_table, SW_TQ)

    da_qk = 2 * DA_HEADS * HEAD_DIM
    da_flag = jnp.concatenate([jnp.ones((1, 2 * da_qk), F32), jnp.zeros((1, DA_HEADS * DA_VDIM), F32)], axis=1)
    sw_qw = SW_Q_HEADS * HEAD_DIM
    sw_kw = SW_KV_HEADS * HEAD_DIM
    sw_flag = jnp.concatenate([jnp.ones((1, sw_qw + sw_kw), F32), jnp.zeros((1, sw_kw), F32)], axis=1)

    for i in range(depth):
        j = i // N_MIXERS
        if i % N_MIXERS == 0:
            gain = jnp.concatenate([jnp.tile(da_q_gain[j], da_qk // HEAD_DIM) * (q_scale * LOG2E),
                                    jnp.tile(da_k_gain[j], da_qk // HEAD_DIM),
                                    jnp.ones((DA_HEADS * DA_VDIM,), F32)])
            qkv = _norm_proj(h, _row(attn_norm_g[i]), da_w_qkv[j].astype(BF16), da_flag, _row(gain))
            nk = seq // DA_T
            vt = qkv[:, 2 * da_qk:].reshape(batch, nk, DA_T, DA_HEADS, DA_VDIM).transpose(0, 3, 1, 4, 2)
            ones_rows = jnp.zeros((DA_VROWS - DA_VDIM, DA_T), BF16).at[0].set(1.0)
            vt = jnp.concatenate([vt, jnp.broadcast_to(ones_rows, vt.shape[:3] + ones_rows.shape)], axis=3)
            score_bound = (HEAD_DIM * q_scale * jnp.max(jnp.abs(da_q_gain[j])) * jnp.max(jnp.abs(da_k_gain[j]))
                           + jnp.max(jnp.abs(rel_table - rel_table[NUM_BUCKETS - 1]))) * LOG2E
            fast = (score_bound <= FAST_BOUND).astype(jnp.int32).reshape(1)
            lamv = jnp.stack([da_lam_q1[j], da_lam_k1[j], da_lam_q2[j], da_lam_k2[j]]).astype(F32)
            lam_init = 0.8 - 0.6 * math.exp(-0.3 * i)
            a = _da_attn(fast, qkv, vt, da_bias, lamv, da_sub_gain[j].reshape(DA_VDIM, 1).astype(F32), batch, seq,
                         lam_init)
            h = _out_proj(a, da_w_o[j].astype(BF16), h)
        else:
            gain = jnp.concatenate([jnp.tile(sw_q_gain[j], SW_Q_HEADS) * (q_scale * LOG2E),
                                    jnp.tile(sw_k_gain[j], SW_KV_HEADS),
                                    jnp.ones((sw_kw,), F32)])
            qkv = _norm_proj(h, _row(attn_norm_g[i]), sw_w_qkv[j].astype(BF16), sw_flag, _row(gain))
            k = qkv[:, sw_qw:sw_qw + sw_kw].reshape(t, SW_KV_HEADS, 1, HEAD_DIM)
            kdup = jnp.broadcast_to(k, (t, SW_KV_HEADS, 2, HEAD_DIM)).reshape(t, SW_KV_HEADS * LANES)
            vt = qkv[:, sw_qw + sw_kw:].reshape(batch, seq, SW_KV_HEADS, HEAD_DIM).transpose(0, 2, 3, 1)
            ones_rows = jnp.zeros((SW_VROWS - HEAD_DIM, seq), BF16).at[0].set(1.0)
            vt = jnp.concatenate([vt, jnp.broadcast_to(ones_rows, vt.shape[:2] + ones_rows.shape)], axis=2)
            score_bound = (HEAD_DIM * q_scale * jnp.max(jnp.abs(sw_q_gain[j])) * jnp.max(jnp.abs(sw_k_gain[j]))
                           + jnp.max(jnp.abs(rel_table)) + jnp.max(jnp.abs(sw_sinks[j]))) * LOG2E
            fast = (score_bound <= FAST_BOUND).astype(jnp.int32).reshape(1)
            a = _sw_attn(fast, qkv, kdup, vt, sw_bias, sw_sinks[j].astype(F32), batch, seq)
            h = _out_proj(a, sw_w_o[j].astype(BF16), h)
        h = _ffn(h, _row(ffn_norm_g[i]), ffn_w_in[i].astype(BF16), ffn_conv_w[i].astype(F32),
                 _row(ffn_conv_b[i]), ffn_w_out[i].astype(BF16), seq)
    return h.reshape(batch, seq, d)
```

```python
import functools
import math

import numpy as np
import jax
import jax.numpy as jnp
from jax import lax
from jax.experimental import pallas as pl
from jax.experimental.pallas import tpu as pltpu

F32 = jnp.float32
BF16 = jnp.bfloat16

HEAD_DIM = 64
N_MAPS = 32
DA_HEADS = 16
DA_VDIM = 128
SW_Q_HEADS = 32
SW_KV_HEADS = 4
SW_GROUP = SW_Q_HEADS // SW_KV_HEADS
SW_WINDOW = 128
NUM_BUCKETS = 32
MAX_DISTANCE = 128
CONV_WIDTH = 3
EPS = 1e-6
N_MIXERS = 2

LANES = 128
NEG = -1e30
LOG2E = math.log2(math.e)
VMEM_LIMIT = 56 * 1024 * 1024

PROJ_TM = 1024
PROJ_TN = 512
PROJ_ROWS = 256
OUT_TM = 1024
OUT_TN = 1024
FFN_TM = 512
FFN_TF = 512
DA_T = 512
DA_VROWS = DA_VDIM + 16
FAST_BOUND = 60.0
SW_TQ = 256
SW_VROWS = HEAD_DIM + 16


def _cparams(*sem):
    return pltpu.CompilerParams(dimension_semantics=sem, vmem_limit_bytes=VMEM_LIMIT)


def _t5_bucket_np(dist):
    max_exact = NUM_BUCKETS // 2
    d = np.maximum(dist, 0)
    df = np.maximum(d, 1).astype(np.float64)
    large = max_exact + (np.log(df / max_exact) / math.log(MAX_DISTANCE / max_exact)
                         * (NUM_BUCKETS - max_exact)).astype(np.int32)
    large = np.minimum(large, NUM_BUCKETS - 1)
    return np.where(d < max_exact, d, large).astype(np.int32)


def _da_bias_kernel(tab_ref, bucket_ref, out_ref, *, t):
    h = pl.program_id(0)
    s = MAX_DISTANCE
    for off in range(2):
        for m in range(2):
            col = 2 * h + m
            far = tab_ref[NUM_BUCKETS - 1, col]
            for kr in range(t // s):
                for qc in range(t // s):
                    base = off * t + (qc - kr) * s
                    dst = (0, off, slice(kr * s, (kr + 1) * s), slice(m * t + qc * s, m * t + (qc + 1) * s))
                    if base - (s - 1) >= MAX_DISTANCE:
                        out_ref[dst] = jnp.zeros((s, s), F32)
                    elif base + (s - 1) < 0:
                        out_ref[dst] = jnp.full((s, s), NEG, F32)
                    else:
                        bk = bucket_ref[off, kr * s:(kr + 1) * s, qc * s:(qc + 1) * s]
                        acc = jnp.zeros(bk.shape, F32)
                        for b in range(NUM_BUCKETS - 1):
                            acc = jnp.where(bk == b, (tab_ref[b, col] - far) * LOG2E, acc)
                        out_ref[dst] = jnp.where(bk < 0, NEG, acc)


def _da_bias(rel_table, t):
    key = np.arange(t)[:, None]
    qry = np.arange(t)[None, :]
    d0 = qry - key
    d1 = t + qry - key
    bucket = np.stack([np.where(d0 >= 0, _t5_bucket_np(d0), -1), _t5_bucket_np(d1)]).astype(np.int32)
    return pl.pallas_call(
        functools.partial(_da_bias_kernel, t=t),
        grid=(DA_HEADS,),
        in_specs=[pl.BlockSpec(memory_space=pltpu.SMEM),
                  pl.BlockSpec((2, t, t), lambda h: (0, 0, 0))],
        out_specs=pl.BlockSpec((1, 2, t, 2 * t), lambda h: (h, 0, 0, 0)),
        out_shape=jax.ShapeDtypeStruct((DA_HEADS, 2, t, 2 * t), F32),
        compiler_params=_cparams("arbitrary"),
        name="da_bias",
    )(rel_table, jnp.asarray(bucket))


def _sw_bias_kernel(tab_ref, bucket_ref, out_ref, *, tq):
    p = pl.program_id(1)
    bk = bucket_ref[0]
    for m in range(2):
        col = 2 * p + m
        acc = jnp.zeros(bk.shape, F32)
        for b in range(NUM_BUCKETS):
            acc = jnp.where(bk == b, tab_ref[b, col] * LOG2E, acc)
        out_ref[0, 0, :, m * tq:(m + 1) * tq] = jnp.where(bk < 0, NEG, acc)


def _sw_bias(rel_table, tq):
    band = SW_WINDOW + tq
    key = np.arange(band)[:, None]
    qry = np.arange(tq)[None, :]
    dist = qry + SW_WINDOW - key
    bucket = np.where((dist >= 0) & (dist < SW_WINDOW), _t5_bucket_np(dist), -1).astype(np.int32)
    first = np.where(key < SW_WINDOW, -1, bucket).astype(np.int32)
    return pl.pallas_call(
        functools.partial(_sw_bias_kernel, tq=tq),
        grid=(2, SW_Q_HEADS // 2),
        in_specs=[pl.BlockSpec(memory_space=pltpu.SMEM),
                  pl.BlockSpec((1, band, tq), lambda v, p: (v, 0, 0))],
        out_specs=pl.BlockSpec((1, 1, band, 2 * tq), lambda v, p: (v, p, 0, 0)),
        out_shape=jax.ShapeDtypeStruct((2, SW_Q_HEADS // 2, band, 2 * tq), F32),
        compiler_params=_cparams("arbitrary", "arbitrary"),
        name="sw_bias",
    )(rel_table, jnp.asarray(np.stack([bucket, first])))


def _rms_rows(x, g):
    ms = jnp.mean(x * x, axis=-1, keepdims=True)
    return x * lax.rsqrt(ms + EPS) * g


def _norm_proj_kernel(x_ref, g_ref, w_ref, flag_ref, gain_ref, o_ref, xn_ref, y0_ref, y1_ref):
    i = pl.program_id(0)
    j = pl.program_id(1)
    tm, tn = y0_ref.shape

    @pl.when((i == 0) & (j == 0))
    def _():
        y1_ref[...] = jnp.zeros(y1_ref.shape, F32)

    @pl.when(j == 0)
    def _():
        xn_ref[...] = _rms_rows(x_ref[...], g_ref[...]).astype(BF16)

    def run(parity):
        y_cur, y_prev = (y0_ref, y1_ref)[::1 - 2 * parity]
        lane = lax.broadcasted_iota(jnp.int32, (1, LANES), 1)
        low = lane < HEAD_DIM
        for r in range(tm // PROJ_ROWS):
            rows = slice(r * PROJ_ROWS, (r + 1) * PROJ_ROWS)
            for c in range(tn // LANES):
                sl = slice(c * LANES, (c + 1) * LANES)
                yc = y_prev[rows, sl]
                y2 = yc * yc
                ms_lo = jnp.sum(jnp.where(low, y2, 0.0), axis=-1, keepdims=True) * (1.0 / HEAD_DIM)
                ms_hi = jnp.sum(jnp.where(low, 0.0, y2), axis=-1, keepdims=True) * (1.0 / HEAD_DIM)
                inv = jnp.where(low, lax.rsqrt(ms_lo + EPS), lax.rsqrt(ms_hi + EPS))
                scale = jnp.where(flag_ref[:, sl] > 0.0, inv, 1.0) * gain_ref[:, sl]
                o_ref[rows, sl] = (yc * scale).astype(BF16)
            y_cur[rows, :] = jnp.dot(xn_ref[rows, :], w_ref[...], preferred_element_type=F32)

    even = (j % 2) == 0
    pl.when(even)(functools.partial(run, 0))
    pl.when(jnp.logical_not(even))(functools.partial(run, 1))


def _norm_proj(x, g, w, flag, gain):
    t, d = x.shape
    n = w.shape[1]
    tm, tn = PROJ_TM, PROJ_TN
    assert t % tm == 0 and n % tn == 0 and tm % PROJ_ROWS == 0
    nj = n // tn

    def tile(j):
        return jnp.clip(j, 0, nj - 1)

    return pl.pallas_call(
        _norm_proj_kernel,
        grid=(t // tm, nj + 1),
        in_specs=[pl.BlockSpec((tm, d), lambda i, j: (i, 0)),
                  pl.BlockSpec((1, d), lambda i, j: (0, 0)),
                  pl.BlockSpec((d, tn), lambda i, j: (0, tile(j))),
                  pl.BlockSpec((1, tn), lambda i, j: (0, tile(j - 1))),
                  pl.BlockSpec((1, tn), lambda i, j: (0, tile(j - 1)))],
        out_specs=pl.BlockSpec((tm, tn), lambda i, j: (i, tile(j - 1))),
        out_shape=jax.ShapeDtypeStruct((t, n), BF16),
        scratch_shapes=[pltpu.VMEM((tm, d), BF16), pltpu.VMEM((tm, tn), F32), pltpu.VMEM((tm, tn), F32)],
        compiler_params=_cparams("arbitrary", "arbitrary"),
        name="norm_proj",
    )(x, g, w, flag, gain)


def _out_proj_kernel(a_ref, w_ref, h_ref, o_ref):
    o_ref[...] = h_ref[...] + jnp.dot(a_ref[...], w_ref[...], preferred_element_type=F32)


def _out_proj(a, w, h):
    t, k = a.shape
    n = w.shape[1]
    tm, tn = OUT_TM, OUT_TN
    assert t % tm == 0 and n % tn == 0
    return pl.pallas_call(
        _out_proj_kernel,
        grid=(t // tm, n // tn),
        in_specs=[pl.BlockSpec((tm, k), lambda i, j: (i, 0)),
                  pl.BlockSpec((k, tn), lambda i, j: (0, j)),
                  pl.BlockSpec((tm, tn), lambda i, j: (i, j))],
        out_specs=pl.BlockSpec((tm, tn), lambda i, j: (i, j)),
        out_shape=jax.ShapeDtypeStruct((t, n), F32),
        compiler_params=_cparams("arbitrary", "arbitrary"),
        name="out_proj",
    )(a, w, h)


def _stack_maps(q):
    lane = lax.broadcasted_iota(jnp.int32, q.shape, 1)
    qf = q.astype(F32)
    return jnp.concatenate([jnp.where(lane < HEAD_DIM, qf, 0.0), jnp.where(lane < HEAD_DIM, 0.0, qf)],
                           axis=0).astype(BF16)


def _da_attn_kernel(fast_ref, q_ref, k_ref, vt_ref, bias_ref, lamv_ref, sg_ref, o_ref, acc_ref, m_ref, *, t, lam_init):
    i = pl.program_id(2)
    qt = q_ref[...].astype(F32).T
    row = lax.broadcasted_iota(jnp.int32, qt.shape, 0)
    qqt = jnp.concatenate([jnp.where(row < HEAD_DIM, qt, 0.0), jnp.where(row < HEAD_DIM, 0.0, qt)],
                          axis=1).astype(BF16)
    acc_ref[...] = jnp.zeros(acc_ref.shape, F32)

    def scores(u, bias):
        k = k_ref[pl.ds(pl.multiple_of(u * t, t), t), :]
        st = jnp.dot(k, qqt, preferred_element_type=F32)
        return st if bias is None else st + bias

    def chunk(u0, biases):
        pv = None
        for n, bias in enumerate(biases):
            p = jnp.exp2(scores(u0 + n, bias)).astype(BF16)
            d = jnp.dot(vt_ref[u0 + n], p, preferred_element_type=F32)
            pv = d if pv is None else pv + d
        acc_ref[...] += pv

    def unit_running_max(u, bias):
        st = scores(u, bias)
        m_prev = m_ref[...]
        m_new = jnp.maximum(m_prev, jnp.max(st, axis=0, keepdims=True))
        p = jnp.exp2(st - m_new).astype(BF16)
        acc_ref[...] = jnp.exp2(m_prev - m_new) * acc_ref[...] + jnp.dot(vt_ref[u], p, preferred_element_type=F32)
        m_ref[...] = m_new

    fast = fast_ref[0] == 1
    odd = (i % 2) == 1

    @pl.when(fast)
    def _():
        def far_pair(c, carry):
            chunk(2 * c, (None, None))
            return carry

        lax.fori_loop(0, jnp.maximum(i - 1, 0) // 2, far_pair, 0)

    @pl.when(fast & (i == 0))
    def _():
        chunk(0, (bias_ref[0, 0],))

    @pl.when(fast & odd)
    def _():
        chunk(i - 1, (bias_ref[0, 1], bias_ref[0, 0]))

    @pl.when(fast & jnp.logical_not(odd) & (i > 0))
    def _():
        chunk(i - 2, (None, bias_ref[0, 1], bias_ref[0, 0]))

    @pl.when(jnp.logical_not(fast))
    def _():
        m_ref[...] = jnp.full(m_ref.shape, NEG, F32)

        def far_unit(u, carry):
            unit_running_max(u, None)
            return carry

        lax.fori_loop(0, jnp.maximum(i - 1, 0), far_unit, 0)

        @pl.when(i >= 1)
        def _():
            unit_running_max(i - 1, bias_ref[0, 1])

        unit_running_max(i, bias_ref[0, 0])

    lv = lamv_ref[...]
    lam = (jnp.exp(jnp.sum(lv[0:1] * lv[1:2], axis=-1, keepdims=True))
           - jnp.exp(jnp.sum(lv[2:3] * lv[3:4], axis=-1, keepdims=True)) + lam_init)
    acc = acc_ref[...]
    o = acc[:DA_VDIM] * (1.0 / acc[DA_VDIM:DA_VDIM + 1])
    ot = o[:, :t] - lam * o[:, t:]
    ms = jnp.mean(ot * ot, axis=0, keepdims=True)
    ot = ot * lax.rsqrt(ms + EPS) * (sg_ref[...] * (1.0 - lam_init))
    o_ref[...] = ot.T.astype(BF16)


def _da_attn(fast, qkv, vt, bias, lamv, sub_gain, batch, seq, lam_init):
    t = DA_T
    nq = seq // t
    return pl.pallas_call(
        functools.partial(_da_attn_kernel, t=t, lam_init=lam_init),
        grid=(batch, DA_HEADS, nq),
        in_specs=[pl.BlockSpec(memory_space=pltpu.SMEM),
                  pl.BlockSpec((t, LANES), lambda b, h, i: (b * nq + i, h)),
                  pl.BlockSpec((seq, LANES), lambda b, h, i: (b, DA_HEADS + h)),
                  pl.BlockSpec((None, None, nq, DA_VROWS, t), lambda b, h, i: (b, h, 0, 0, 0)),
                  pl.BlockSpec((1, 2, t, 2 * t), lambda b, h, i: (h, 0, 0, 0)),
                  pl.BlockSpec((4, HEAD_DIM), lambda b, h, i: (0, 0)),
                  pl.BlockSpec((DA_VDIM, 1), lambda b, h, i: (0, 0))],
        out_specs=pl.BlockSpec((t, LANES), lambda b, h, i: (b * nq + i, h)),
        out_shape=jax.ShapeDtypeStruct((batch * seq, DA_HEADS * DA_VDIM), BF16),
        scratch_shapes=[pltpu.VMEM((DA_VROWS, 2 * t), F32), pltpu.VMEM((1, 2 * t), F32)],
        compiler_params=_cparams("arbitrary", "arbitrary", "arbitrary"),
        name="da_attn",
    )(fast, qkv, qkv, vt, bias, lamv, sub_gain)


def _sw_attn_kernel(fast_ref, sink_ref, q_ref, kp_ref, kc_ref, vp_ref, vc_ref, bias_ref, o_ref, *, tq):
    g = pl.program_id(0)
    kb = jnp.concatenate([kp_ref[...], kc_ref[...]], axis=0)
    vtb = jnp.concatenate([vp_ref[...], vc_ref[...]], axis=1)
    col = lax.broadcasted_iota(jnp.int32, (1, 2 * tq), 1)

    def lane_slice(pp):
        return slice(pp * LANES, (pp + 1) * LANES)

    def scores(pp):
        qt = q_ref[:, lane_slice(pp)].astype(F32).T
        row = lax.broadcasted_iota(jnp.int32, qt.shape, 0)
        qqt = jnp.concatenate([jnp.where(row < HEAD_DIM, qt, 0.0), jnp.where(row < HEAD_DIM, 0.0, qt)],
                              axis=1).astype(BF16)
        return jnp.dot(kb, qqt, preferred_element_type=F32) + bias_ref[0, pp]

    def weighted_values(pp, st, fast):
        head = g * SW_GROUP + 2 * pp
        sink = jnp.where(col < tq, sink_ref[head], sink_ref[head + 1]) * LOG2E
        if fast:
            ot = jnp.dot(vtb, jnp.exp2(st).astype(BF16), preferred_element_type=F32)
            return ot, jnp.exp2(sink)
        m = jnp.maximum(jnp.max(st, axis=0, keepdims=True), sink)
        ot = jnp.dot(vtb, jnp.exp2(st - m).astype(BF16), preferred_element_type=F32)
        return ot, jnp.exp2(sink - m)

    def finish(pp, ot, sink_term):
        ot = ot[:HEAD_DIM] * (1.0 / (ot[HEAD_DIM:HEAD_DIM + 1] + sink_term))
        o_ref[:, lane_slice(pp)] = jnp.concatenate([ot[:, :tq], ot[:, tq:]], axis=0).T.astype(BF16)

    def all_pairs(fast):
        pairs = range(SW_GROUP // 2)
        sts = [scores(pp) for pp in pairs]
        ots = [weighted_values(pp, sts[pp], fast) for pp in pairs]
        for pp in pairs:
            finish(pp, *ots[pp])

    pl.when(fast_ref[0] == 1)(functools.partial(all_pairs, True))
    pl.when(fast_ref[0] != 1)(functools.partial(all_pairs, False))


def _sw_attn(fast, q, kdup, vt, bias, sinks, batch, seq):
    tq = SW_TQ
    nq = seq // tq
    r = tq // SW_WINDOW
    band = SW_WINDOW + tq
    gw = SW_GROUP * HEAD_DIM
    return pl.pallas_call(
        functools.partial(_sw_attn_kernel, tq=tq),
        grid=(SW_KV_HEADS, batch, nq),
        in_specs=[pl.BlockSpec(memory_space=pltpu.SMEM),
                  pl.BlockSpec(memory_space=pltpu.SMEM),
                  pl.BlockSpec((tq, gw), lambda g, b, i: (b * nq + i, g)),
                  pl.BlockSpec((SW_WINDOW, LANES), lambda g, b, i: (jnp.maximum((b * nq + i) * r - 1, 0), g)),
                  pl.BlockSpec((tq, LANES), lambda g, b, i: (b * nq + i, g)),
                  pl.BlockSpec((None, None, SW_VROWS, SW_WINDOW), lambda g, b, i: (b, g, 0, jnp.maximum(i * r - 1, 0))),
                  pl.BlockSpec((None, None, SW_VROWS, tq), lambda g, b, i: (b, g, 0, i)),
                  pl.BlockSpec((1, SW_GROUP // 2, band, 2 * tq), lambda g, b, i: (jnp.where(i == 0, 1, 0), g, 0, 0))],
        out_specs=pl.BlockSpec((tq, gw), lambda g, b, i: (b * nq + i, g)),
        out_shape=jax.ShapeDtypeStruct((batch * seq, SW_Q_HEADS * HEAD_DIM), BF16),
        compiler_params=_cparams("arbitrary", "arbitrary", "arbitrary"),
        name="sw_attn",
    )(fast, sinks, q, kdup, kdup, vt, vt, bias)


def _causal_conv(u, halo, cw_ref, cb_ref):
    row = lax.broadcasted_iota(jnp.int32, (8, 1), 0)
    r1 = pltpu.roll(u, 1, axis=0)
    r2 = pltpu.roll(u, 2, axis=0)
    u1 = jnp.concatenate([jnp.where(row == 0, halo[7:8], r1[:8]), r1[8:]], axis=0)
    u2 = jnp.concatenate([jnp.where(row == 0, halo[6:7], jnp.where(row == 1, halo[7:8], r2[:8])), r2[8:]], axis=0)
    return cw_ref[0:1, :] * u2 + cw_ref[1:2, :] * u1 + cw_ref[2:3, :] * u + cb_ref[...]


def _ffn_kernel(h_ref, g_ref, wg_ref, wu_ref, cwg_ref, cwu_ref, cbg_ref, cbu_ref, wo_ref, o_ref,
                f_ref, act_ref, halo_g_ref, halo_u_ref, *, tiles_per_seq, nj):
    i = pl.program_id(0)
    j = pl.program_id(1)
    tm = h_ref.shape[0]

    @pl.when(j == 0)
    def _():
        x = h_ref[...]
        f_ref[...] = _rms_rows(x, g_ref[...]).astype(BF16)
        o_ref[...] = x

    @pl.when(((i % tiles_per_seq) == 0) & (j < nj))
    def _():
        halo_g_ref[j] = jnp.zeros(halo_g_ref.shape[1:], F32)
        halo_u_ref[j] = jnp.zeros(halo_u_ref.shape[1:], F32)

    def run(do_up, do_down):
        if do_up:
            f = f_ref[...]
            ug = jnp.dot(f, wg_ref[...], preferred_element_type=F32)
            uu = jnp.dot(f, wu_ref[...], preferred_element_type=F32)
        if do_down:
            o_ref[...] += jnp.dot(act_ref[(j + 1) % 2], wo_ref[...], preferred_element_type=F32)
        if do_up:
            def conv(u, cw_ref, cb_ref, halo_ref):
                halo = halo_ref[j]
                halo_ref[j] = u[tm - 8:, :]
                return _causal_conv(u, halo, cw_ref, cb_ref)

            gate = conv(ug, cwg_ref, cbg_ref, halo_g_ref)
            up = conv(uu, cwu_ref, cbu_ref, halo_u_ref)
            act_ref[j % 2] = (gate * (1.0 / (1.0 + jnp.exp(-gate))) * up).astype(BF16)

    pl.when(j == 0)(functools.partial(run, True, False))
    pl.when((j > 0) & (j < nj))(functools.partial(run, True, True))
    pl.when(j == nj)(functools.partial(run, False, True))


def _ffn(h, g, w_in, conv_w, conv_b, w_out, seq):
    t, d = h.shape
    dff = w_out.shape[0]
    tm, tf = FFN_TM, FFN_TF
    assert t % tm == 0 and seq % tm == 0 and dff % tf == 0
    nj = dff // tf

    def up_tile(j):
        return jnp.minimum(j, nj - 1)

    return pl.pallas_call(
        functools.partial(_ffn_kernel, tiles_per_seq=seq // tm, nj=nj),
        grid=(t // tm, nj + 1),
        in_specs=[pl.BlockSpec((tm, d), lambda i, j: (i, 0)),
                  pl.BlockSpec((1, d), lambda i, j: (0, 0)),
                  pl.BlockSpec((d, tf), lambda i, j: (0, up_tile(j))),
                  pl.BlockSpec((d, tf), lambda i, j: (0, nj + up_tile(j))),
                  pl.BlockSpec((CONV_WIDTH, tf), lambda i, j: (0, up_tile(j))),
                  pl.BlockSpec((CONV_WIDTH, tf), lambda i, j: (0, nj + up_tile(j))),
                  pl.BlockSpec((1, tf), lambda i, j: (0, up_tile(j))),
                  pl.BlockSpec((1, tf), lambda i, j: (0, nj + up_tile(j))),
                  pl.BlockSpec((tf, d), lambda i, j: (jnp.maximum(j - 1, 0), 0))],
        out_specs=pl.BlockSpec((tm, d), lambda i, j: (i, 0)),
        out_shape=jax.ShapeDtypeStruct((t, d), F32),
        scratch_shapes=[pltpu.VMEM((tm, d), BF16),
                        pltpu.VMEM((2, tm, tf), BF16),
                        pltpu.VMEM((nj, 8, tf), F32),
                        pltpu.VMEM((nj, 8, tf), F32)],
        compiler_params=_cparams("arbitrary", "arbitrary"),
        name="conv_ffn",
    )(h, g, w_in, w_in, conv_w, conv_w, conv_b, conv_b, w_out)


def _row(v):
    return v.reshape(1, -1).astype(F32)


def kernel(x, rel_table, attn_norm_g, ffn_norm_g, da_w_qkv, da_w_o, da_q_gain, da_k_gain, da_lam_q1, da_lam_k1, da_lam_q2, da_lam_k2, da_sub_gain, sw_w_qkv, sw_w_o, sw_q_gain, sw_k_gain, sw_sinks, ffn_w_in, ffn_conv_w, ffn_conv_b, ffn_w_out):
    batch, seq, d = x.shape
    depth = attn_norm_g.shape[0]
    t = batch * seq
    h = x.reshape(t, d)
    q_scale = HEAD_DIM ** -0.5

    da_bias = _da_bias(rel_table, DA_T)
    sw_bias = _sw_bias(rel_table, SW_TQ)

    da_qk = 2 * DA_HEADS * HEAD_DIM
    da_flag = jnp.concatenate([jnp.ones((1, 2 * da_qk), F32), jnp.zeros((1, DA_HEADS * DA_VDIM), F32)], axis=1)
    sw_qw = SW_Q_HEADS * HEAD_DIM
    sw_kw = SW_KV_HEADS * HEAD_DIM
    sw_flag = jnp.concatenate([jnp.ones((1, sw_qw + sw_kw), F32), jnp.zeros((1, sw_kw), F32)], axis=1)

    for i in range(depth):
        j = i // N_MIXERS
        if i % N_MIXERS == 0:
            gain = jnp.concatenate([jnp.tile(da_q_gain[j], da_qk // HEAD_DIM) * (q_scale * LOG2E),
                                    jnp.tile(da_k_gain[j], da_qk // HEAD_DIM),
                                    jnp.ones((DA_HEADS * DA_VDIM,), F32)])
            qkv = _norm_proj(h, _row(attn_norm_g[i]), da_w_qkv[j].astype(BF16), da_flag, _row(gain))
            nk = seq // DA_T
            vt = qkv[:, 2 * da_qk:].reshape(batch, nk, DA_T, DA_HEADS, DA_VDIM).transpose(0, 3, 1, 4, 2)
            ones_rows = jnp.zeros((DA_VROWS - DA_VDIM, DA_T), BF16).at[0].set(1.0)
            vt = jnp.concatenate([vt, jnp.broadcast_to(ones_rows, vt.shape[:3] + ones_rows.shape)], axis=3)
            score_bound = (HEAD_DIM * q_scale * jnp.max(jnp.abs(da_q_gain[j])) * jnp.max(jnp.abs(da_k_gain[j]))
                           + jnp.max(jnp.abs(rel_table - rel_table[NUM_BUCKETS - 1]))) * LOG2E
            fast = (score_bound <= FAST_BOUND).astype(jnp.int32).reshape(1)
            lamv = jnp.stack([da_lam_q1[j], da_lam_k1[j], da_lam_q2[j], da_lam_k2[j]]).astype(F32)
            lam_init = 0.8 - 0.6 * math.exp(-0.3 * i)
            a = _da_attn(fast, qkv, vt, da_bias, lamv, da_sub_gain[j].reshape(DA_VDIM, 1).astype(F32), batch, seq,
                         lam_init)
            h = _out_proj(a, da_w_o[j].astype(BF16), h)
        else:
            gain = jnp.concatenate([jnp.tile(sw_q_gain[j], SW_Q_HEADS) * (q_scale * LOG2E),
                                    jnp.tile(sw_k_gain[j], SW_KV_HEADS),
                                    jnp.ones((sw_kw,), F32)])
            qkv = _norm_proj(h, _row(attn_norm_g[i]), sw_w_qkv[j].astype(BF16), sw_flag, _row(gain))
            k = qkv[:, sw_qw:sw_qw + sw_kw].reshape(t, SW_KV_HEADS, 1, HEAD_DIM)
            kdup = jnp.broadcast_to(k, (t, SW_KV_HEADS, 2, HEAD_DIM)).reshape(t, SW_KV_HEADS * LANES)
            vt = qkv[:, sw_qw + sw_kw:].reshape(batch, seq, SW_KV_HEADS, HEAD_DIM).transpose(0, 2, 3, 1)
            ones_rows = jnp.zeros((SW_VROWS - HEAD_DIM, seq), BF16).at[0].set(1.0)
            vt = jnp.concatenate([vt, jnp.broadcast_to(ones_rows, vt.shape[:2] + ones_rows.shape)], axis=2)
            score_bound = (HEAD_DIM * q_scale * jnp.max(jnp.abs(sw_q_gain[j])) * jnp.max(jnp.abs(sw_k_gain[j]))
                           + jnp.max(jnp.abs(rel_table)) + jnp.max(jnp.abs(sw_sinks[j]))) * LOG2E
            fast = (score_bound <= FAST_BOUND).astype(jnp.int32).reshape(1)
            a = _sw_attn(fast, qkv, kdup, vt, sw_bias, sw_sinks[j].astype(F32), batch, seq)
            h = _out_proj(a, sw_w_o[j].astype(BF16), h)
        h = _ffn(h, _row(ffn_norm_g[i]), ffn_w_in[i].astype(BF16), ffn_conv_w[i].astype(F32),
                 _row(ffn_conv_b[i]), ffn_w_out[i].astype(BF16), seq)
    return h.reshape(batch, seq, d)
```

```python
import functools
import math

import numpy as np
import jax
import jax.numpy as jnp
from jax import lax
from jax.experimental import pallas as pl
from jax.experimental.pallas import tpu as pltpu

F32 = jnp.float32
BF16 = jnp.bfloat16

HEAD_DIM = 64
N_MAPS = 32
DA_HEADS = 16
DA_VDIM = 128
SW_Q_HEADS = 32
SW_KV_HEADS = 4
SW_GROUP = SW_Q_HEADS // SW_KV_HEADS
SW_WINDOW = 128
NUM_BUCKETS = 32
MAX_DISTANCE = 128
CONV_WIDTH = 3
EPS = 1e-6
N_MIXERS = 2

LANES = 128
NEG = -1e30
LOG2E = math.log2(math.e)
VMEM_LIMIT = 56 * 1024 * 1024

PROJ_TM = 1024
PROJ_TN_DA = 1024
PROJ_TN_SW = 1280
PROJ_ROWS = 256
OUT_TM = 1024
OUT_TN = 1024
FFN_TM = 512
FFN_TF = 512
DA_T = 512
FAST_BOUND = 60.0
SW_TQ = 256
SW_VROWS = HEAD_DIM + 16


def _cparams(*sem):
    return pltpu.CompilerParams(dimension_semantics=sem, vmem_limit_bytes=VMEM_LIMIT)


def _t5_bucket_np(dist):
    max_exact = NUM_BUCKETS // 2
    d = np.maximum(dist, 0)
    df = np.maximum(d, 1).astype(np.float64)
    large = max_exact + (np.log(df / max_exact) / math.log(MAX_DISTANCE / max_exact)
                         * (NUM_BUCKETS - max_exact)).astype(np.int32)
    large = np.minimum(large, NUM_BUCKETS - 1)
    return np.where(d < max_exact, d, large).astype(np.int32)


def _da_bias_kernel(tab_ref, bucket_ref, out_ref, *, t):
    h = pl.program_id(0)
    s = MAX_DISTANCE
    for off in range(2):
        for m in range(2):
            col = 2 * h + m
            far = tab_ref[NUM_BUCKETS - 1, col]
            for kr in range(t // s):
                for qc in range(t // s):
                    base = off * t + (qc - kr) * s
                    dst = (0, off, slice(kr * s, (kr + 1) * s), slice(m * t + qc * s, m * t + (qc + 1) * s))
                    if base - (s - 1) >= MAX_DISTANCE:
                        out_ref[dst] = jnp.zeros((s, s), F32)
                    elif base + (s - 1) < 0:
                        out_ref[dst] = jnp.full((s, s), NEG, F32)
                    else:
                        bk = bucket_ref[off, kr * s:(kr + 1) * s, qc * s:(qc + 1) * s]
                        acc = jnp.zeros(bk.shape, F32)
                        for b in range(NUM_BUCKETS - 1):
                            acc = jnp.where(bk == b, (tab_ref[b, col] - far) * LOG2E, acc)
                        out_ref[dst] = jnp.where(bk < 0, NEG, acc)


def _da_bias(rel_table, t):
    key = np.arange(t)[:, None]
    qry = np.arange(t)[None, :]
    d0 = qry - key
    d1 = t + qry - key
    bucket = np.stack([np.where(d0 >= 0, _t5_bucket_np(d0), -1), _t5_bucket_np(d1)]).astype(np.int32)
    return pl.pallas_call(
        functools.partial(_da_bias_kernel, t=t),
        grid=(DA_HEADS,),
        in_specs=[pl.BlockSpec(memory_space=pltpu.SMEM),
                  pl.BlockSpec((2, t, t), lambda h: (0, 0, 0))],
        out_specs=pl.BlockSpec((1, 2, t, 2 * t), lambda h: (h, 0, 0, 0)),
        out_shape=jax.ShapeDtypeStruct((DA_HEADS, 2, t, 2 * t), F32),
        compiler_params=_cparams("arbitrary"),
        name="da_bias",
    )(rel_table, jnp.asarray(bucket))


def _sw_bias_kernel(tab_ref, bucket_ref, out_ref, *, tq):
    p = pl.program_id(1)
    bk = bucket_ref[0]
    for m in range(2):
        col = 2 * p + m
        acc = jnp.zeros(bk.shape, F32)
        for b in range(NUM_BUCKETS):
            acc = jnp.where(bk == b, tab_ref[b, col] * LOG2E, acc)
        out_ref[0, 0, :, m * tq:(m + 1) * tq] = jnp.where(bk < 0, NEG, acc)


def _sw_bias(rel_table, tq):
    band = SW_WINDOW + tq
    key = np.arange(band)[:, None]
    qry = np.arange(tq)[None, :]
    dist = qry + SW_WINDOW - key
    bucket = np.where((dist >= 0) & (dist < SW_WINDOW), _t5_bucket_np(dist), -1).astype(np.int32)
    first = np.where(key < SW_WINDOW, -1, bucket).astype(np.int32)
    return pl.pallas_call(
        functools.partial(_sw_bias_kernel, tq=tq),
        grid=(2, SW_Q_HEADS // 2),
        in_specs=[pl.BlockSpec(memory_space=pltpu.SMEM),
                  pl.BlockSpec((1, band, tq), lambda v, p: (v, 0, 0))],
        out_specs=pl.BlockSpec((1, 1, band, 2 * tq), lambda v, p: (v, p, 0, 0)),
        out_shape=jax.ShapeDtypeStruct((2, SW_Q_HEADS // 2, band, 2 * tq), F32),
        compiler_params=_cparams("arbitrary", "arbitrary"),
        name="sw_bias",
    )(rel_table, jnp.asarray(np.stack([bucket, first])))


def _rms_rows(x, g):
    ms = jnp.mean(x * x, axis=-1, keepdims=True)
    return x * lax.rsqrt(ms + EPS) * g


def _norm_proj_kernel(x_ref, g_ref, w_ref, flag_ref, gain_ref, o_ref, xn_ref, y0_ref, y1_ref):
    i = pl.program_id(0)
    j = pl.program_id(1)
    tm, tn = y0_ref.shape

    @pl.when((i == 0) & (j == 0))
    def _():
        y1_ref[...] = jnp.zeros(y1_ref.shape, F32)

    @pl.when(j == 0)
    def _():
        xn_ref[...] = _rms_rows(x_ref[...], g_ref[...]).astype(BF16)

    def run(parity):
        y_cur, y_prev = (y0_ref, y1_ref)[::1 - 2 * parity]
        lane = lax.broadcasted_iota(jnp.int32, (1, LANES), 1)
        low = lane < HEAD_DIM
        for r in range(tm // PROJ_ROWS):
            rows = slice(r * PROJ_ROWS, (r + 1) * PROJ_ROWS)
            for c in range(tn // LANES):
                sl = slice(c * LANES, (c + 1) * LANES)
                yc = y_prev[rows, sl]
                y2 = yc * yc
                ms_lo = jnp.sum(jnp.where(low, y2, 0.0), axis=-1, keepdims=True) * (1.0 / HEAD_DIM)
                ms_hi = jnp.sum(jnp.where(low, 0.0, y2), axis=-1, keepdims=True) * (1.0 / HEAD_DIM)
                inv = jnp.where(low, lax.rsqrt(ms_lo + EPS), lax.rsqrt(ms_hi + EPS))
                scale = jnp.where(flag_ref[:, sl] > 0.0, inv, 1.0) * gain_ref[:, sl]
                o_ref[rows, sl] = (yc * scale).astype(BF16)
            y_cur[rows, :] = jnp.dot(xn_ref[rows, :], w_ref[...], preferred_element_type=F32)

    even = (j % 2) == 0
    pl.when(even)(functools.partial(run, 0))
    pl.when(jnp.logical_not(even))(functools.partial(run, 1))


def _norm_proj(x, g, w, flag, gain, tn):
    t, d = x.shape
    n = w.shape[1]
    tm = PROJ_TM
    assert t % tm == 0 and n % tn == 0 and tm % PROJ_ROWS == 0
    nj = n // tn

    def tile(j):
        return jnp.clip(j, 0, nj - 1)

    return pl.pallas_call(
        _norm_proj_kernel,
        grid=(t // tm, nj + 1),
        in_specs=[pl.BlockSpec((tm, d), lambda i, j: (i, 0)),
                  pl.BlockSpec((1, d), lambda i, j: (0, 0)),
                  pl.BlockSpec((d, tn), lambda i, j: (0, tile(j))),
                  pl.BlockSpec((1, tn), lambda i, j: (0, tile(j - 1))),
                  pl.BlockSpec((1, tn), lambda i, j: (0, tile(j - 1)))],
        out_specs=pl.BlockSpec((tm, tn), lambda i, j: (i, tile(j - 1))),
        out_shape=jax.ShapeDtypeStruct((t, n), BF16),
        scratch_shapes=[pltpu.VMEM((tm, d), BF16), pltpu.VMEM((tm, tn), F32), pltpu.VMEM((tm, tn), F32)],
        compiler_params=_cparams("arbitrary", "arbitrary"),
        name="norm_proj",
    )(x, g, w, flag, gain)


def _out_proj_kernel(a_ref, w_ref, h_ref, o_ref):
    o_ref[...] = h_ref[...] + jnp.dot(a_ref[...], w_ref[...], preferred_element_type=F32)


def _out_proj(a, w, h):
    t, k = a.shape
    n = w.shape[1]
    tm, tn = OUT_TM, OUT_TN
    assert t % tm == 0 and n % tn == 0
    return pl.pallas_call(
        _out_proj_kernel,
        grid=(t // tm, n // tn),
        in_specs=[pl.BlockSpec((tm, k), lambda i, j: (i, 0)),
                  pl.BlockSpec((k, tn), lambda i, j: (0, j)),
                  pl.BlockSpec((tm, tn), lambda i, j: (i, j))],
        out_specs=pl.BlockSpec((tm, tn), lambda i, j: (i, j)),
        out_shape=jax.ShapeDtypeStruct((t, n), F32),
        compiler_params=_cparams("arbitrary", "arbitrary"),
        name="out_proj",
    )(a, w, h)


def _stack_maps(q):
    lane = lax.broadcasted_iota(jnp.int32, q.shape, 1)
    qf = q.astype(F32)
    return jnp.concatenate([jnp.where(lane < HEAD_DIM, qf, 0.0), jnp.where(lane < HEAD_DIM, 0.0, qf)],
                           axis=0).astype(BF16)


def _da_attn_kernel(fast_ref, q_ref, k_ref, vt_ref, bias_ref, lamv_ref, sg_ref, o_ref, acc_ref, l_ref, m_ref, *, t,
                    lam_init):
    i = pl.program_id(2)
    qt = q_ref[...].astype(F32).T
    row = lax.broadcasted_iota(jnp.int32, qt.shape, 0)
    qqt = jnp.concatenate([jnp.where(row < HEAD_DIM, qt, 0.0), jnp.where(row < HEAD_DIM, 0.0, qt)],
                          axis=1).astype(BF16)
    acc_ref[...] = jnp.zeros(acc_ref.shape, F32)
    l_ref[...] = jnp.zeros(l_ref.shape, F32)

    def scores(u, bias):
        k = k_ref[pl.ds(pl.multiple_of(u * t, t), t), :]
        st = jnp.dot(k, qqt, preferred_element_type=F32)
        return st if bias is None else st + bias

    def chunk(u0, biases):
        pv = None
        ls = None
        for n, bias in enumerate(biases):
            p = jnp.exp2(scores(u0 + n, bias))
            s = jnp.sum(p, axis=0, keepdims=True)
            d = jnp.dot(vt_ref[u0 + n], p.astype(BF16), preferred_element_type=F32)
            pv, ls = (d, s) if pv is None else (pv + d, ls + s)
        acc_ref[...] += pv
        l_ref[...] += ls

    def unit_running_max(u, bias):
        st = scores(u, bias)
        m_prev = m_ref[...]
        m_new = jnp.maximum(m_prev, jnp.max(st, axis=0, keepdims=True))
        alpha = jnp.exp2(m_prev - m_new)
        p = jnp.exp2(st - m_new)
        l_ref[...] = alpha * l_ref[...] + jnp.sum(p, axis=0, keepdims=True)
        acc_ref[...] = alpha * acc_ref[...] + jnp.dot(vt_ref[u], p.astype(BF16), preferred_element_type=F32)
        m_ref[...] = m_new

    fast = fast_ref[0] == 1
    odd = (i % 2) == 1

    @pl.when(fast)
    def _():
        def far_pair(c, carry):
            chunk(2 * c, (None, None))
            return carry

        lax.fori_loop(0, jnp.maximum(i - 1, 0) // 2, far_pair, 0)

    @pl.when(fast & (i == 0))
    def _():
        chunk(0, (bias_ref[0, 0],))

    @pl.when(fast & odd)
    def _():
        chunk(i - 1, (bias_ref[0, 1], bias_ref[0, 0]))

    @pl.when(fast & jnp.logical_not(odd) & (i > 0))
    def _():
        chunk(i - 2, (None, bias_ref[0, 1], bias_ref[0, 0]))

    @pl.when(jnp.logical_not(fast))
    def _():
        m_ref[...] = jnp.full(m_ref.shape, NEG, F32)

        def far_unit(u, carry):
            unit_running_max(u, None)
            return carry

        lax.fori_loop(0, jnp.maximum(i - 1, 0), far_unit, 0)

        @pl.when(i >= 1)
        def _():
            unit_running_max(i - 1, bias_ref[0, 1])

        unit_running_max(i, bias_ref[0, 0])

    lv = lamv_ref[...]
    lam = (jnp.exp(jnp.sum(lv[0:1] * lv[1:2], axis=-1, keepdims=True))
           - jnp.exp(jnp.sum(lv[2:3] * lv[3:4], axis=-1, keepdims=True)) + lam_init)
    o = acc_ref[...] * (1.0 / l_ref[...])
    ot = o[:, :t] - lam * o[:, t:]
    ms = jnp.mean(ot * ot, axis=0, keepdims=True)
    ot = ot * lax.rsqrt(ms + EPS) * (sg_ref[...] * (1.0 - lam_init))
    o_ref[...] = ot.T.astype(BF16)


def _da_attn(fast, qkv, vt, bias, lamv, sub_gain, batch, seq, lam_init):
    t = DA_T
    nq = seq // t
    return pl.pallas_call(
        functools.partial(_da_attn_kernel, t=t, lam_init=lam_init),
        grid=(batch, DA_HEADS, nq),
        in_specs=[pl.BlockSpec(memory_space=pltpu.SMEM),
                  pl.BlockSpec((t, LANES), lambda b, h, i: (b * nq + i, h)),
                  pl.BlockSpec((seq, LANES), lambda b, h, i: (b, DA_HEADS + h)),
                  pl.BlockSpec((None, None, nq, DA_VDIM, t), lambda b, h, i: (b, h, 0, 0, 0)),
                  pl.BlockSpec((1, 2, t, 2 * t), lambda b, h, i: (h, 0, 0, 0)),
                  pl.BlockSpec((4, HEAD_DIM), lambda b, h, i: (0, 0)),
                  pl.BlockSpec((DA_VDIM, 1), lambda b, h, i: (0, 0))],
        out_specs=pl.BlockSpec((t, LANES), lambda b, h, i: (b * nq + i, h)),
        out_shape=jax.ShapeDtypeStruct((batch * seq, DA_HEADS * DA_VDIM), BF16),
        scratch_shapes=[pltpu.VMEM((DA_VDIM, 2 * t), F32), pltpu.VMEM((1, 2 * t), F32), pltpu.VMEM((1, 2 * t), F32)],
        compiler_params=_cparams("arbitrary", "arbitrary", "arbitrary"),
        name="da_attn",
    )(fast, qkv, qkv, vt, bias, lamv, sub_gain)


def _sw_attn_kernel(fast_ref, sink_ref, q_ref, kp_ref, kc_ref, vp_ref, vc_ref, bias_ref, o_ref, *, tq):
    g = pl.program_id(0)
    kb = jnp.concatenate([kp_ref[...], kc_ref[...]], axis=0)
    vtb = jnp.concatenate([vp_ref[...], vc_ref[...]], axis=1)
    col = lax.broadcasted_iota(jnp.int32, (1, 2 * tq), 1)

    def lane_slice(pp):
        return slice(pp * LANES, (pp + 1) * LANES)

    def scores(pp):
        qt = q_ref[:, lane_slice(pp)].astype(F32).T
        row = lax.broadcasted_iota(jnp.int32, qt.shape, 0)
        qqt = jnp.concatenate([jnp.where(row < HEAD_DIM, qt, 0.0), jnp.where(row < HEAD_DIM, 0.0, qt)],
                              axis=1).astype(BF16)
        return jnp.dot(kb, qqt, preferred_element_type=F32) + bias_ref[0, pp]

    def weighted_values(pp, st, fast):
        head = g * SW_GROUP + 2 * pp
        sink = jnp.where(col < tq, sink_ref[head], sink_ref[head + 1]) * LOG2E
        if fast:
            ot = jnp.dot(vtb, jnp.exp2(st).astype(BF16), preferred_element_type=F32)
            return ot, jnp.exp2(sink)
        m = jnp.maximum(jnp.max(st, axis=0, keepdims=True), sink)
        ot = jnp.dot(vtb, jnp.exp2(st - m).astype(BF16), preferred_element_type=F32)
        return ot, jnp.exp2(sink - m)

    def finish(pp, ot, sink_term):
        ot = ot[:HEAD_DIM] * (1.0 / (ot[HEAD_DIM:HEAD_DIM + 1] + sink_term))
        o_ref[:, lane_slice(pp)] = jnp.concatenate([ot[:, :tq], ot[:, tq:]], axis=0).T.astype(BF16)

    def all_pairs(fast):
        pairs = range(SW_GROUP // 2)
        sts = [scores(pp) for pp in pairs]
        ots = [weighted_values(pp, sts[pp], fast) for pp in pairs]
        for pp in pairs:
            finish(pp, *ots[pp])

    pl.when(fast_ref[0] == 1)(functools.partial(all_pairs, True))
    pl.when(fast_ref[0] != 1)(functools.partial(all_pairs, False))


def _sw_attn(fast, q, kdup, vt, bias, sinks, batch, seq):
    tq = SW_TQ
    nq = seq // tq
    r = tq // SW_WINDOW
    band = SW_WINDOW + tq
    gw = SW_GROUP * HEAD_DIM
    return pl.pallas_call(
        functools.partial(_sw_attn_kernel, tq=tq),
        grid=(SW_KV_HEADS, batch, nq),
        in_specs=[pl.BlockSpec(memory_space=pltpu.SMEM),
                  pl.BlockSpec(memory_space=pltpu.SMEM),
                  pl.BlockSpec((tq, gw), lambda g, b, i: (b * nq + i, g)),
                  pl.BlockSpec((SW_WINDOW, LANES), lambda g, b, i: (jnp.maximum((b * nq + i) * r - 1, 0), g)),
                  pl.BlockSpec((tq, LANES), lambda g, b, i: (b * nq + i, g)),
                  pl.BlockSpec((None, None, SW_VROWS, SW_WINDOW), lambda g, b, i: (b, g, 0, jnp.maximum(i * r - 1, 0))),
                  pl.BlockSpec((None, None, SW_VROWS, tq), lambda g, b, i: (b, g, 0, i)),
                  pl.BlockSpec((1, SW_GROUP // 2, band, 2 * tq), lambda g, b, i: (jnp.where(i == 0, 1, 0), g, 0, 0))],
        out_specs=pl.BlockSpec((tq, gw), lambda g, b, i: (b * nq + i, g)),
        out_shape=jax.ShapeDtypeStruct((batch * seq, SW_Q_HEADS * HEAD_DIM), BF16),
        compiler_params=_cparams("arbitrary", "arbitrary", "arbitrary"),
        name="sw_attn",
    )(fast, sinks, q, kdup, kdup, vt, vt, bias)


def _causal_conv(u, halo, cw_ref, cb_ref):
    row = lax.broadcasted_iota(jnp.int32, (8, 1), 0)
    r1 = pltpu.roll(u, 1, axis=0)
    r2 = pltpu.roll(u, 2, axis=0)
    u1 = jnp.concatenate([jnp.where(row == 0, halo[7:8], r1[:8]), r1[8:]], axis=0)
    u2 = jnp.concatenate([jnp.where(row == 0, halo[6:7], jnp.where(row == 1, halo[7:8], r2[:8])), r2[8:]], axis=0)
    return cw_ref[0:1, :] * u2 + cw_ref[1:2, :] * u1 + cw_ref[2:3, :] * u + cb_ref[...]


def _ffn_kernel(h_ref, g_ref, wg_ref, wu_ref, cwg_ref, cwu_ref, cbg_ref, cbu_ref, wo_ref, o_ref,
                f_ref, act_ref, halo_g_ref, halo_u_ref, *, tiles_per_seq, nj):
    i = pl.program_id(0)
    j = pl.program_id(1)
    tm = h_ref.shape[0]

    @pl.when(j == 0)
    def _():
        x = h_ref[...]
        f_ref[...] = _rms_rows(x, g_ref[...]).astype(BF16)
        o_ref[...] = x

    @pl.when(((i % tiles_per_seq) == 0) & (j < nj))
    def _():
        halo_g_ref[j] = jnp.zeros(halo_g_ref.shape[1:], F32)
        halo_u_ref[j] = jnp.zeros(halo_u_ref.shape[1:], F32)

    def run(do_up, do_down):
        if do_up:
            f = f_ref[...]
            ug = jnp.dot(f, wg_ref[...], preferred_element_type=F32)
            uu = jnp.dot(f, wu_ref[...], preferred_element_type=F32)
        if do_down:
            o_ref[...] += jnp.dot(act_ref[(j + 1) % 2], wo_ref[...], preferred_element_type=F32)
        if do_up:
            def conv(u, cw_ref, cb_ref, halo_ref):
                halo = halo_ref[j]
                halo_ref[j] = u[tm - 8:, :]
                return _causal_conv(u, halo, cw_ref, cb_ref)

            gate = conv(ug, cwg_ref, cbg_ref, halo_g_ref)
            up = conv(uu, cwu_ref, cbu_ref, halo_u_ref)
            act_ref[j % 2] = (gate * (1.0 / (1.0 + jnp.exp(-gate))) * up).astype(BF16)

    pl.when(j == 0)(functools.partial(run, True, False))
    pl.when((j > 0) & (j < nj))(functools.partial(run, True, True))
    pl.when(j == nj)(functools.partial(run, False, True))


def _ffn(h, g, w_in, conv_w, conv_b, w_out, seq):
    t, d = h.shape
    dff = w_out.shape[0]
    tm, tf = FFN_TM, FFN_TF
    assert t % tm == 0 and seq % tm == 0 and dff % tf == 0
    nj = dff // tf

    def up_tile(j):
        return jnp.minimum(j, nj - 1)

    return pl.pallas_call(
        functools.partial(_ffn_kernel, tiles_per_seq=seq // tm, nj=nj),
        grid=(t // tm, nj + 1),
        in_specs=[pl.BlockSpec((tm, d), lambda i, j: (i, 0)),
                  pl.BlockSpec((1, d), lambda i, j: (0, 0)),
                  pl.BlockSpec((d, tf), lambda i, j: (0, up_tile(j))),
                  pl.BlockSpec((d, tf), lambda i, j: (0, nj + up_tile(j))),
                  pl.BlockSpec((CONV_WIDTH, tf), lambda i, j: (0, up_tile(j))),
                  pl.BlockSpec((CONV_WIDTH, tf), lambda i, j: (0, nj + up_tile(j))),
                  pl.BlockSpec((1, tf), lambda i, j: (0, up_tile(j))),
                  pl.BlockSpec((1, tf), lambda i, j: (0, nj + up_tile(j))),
                  pl.BlockSpec((tf, d), lambda i, j: (jnp.maximum(j - 1, 0), 0))],
        out_specs=pl.BlockSpec((tm, d), lambda i, j: (i, 0)),
        out_shape=jax.ShapeDtypeStruct((t, d), F32),
        scratch_shapes=[pltpu.VMEM((tm, d), BF16),
                        pltpu.VMEM((2, tm, tf), BF16),
                        pltpu.VMEM((nj, 8, tf), F32),
                        pltpu.VMEM((nj, 8, tf), F32)],
        compiler_params=_cparams("arbitrary", "arbitrary"),
        name="conv_ffn",
    )(h, g, w_in, w_in, conv_w, conv_w, conv_b, conv_b, w_out)


def _row(v):
    return v.reshape(1, -1).astype(F32)


def kernel(x, rel_table, attn_norm_g, ffn_norm_g, da_w_qkv, da_w_o, da_q_gain, da_k_gain, da_lam_q1, da_lam_k1, da_lam_q2, da_lam_k2, da_sub_gain, sw_w_qkv, sw_w_o, sw_q_gain, sw_k_gain, sw_sinks, ffn_w_in, ffn_conv_w, ffn_conv_b, ffn_w_out):
    batch, seq, d = x.shape
    depth = attn_norm_g.shape[0]
    t = batch * seq
    h = x.reshape(t, d)
    q_scale = HEAD_DIM ** -0.5

    da_bias = _da_bias(rel_table, DA_T)
    sw_bias = _sw_bias(rel_table, SW_TQ)

    da_qk = 2 * DA_HEADS * HEAD_DIM
    da_flag = jnp.concatenate([jnp.ones((1, 2 * da_qk), F32), jnp.zeros((1, DA_HEADS * DA_VDIM), F32)], axis=1)
    sw_qw = SW_Q_HEADS * HEAD_DIM
    sw_kw = SW_KV_HEADS * HEAD_DIM
    sw_flag = jnp.concatenate([jnp.ones((1, sw_qw + sw_kw), F32), jnp.zeros((1, sw_kw), F32)], axis=1)

    for i in range(depth):
        j = i // N_MIXERS
        if i % N_MIXERS == 0:
            gain = jnp.concatenate([jnp.tile(da_q_gain[j], da_qk // HEAD_DIM) * (q_scale * LOG2E),
                                    jnp.tile(da_k_gain[j], da_qk // HEAD_DIM),
                                    jnp.ones((DA_HEADS * DA_VDIM,), F32)])
            qkv = _norm_proj(h, _row(attn_norm_g[i]), da_w_qkv[j].astype(BF16), da_flag, _row(gain), PROJ_TN_DA)
            nk = seq // DA_T
            vt = qkv[:, 2 * da_qk:].reshape(batch, nk, DA_T, DA_HEADS, DA_VDIM).transpose(0, 3, 1, 4, 2)
            score_bound = (HEAD_DIM * q_scale * jnp.max(jnp.abs(da_q_gain[j])) * jnp.max(jnp.abs(da_k_gain[j]))
                           + jnp.max(jnp.abs(rel_table - rel_table[NUM_BUCKETS - 1]))) * LOG2E
            fast = (score_bound <= FAST_BOUND).astype(jnp.int32).reshape(1)
            lamv = jnp.stack([da_lam_q1[j], da_lam_k1[j], da_lam_q2[j], da_lam_k2[j]]).astype(F32)
            lam_init = 0.8 - 0.6 * math.exp(-0.3 * i)
            a = _da_attn(fast, qkv, vt, da_bias, lamv, da_sub_gain[j].reshape(DA_VDIM, 1).astype(F32), batch, seq,
                         lam_init)
            h = _out_proj(a, da_w_o[j].astype(BF16), h)
        else:
            gain = jnp.concatenate([jnp.tile(sw_q_gain[j], SW_Q_HEADS) * (q_scale * LOG2E),
                                    jnp.tile(sw_k_gain[j], SW_KV_HEADS),
                                    jnp.ones((sw_kw,), F32)])
            qkv = _norm_proj(h, _row(attn_norm_g[i]), sw_w_qkv[j].astype(BF16), sw_flag, _row(gain), PROJ_TN_SW)
            k = qkv[:, sw_qw:sw_qw + sw_kw].reshape(t, SW_KV_HEADS, 1, HEAD_DIM)
            kdup = jnp.broadcast_to(k, (t, SW_KV_HEADS, 2, HEAD_DIM)).reshape(t, SW_KV_HEADS * LANES)
            vt = qkv[:, sw_qw + sw_kw:].reshape(batch, seq, SW_KV_HEADS, HEAD_DIM).transpose(0, 2, 3, 1)
            ones_rows = jnp.zeros((SW_VROWS - HEAD_DIM, seq), BF16).at[0].set(1.0)
            vt = jnp.concatenate([vt, jnp.broadcast_to(ones_rows, vt.shape[:2] + ones_rows.shape)], axis=2)
            score_bound = (HEAD_DIM * q_scale * jnp.max(jnp.abs(sw_q_gain[j])) * jnp.max(jnp.abs(sw_k_gain[j]))
                           + jnp.max(jnp.abs(rel_table)) + jnp.max(jnp.abs(sw_sinks[j]))) * LOG2E
            fast = (score_bound <= FAST_BOUND).astype(jnp.int32).reshape(1)
            a = _sw_attn(fast, qkv, kdup, vt, sw_bias, sw_sinks[j].astype(F32), batch, seq)
            h = _out_proj(a, sw_w_o[j].astype(BF16), h)
        h = _ffn(h, _row(ffn_norm_g[i]), ffn_w_in[i].astype(BF16), ffn_conv_w[i].astype(F32),
                 _row(ffn_conv_b[i]), ffn_w_out[i].astype(BF16), seq)
    return h.reshape(batch, seq, d)
```

```python
import functools
import math

import numpy as np
import jax
import jax.numpy as jnp
from jax import lax
from jax.experimental import pallas as pl
from jax.experimental.pallas import tpu as pltpu

F32 = jnp.float32
BF16 = jnp.bfloat16

HEAD_DIM = 64
N_MAPS = 32
DA_HEADS = 16
DA_VDIM = 128
SW_Q_HEADS = 32
SW_KV_HEADS = 4
SW_GROUP = SW_Q_HEADS // SW_KV_HEADS
SW_WINDOW = 128
NUM_BUCKETS = 32
MAX_DISTANCE = 128
CONV_WIDTH = 3
EPS = 1e-6
N_MIXERS = 2

LANES = 128
NEG = -1e30
LOG2E = math.log2(math.e)
VMEM_LIMIT = 56 * 1024 * 1024

PROJ_TM = 1024
PROJ_TN_DA = 1024
PROJ_TN_SW = 1280
PROJ_ROWS = 256
OUT_TM = 1024
OUT_TN = 1024
FFN_TM = 512
FFN_TF = 512
DA_T = 512
FAST_BOUND = 60.0
SW_TQ = 256
SW_VROWS = HEAD_DIM + 16


def _cparams(*sem):
    return pltpu.CompilerParams(dimension_semantics=sem, vmem_limit_bytes=VMEM_LIMIT)


def _t5_bucket_np(dist):
    max_exact = NUM_BUCKETS // 2
    d = np.maximum(dist, 0)
    df = np.maximum(d, 1).astype(np.float64)
    large = max_exact + (np.log(df / max_exact) / math.log(MAX_DISTANCE / max_exact)
                         * (NUM_BUCKETS - max_exact)).astype(np.int32)
    large = np.minimum(large, NUM_BUCKETS - 1)
    return np.where(d < max_exact, d, large).astype(np.int32)


def _da_bias_kernel(tab_ref, bucket_ref, out_ref, *, t):
    h = pl.program_id(0)
    s = MAX_DISTANCE
    for off in range(2):
        for m in range(2):
            col = 2 * h + m
            far = tab_ref[NUM_BUCKETS - 1, col]
            for kr in range(t // s):
                for qc in range(t // s):
                    base = off * t + (qc - kr) * s
                    dst = (0, off, slice(kr * s, (kr + 1) * s), slice(m * t + qc * s, m * t + (qc + 1) * s))
                    if base - (s - 1) >= MAX_DISTANCE:
                        out_ref[dst] = jnp.zeros((s, s), F32)
                    elif base + (s - 1) < 0:
                        out_ref[dst] = jnp.full((s, s), NEG, F32)
                    else:
                        bk = bucket_ref[off, kr * s:(kr + 1) * s, qc * s:(qc + 1) * s]
                        acc = jnp.zeros(bk.shape, F32)
                        for b in range(NUM_BUCKETS - 1):
                            acc = jnp.where(bk == b, (tab_ref[b, col] - far) * LOG2E, acc)
                        out_ref[dst] = jnp.where(bk < 0, NEG, acc)


def _da_bias(rel_table, t):
    key = np.arange(t)[:, None]
    qry = np.arange(t)[None, :]
    d0 = qry - key
    d1 = t + qry - key
    bucket = np.stack([np.where(d0 >= 0, _t5_bucket_np(d0), -1), _t5_bucket_np(d1)]).astype(np.int32)
    return pl.pallas_call(
        functools.partial(_da_bias_kernel, t=t),
        grid=(DA_HEADS,),
        in_specs=[pl.BlockSpec(memory_space=pltpu.SMEM),
                  pl.BlockSpec((2, t, t), lambda h: (0, 0, 0))],
        out_specs=pl.BlockSpec((1, 2, t, 2 * t), lambda h: (h, 0, 0, 0)),
        out_shape=jax.ShapeDtypeStruct((DA_HEADS, 2, t, 2 * t), F32),
        compiler_params=_cparams("arbitrary"),
        name="da_bias",
    )(rel_table, jnp.asarray(bucket))


def _sw_bias_kernel(tab_ref, bucket_ref, out_ref, *, tq):
    p = pl.program_id(1)
    bk = bucket_ref[0]
    for m in range(2):
        col = 2 * p + m
        acc = jnp.zeros(bk.shape, F32)
        for b in range(NUM_BUCKETS):
            acc = jnp.where(bk == b, tab_ref[b, col] * LOG2E, acc)
        out_ref[0, 0, :, m * tq:(m + 1) * tq] = jnp.where(bk < 0, NEG, acc)


def _sw_bias(rel_table, tq):
    band = SW_WINDOW + tq
    key = np.arange(band)[:, None]
    qry = np.arange(tq)[None, :]
    dist = qry + SW_WINDOW - key
    bucket = np.where((dist >= 0) & (dist < SW_WINDOW), _t5_bucket_np(dist), -1).astype(np.int32)
    first = np.where(key < SW_WINDOW, -1, bucket).astype(np.int32)
    return pl.pallas_call(
        functools.partial(_sw_bias_kernel, tq=tq),
        grid=(2, SW_Q_HEADS // 2),
        in_specs=[pl.BlockSpec(memory_space=pltpu.SMEM),
                  pl.BlockSpec((1, band, tq), lambda v, p: (v, 0, 0))],
        out_specs=pl.BlockSpec((1, 1, band, 2 * tq), lambda v, p: (v, p, 0, 0)),
        out_shape=jax.ShapeDtypeStruct((2, SW_Q_HEADS // 2, band, 2 * tq), F32),
        compiler_params=_cparams("arbitrary", "arbitrary"),
        name="sw_bias",
    )(rel_table, jnp.asarray(np.stack([bucket, first])))


def _rms_rows(x, g):
    ms = jnp.mean(x * x, axis=-1, keepdims=True)
    return x * lax.rsqrt(ms + EPS) * g


def _norm_proj_kernel(x_ref, g_ref, w_ref, flag_ref, gain_ref, o_ref, xn_ref, y0_ref, y1_ref, *, nj):
    j = pl.program_id(1)
    tm, tn = y0_ref.shape

    @pl.when(j == 0)
    def _():
        xn_ref[...] = _rms_rows(x_ref[...], g_ref[...]).astype(BF16)

    def run(parity, do_norm, do_project):
        y_cur, y_prev = (y0_ref, y1_ref)[::1 - 2 * parity]
        lane = lax.broadcasted_iota(jnp.int32, (1, LANES), 1)
        low = lane < HEAD_DIM
        for r in range(tm // PROJ_ROWS):
            rows = slice(r * PROJ_ROWS, (r + 1) * PROJ_ROWS)
            if do_norm:
                for c in range(tn // LANES):
                    sl = slice(c * LANES, (c + 1) * LANES)
                    yc = y_prev[rows, sl]
                    y2 = yc * yc
                    ms_lo = jnp.sum(jnp.where(low, y2, 0.0), axis=-1, keepdims=True) * (1.0 / HEAD_DIM)
                    ms_hi = jnp.sum(jnp.where(low, 0.0, y2), axis=-1, keepdims=True) * (1.0 / HEAD_DIM)
                    inv = jnp.where(low, lax.rsqrt(ms_lo + EPS), lax.rsqrt(ms_hi + EPS))
                    scale = jnp.where(flag_ref[:, sl] > 0.0, inv, 1.0) * gain_ref[:, sl]
                    o_ref[rows, sl] = (yc * scale).astype(BF16)
            if do_project:
                y_cur[rows, :] = jnp.dot(xn_ref[rows, :], w_ref[...], preferred_element_type=F32)

    even = (j % 2) == 0
    inner = (j >= 1) & (j < nj)
    pl.when(j == 0)(functools.partial(run, 0, False, True))
    pl.when(inner & even)(functools.partial(run, 0, True, True))
    pl.when(inner & jnp.logical_not(even))(functools.partial(run, 1, True, True))
    pl.when(j == nj)(functools.partial(run, nj % 2, True, False))


def _norm_proj(x, g, w, flag, gain, tn):
    t, d = x.shape
    n = w.shape[1]
    tm = PROJ_TM
    assert t % tm == 0 and n % tn == 0 and tm % PROJ_ROWS == 0
    nj = n // tn

    def tile(j):
        return jnp.clip(j, 0, nj - 1)

    return pl.pallas_call(
        functools.partial(_norm_proj_kernel, nj=nj),
        grid=(t // tm, nj + 1),
        in_specs=[pl.BlockSpec((tm, d), lambda i, j: (i, 0)),
                  pl.BlockSpec((1, d), lambda i, j: (0, 0)),
                  pl.BlockSpec((d, tn), lambda i, j: (0, tile(j))),
                  pl.BlockSpec((1, tn), lambda i, j: (0, tile(j - 1))),
                  pl.BlockSpec((1, tn), lambda i, j: (0, tile(j - 1)))],
        out_specs=pl.BlockSpec((tm, tn), lambda i, j: (i, tile(j - 1))),
        out_shape=jax.ShapeDtypeStruct((t, n), BF16),
        scratch_shapes=[pltpu.VMEM((tm, d), BF16), pltpu.VMEM((tm, tn), F32), pltpu.VMEM((tm, tn), F32)],
        compiler_params=_cparams("arbitrary", "arbitrary"),
        name="norm_proj",
    )(x, g, w, flag, gain)


def _out_proj_kernel(a_ref, w_ref, h_ref, o_ref):
    o_ref[...] = h_ref[...] + jnp.dot(a_ref[...], w_ref[...], preferred_element_type=F32)


def _out_proj(a, w, h):
    t, k = a.shape
    n = w.shape[1]
    tm, tn = OUT_TM, OUT_TN
    assert t % tm == 0 and n % tn == 0
    return pl.pallas_call(
        _out_proj_kernel,
        grid=(t // tm, n // tn),
        in_specs=[pl.BlockSpec((tm, k), lambda i, j: (i, 0)),
                  pl.BlockSpec((k, tn), lambda i, j: (0, j)),
                  pl.BlockSpec((tm, tn), lambda i, j: (i, j))],
        out_specs=pl.BlockSpec((tm, tn), lambda i, j: (i, j)),
        out_shape=jax.ShapeDtypeStruct((t, n), F32),
        compiler_params=_cparams("arbitrary", "arbitrary"),
        name="out_proj",
    )(a, w, h)


def _stack_maps(q):
    lane = lax.broadcasted_iota(jnp.int32, q.shape, 1)
    qf = q.astype(F32)
    return jnp.concatenate([jnp.where(lane < HEAD_DIM, qf, 0.0), jnp.where(lane < HEAD_DIM, 0.0, qf)],
                           axis=0).astype(BF16)


def _da_attn_kernel(fast_ref, q_ref, k_ref, vt_ref, bias_ref, lamv_ref, sg_ref, o_ref, acc_ref, l_ref, m_ref, *, t,
                    lam_init):
    i = pl.program_id(2)
    qt = q_ref[...].astype(F32).T
    row = lax.broadcasted_iota(jnp.int32, qt.shape, 0)
    qqt = jnp.concatenate([jnp.where(row < HEAD_DIM, qt, 0.0), jnp.where(row < HEAD_DIM, 0.0, qt)],
                          axis=1).astype(BF16)
    acc_ref[...] = jnp.zeros(acc_ref.shape, F32)
    l_ref[...] = jnp.zeros(l_ref.shape, F32)

    def scores(u, bias):
        k = k_ref[pl.ds(pl.multiple_of(u * t, t), t), :]
        st = jnp.dot(k, qqt, preferred_element_type=F32)
        return st if bias is None else st + bias

    def chunk(u0, biases):
        pv = None
        ls = None
        for n, bias in enumerate(biases):
            p = jnp.exp2(scores(u0 + n, bias))
            s = jnp.sum(p, axis=0, keepdims=True)
            d = jnp.dot(vt_ref[u0 + n], p.astype(BF16), preferred_element_type=F32)
            pv, ls = (d, s) if pv is None else (pv + d, ls + s)
        acc_ref[...] += pv
        l_ref[...] += ls

    def unit_running_max(u, bias):
        st = scores(u, bias)
        m_prev = m_ref[...]
        m_new = jnp.maximum(m_prev, jnp.max(st, axis=0, keepdims=True))
        alpha = jnp.exp2(m_prev - m_new)
        p = jnp.exp2(st - m_new)
        l_ref[...] = alpha * l_ref[...] + jnp.sum(p, axis=0, keepdims=True)
        acc_ref[...] = alpha * acc_ref[...] + jnp.dot(vt_ref[u], p.astype(BF16), preferred_element_type=F32)
        m_ref[...] = m_new

    fast = fast_ref[0] == 1
    odd = (i % 2) == 1

    @pl.when(fast)
    def _():
        def far_pair(c, carry):
            chunk(2 * c, (None, None))
            return carry

        lax.fori_loop(0, jnp.maximum(i - 1, 0) // 2, far_pair, 0)

    @pl.when(fast & (i == 0))
    def _():
        chunk(0, (bias_ref[0, 0],))

    @pl.when(fast & odd)
    def _():
        chunk(i - 1, (bias_ref[0, 1], bias_ref[0, 0]))

    @pl.when(fast & jnp.logical_not(odd) & (i > 0))
    def _():
        chunk(i - 2, (None, bias_ref[0, 1], bias_ref[0, 0]))

    @pl.when(jnp.logical_not(fast))
    def _():
        m_ref[...] = jnp.full(m_ref.shape, NEG, F32)

        def far_unit(u, carry):
            unit_running_max(u, None)
            return carry

        lax.fori_loop(0, jnp.maximum(i - 1, 0), far_unit, 0)

        @pl.when(i >= 1)
        def _():
            unit_running_max(i - 1, bias_ref[0, 1])

        unit_running_max(i, bias_ref[0, 0])

    lv = lamv_ref[...]
    lam = (jnp.exp(jnp.sum(lv[0:1] * lv[1:2], axis=-1, keepdims=True))
           - jnp.exp(jnp.sum(lv[2:3] * lv[3:4], axis=-1, keepdims=True)) + lam_init)
    o = acc_ref[...] * (1.0 / l_ref[...])
    ot = o[:, :t] - lam * o[:, t:]
    ms = jnp.mean(ot * ot, axis=0, keepdims=True)
    ot = ot * lax.rsqrt(ms + EPS) * (sg_ref[...] * (1.0 - lam_init))
    o_ref[...] = ot.T.astype(BF16)


def _da_attn(fast, qkv, vt, bias, lamv, sub_gain, batch, seq, lam_init):
    t = DA_T
    nq = seq // t
    return pl.pallas_call(
        functools.partial(_da_attn_kernel, t=t, lam_init=lam_init),
        grid=(batch, DA_HEADS, nq),
        in_specs=[pl.BlockSpec(memory_space=pltpu.SMEM),
                  pl.BlockSpec((t, LANES), lambda b, h, i: (b * nq + i, h)),
                  pl.BlockSpec((seq, LANES), lambda b, h, i: (b, DA_HEADS + h)),
                  pl.BlockSpec((None, None, nq, DA_VDIM, t), lambda b, h, i: (b, h, 0, 0, 0)),
                  pl.BlockSpec((1, 2, t, 2 * t), lambda b, h, i: (h, 0, 0, 0)),
                  pl.BlockSpec((4, HEAD_DIM), lambda b, h, i: (0, 0)),
                  pl.BlockSpec((DA_VDIM, 1), lambda b, h, i: (0, 0))],
        out_specs=pl.BlockSpec((t, LANES), lambda b, h, i: (b * nq + i, h)),
        out_shape=jax.ShapeDtypeStruct((batch * seq, DA_HEADS * DA_VDIM), BF16),
        scratch_shapes=[pltpu.VMEM((DA_VDIM, 2 * t), F32), pltpu.VMEM((1, 2 * t), F32), pltpu.VMEM((1, 2 * t), F32)],
        compiler_params=_cparams("arbitrary", "arbitrary", "arbitrary"),
        name="da_attn",
    )(fast, qkv, qkv, vt, bias, lamv, sub_gain)


def _sw_attn_kernel(fast_ref, sink_ref, q_ref, kp_ref, kc_ref, vp_ref, vc_ref, bias_ref, o_ref, *, tq):
    g = pl.program_id(0)
    kb = jnp.concatenate([kp_ref[...], kc_ref[...]], axis=0)
    vtb = jnp.concatenate([vp_ref[...], vc_ref[...]], axis=1)
    col = lax.broadcasted_iota(jnp.int32, (1, 2 * tq), 1)

    def lane_slice(pp):
        return slice(pp * LANES, (pp + 1) * LANES)

    def scores(pp):
        qt = q_ref[:, lane_slice(pp)].astype(F32).T
        row = lax.broadcasted_iota(jnp.int32, qt.shape, 0)
        qqt = jnp.concatenate([jnp.where(row < HEAD_DIM, qt, 0.0), jnp.where(row < HEAD_DIM, 0.0, qt)],
                              axis=1).astype(BF16)
        return jnp.dot(kb, qqt, preferred_element_type=F32) + bias_ref[0, pp]

    def weighted_values(pp, st, fast):
        head = g * SW_GROUP + 2 * pp
        sink = jnp.where(col < tq, sink_ref[head], sink_ref[head + 1]) * LOG2E
        if fast:
            ot = jnp.dot(vtb, jnp.exp2(st).astype(BF16), preferred_element_type=F32)
            return ot, jnp.exp2(sink)
        m = jnp.maximum(jnp.max(st, axis=0, keepdims=True), sink)
        ot = jnp.dot(vtb, jnp.exp2(st - m).astype(BF16), preferred_element_type=F32)
        return ot, jnp.exp2(sink - m)

    def finish(pp, ot, sink_term):
        ot = ot[:HEAD_DIM] * (1.0 / (ot[HEAD_DIM:HEAD_DIM + 1] + sink_term))
        o_ref[:, lane_slice(pp)] = jnp.concatenate([ot[:, :tq], ot[:, tq:]], axis=0).T.astype(BF16)

    def all_pairs(fast):
        pairs = range(SW_GROUP // 2)
        sts = [scores(pp) for pp in pairs]
        ots = [weighted_values(pp, sts[pp], fast) for pp in pairs]
        for pp in pairs:
            finish(pp, *ots[pp])

    pl.when(fast_ref[0] == 1)(functools.partial(all_pairs, True))
    pl.when(fast_ref[0] != 1)(functools.partial(all_pairs, False))


def _sw_attn(fast, q, kdup, vt, bias, sinks, batch, seq):
    tq = SW_TQ
    nq = seq // tq
    r = tq // SW_WINDOW
    band = SW_WINDOW + tq
    gw = SW_GROUP * HEAD_DIM
    return pl.pallas_call(
        functools.partial(_sw_attn_kernel, tq=tq),
        grid=(SW_KV_HEADS, batch, nq),
        in_specs=[pl.BlockSpec(memory_space=pltpu.SMEM),
                  pl.BlockSpec(memory_space=pltpu.SMEM),
                  pl.BlockSpec((tq, gw), lambda g, b, i: (b * nq + i, g)),
                  pl.BlockSpec((SW_WINDOW, LANES), lambda g, b, i: (jnp.maximum((b * nq + i) * r - 1, 0), g)),
                  pl.BlockSpec((tq, LANES), lambda g, b, i: (b * nq + i, g)),
                  pl.BlockSpec((None, None, SW_VROWS, SW_WINDOW), lambda g, b, i: (b, g, 0, jnp.maximum(i * r - 1, 0))),
                  pl.BlockSpec((None, None, SW_VROWS, tq), lambda g, b, i: (b, g, 0, i)),
                  pl.BlockSpec((1, SW_GROUP // 2, band, 2 * tq), lambda g, b, i: (jnp.where(i == 0, 1, 0), g, 0, 0))],
        out_specs=pl.BlockSpec((tq, gw), lambda g, b, i: (b * nq + i, g)),
        out_shape=jax.ShapeDtypeStruct((batch * seq, SW_Q_HEADS * HEAD_DIM), BF16),
        compiler_params=_cparams("arbitrary", "arbitrary", "arbitrary"),
        name="sw_attn",
    )(fast, sinks, q, kdup, kdup, vt, vt, bias)


def _causal_conv(u, halo, cw_ref, cb_ref):
    row = lax.broadcasted_iota(jnp.int32, (8, 1), 0)
    r1 = pltpu.roll(u, 1, axis=0)
    r2 = pltpu.roll(u, 2, axis=0)
    u1 = jnp.concatenate([jnp.where(row == 0, halo[7:8], r1[:8]), r1[8:]], axis=0)
    u2 = jnp.concatenate([jnp.where(row == 0, halo[6:7], jnp.where(row == 1, halo[7:8], r2[:8])), r2[8:]], axis=0)
    return cw_ref[0:1, :] * u2 + cw_ref[1:2, :] * u1 + cw_ref[2:3, :] * u + cb_ref[...]


def _ffn_kernel(h_ref, g_ref, wg_ref, wu_ref, cwg_ref, cwu_ref, cbg_ref, cbu_ref, wo_ref, o_ref,
                f_ref, act_ref, halo_g_ref, halo_u_ref, *, tiles_per_seq, nj):
    i = pl.program_id(0)
    j = pl.program_id(1)
    tm = h_ref.shape[0]

    @pl.when(j == 0)
    def _():
        x = h_ref[...]
        f_ref[...] = _rms_rows(x, g_ref[...]).astype(BF16)
        o_ref[...] = x

    @pl.when(((i % tiles_per_seq) == 0) & (j < nj))
    def _():
        halo_g_ref[j] = jnp.zeros(halo_g_ref.shape[1:], F32)
        halo_u_ref[j] = jnp.zeros(halo_u_ref.shape[1:], F32)

    def run(do_up, do_down):
        if do_up:
            f = f_ref[...]
            ug = jnp.dot(f, wg_ref[...], preferred_element_type=F32)
            uu = jnp.dot(f, wu_ref[...], preferred_element_type=F32)
        if do_down:
            o_ref[...] += jnp.dot(act_ref[(j + 1) % 2], wo_ref[...], preferred_element_type=F32)
        if do_up:
            def conv(u, cw_ref, cb_ref, halo_ref):
                halo = halo_ref[j]
                halo_ref[j] = u[tm - 8:, :]
                return _causal_conv(u, halo, cw_ref, cb_ref)

            gate = conv(ug, cwg_ref, cbg_ref, halo_g_ref)
            up = conv(uu, cwu_ref, cbu_ref, halo_u_ref)
            act_ref[j % 2] = (gate * (1.0 / (1.0 + jnp.exp(-gate))) * up).astype(BF16)

    pl.when(j == 0)(functools.partial(run, True, False))
    pl.when((j > 0) & (j < nj))(functools.partial(run, True, True))
    pl.when(j == nj)(functools.partial(run, False, True))


def _ffn(h, g, w_in, conv_w, conv_b, w_out, seq):
    t, d = h.shape
    dff = w_out.shape[0]
    tm, tf = FFN_TM, FFN_TF
    assert t % tm == 0 and seq % tm == 0 and dff % tf == 0
    nj = dff // tf

    def up_tile(j):
        return jnp.minimum(j, nj - 1)

    return pl.pallas_call(
        functools.partial(_ffn_kernel, tiles_per_seq=seq // tm, nj=nj),
        grid=(t // tm, nj + 1),
        in_specs=[pl.BlockSpec((tm, d), lambda i, j: (i, 0)),
                  pl.BlockSpec((1, d), lambda i, j: (0, 0)),
                  pl.BlockSpec((d, tf), lambda i, j: (0, up_tile(j))),
                  pl.BlockSpec((d, tf), lambda i, j: (0, nj + up_tile(j))),
                  pl.BlockSpec((CONV_WIDTH, tf), lambda i, j: (0, up_tile(j))),
                  pl.BlockSpec((CONV_WIDTH, tf), lambda i, j: (0, nj + up_tile(j))),
                  pl.BlockSpec((1, tf), lambda i, j: (0, up_tile(j))),
                  pl.BlockSpec((1, tf), lambda i, j: (0, nj + up_tile(j))),
                  pl.BlockSpec((tf, d), lambda i, j: (jnp.maximum(j - 1, 0), 0))],
        out_specs=pl.BlockSpec((tm, d), lambda i, j: (i, 0)),
        out_shape=jax.ShapeDtypeStruct((t, d), F32),
        scratch_shapes=[pltpu.VMEM((tm, d), BF16),
                        pltpu.VMEM((2, tm, tf), BF16),
                        pltpu.VMEM((nj, 8, tf), F32),
                        pltpu.VMEM((nj, 8, tf), F32)],
        compiler_params=_cparams("arbitrary", "arbitrary"),
        name="conv_ffn",
    )(h, g, w_in, w_in, conv_w, conv_w, conv_b, conv_b, w_out)


def _row(v):
    return v.reshape(1, -1).astype(F32)


def kernel(x, rel_table, attn_norm_g, ffn_norm_g, da_w_qkv, da_w_o, da_q_gain, da_k_gain, da_lam_q1, da_lam_k1, da_lam_q2, da_lam_k2, da_sub_gain, sw_w_qkv, sw_w_o, sw_q_gain, sw_k_gain, sw_sinks, ffn_w_in, ffn_conv_w, ffn_conv_b, ffn_w_out):
    batch, seq, d = x.shape
    depth = attn_norm_g.shape[0]
    t = batch * seq
    h = x.reshape(t, d)
    q_scale = HEAD_DIM ** -0.5

    da_bias = _da_bias(rel_table, DA_T)
    sw_bias = _sw_bias(rel_table, SW_TQ)

    da_qk = 2 * DA_HEADS * HEAD_DIM
    da_flag = jnp.concatenate([jnp.ones((1, 2 * da_qk), F32), jnp.zeros((1, DA_HEADS * DA_VDIM), F32)], axis=1)
    sw_qw = SW_Q_HEADS * HEAD_DIM
    sw_kw = SW_KV_HEADS * HEAD_DIM
    sw_flag = jnp.concatenate([jnp.ones((1, sw_qw + sw_kw), F32), jnp.zeros((1, sw_kw), F32)], axis=1)

    for i in range(depth):
        j = i // N_MIXERS
        if i % N_MIXERS == 0:
            gain = jnp.concatenate([jnp.tile(da_q_gain[j], da_qk // HEAD_DIM) * (q_scale * LOG2E),
                                    jnp.tile(da_k_gain[j], da_qk // HEAD_DIM),
                                    jnp.ones((DA_HEADS * DA_VDIM,), F32)])
            qkv = _norm_proj(h, _row(attn_norm_g[i]), da_w_qkv[j].astype(BF16), da_flag, _row(gain), PROJ_TN_DA)
            nk = seq // DA_T
            vt = qkv[:, 2 * da_qk:].reshape(batch, nk, DA_T, DA_HEADS, DA_VDIM).transpose(0, 3, 1, 4, 2)
            score_bound = (HEAD_DIM * q_scale * jnp.max(jnp.abs(da_q_gain[j])) * jnp.max(jnp.abs(da_k_gain[j]))
                           + jnp.max(jnp.abs(rel_table - rel_table[NUM_BUCKETS - 1]))) * LOG2E
            fast = (score_bound <= FAST_BOUND).astype(jnp.int32).reshape(1)
            lamv = jnp.stack([da_lam_q1[j], da_lam_k1[j], da_lam_q2[j], da_lam_k2[j]]).astype(F32)
            lam_init = 0.8 - 0.6 * math.exp(-0.3 * i)
            a = _da_attn(fast, qkv, vt, da_bias, lamv, da_sub_gain[j].reshape(DA_VDIM, 1).astype(F32), batch, seq,
                         lam_init)
            h = _out_proj(a, da_w_o[j].astype(BF16), h)
        else:
            gain = jnp.concatenate([jnp.tile(sw_q_gain[j], SW_Q_HEADS) * (q_scale * LOG2E),
                                    jnp.tile(sw_k_gain[j], SW_KV_HEADS),
                                    jnp.ones((sw_kw,), F32)])
            qkv = _norm_proj(h, _row(attn_norm_g[i]), sw_w_qkv[j].astype(BF16), sw_flag, _row(gain), PROJ_TN_SW)
            k = qkv[:, sw_qw:sw_qw + sw_kw].reshape(t, SW_KV_HEADS, 1, HEAD_DIM)
            kdup = jnp.broadcast_to(k, (t, SW_KV_HEADS, 2, HEAD_DIM)).reshape(t, SW_KV_HEADS * LANES)
            vt = qkv[:, sw_qw + sw_kw:].reshape(batch, seq, SW_KV_HEADS, HEAD_DIM).transpose(0, 2, 3, 1)
            ones_rows = jnp.zeros((SW_VROWS - HEAD_DIM, seq), BF16).at[0].set(1.0)
            vt = jnp.concatenate([vt, jnp.broadcast_to(ones_rows, vt.shape[:2] + ones_rows.shape)], axis=2)
            score_bound = (HEAD_DIM * q_scale * jnp.max(jnp.abs(sw_q_gain[j])) * jnp.max(jnp.abs(sw_k_gain[j]))
                           + jnp.max(jnp.abs(rel_table)) + jnp.max(jnp.abs(sw_sinks[j]))) * LOG2E
            fast = (score_bound <= FAST_BOUND).astype(jnp.int32).reshape(1)
            a = _sw_attn(fast, qkv, kdup, vt, sw_bias, sw_sinks[j].astype(F32), batch, seq)
            h = _out_proj(a, sw_w_o[j].astype(BF16), h)
        h = _ffn(h, _row(ffn_norm_g[i]), ffn_w_in[i].astype(BF16), ffn_conv_w[i].astype(F32),
                 _row(ffn_conv_b[i]), ffn_w_out[i].astype(BF16), seq)
    return h.reshape(batch, seq, d)
```

```python
import functools
import math

import numpy as np
import jax
import jax.numpy as jnp
from jax import lax
from jax.experimental import pallas as pl
from jax.experimental.pallas import tpu as pltpu

F32 = jnp.float32
BF16 = jnp.bfloat16

HEAD_DIM = 64
N_MAPS = 32
DA_HEADS = 16
DA_VDIM = 128
SW_Q_HEADS = 32
SW_KV_HEADS = 4
SW_GROUP = SW_Q_HEADS // SW_KV_HEADS
SW_WINDOW = 128
NUM_BUCKETS = 32
MAX_DISTANCE = 128
CONV_WIDTH = 3
EPS = 1e-6
N_MIXERS = 2

LANES = 128
NEG = -1e30
LOG2E = math.log2(math.e)
VMEM_LIMIT = 56 * 1024 * 1024

PROJ_TM = 1024
PROJ_TN_DA = 1024
PROJ_TN_SW = 1280
PROJ_ROWS = 256
OUT_TM = 1024
OUT_TN = 1024
FFN_TM = 512
FFN_TF = 512
DA_T = 512
FAST_BOUND = 60.0
SW_TQ = 256
SW_VROWS = HEAD_DIM + 16


def _cparams(*sem):
    return pltpu.CompilerParams(dimension_semantics=sem, vmem_limit_bytes=VMEM_LIMIT)


def _t5_bucket_np(dist):
    max_exact = NUM_BUCKETS // 2
    d = np.maximum(dist, 0)
    df = np.maximum(d, 1).astype(np.float64)
    large = max_exact + (np.log(df / max_exact) / math.log(MAX_DISTANCE / max_exact)
                         * (NUM_BUCKETS - max_exact)).astype(np.int32)
    large = np.minimum(large, NUM_BUCKETS - 1)
    return np.where(d < max_exact, d, large).astype(np.int32)


def _da_bias_kernel(tab_ref, bucket_ref, out_ref, *, t):
    h = pl.program_id(0)
    s = MAX_DISTANCE
    for off in range(2):
        for m in range(2):
            col = 2 * h + m
            far = tab_ref[NUM_BUCKETS - 1, col]
            for kr in range(t // s):
                for qc in range(t // s):
                    base = off * t + (qc - kr) * s
                    dst = (0, off, slice(kr * s, (kr + 1) * s), slice(m * t + qc * s, m * t + (qc + 1) * s))
                    if base - (s - 1) >= MAX_DISTANCE:
                        out_ref[dst] = jnp.zeros((s, s), F32)
                    elif base + (s - 1) < 0:
                        out_ref[dst] = jnp.full((s, s), NEG, F32)
                    else:
                        bk = bucket_ref[off, kr * s:(kr + 1) * s, qc * s:(qc + 1) * s]
                        acc = jnp.zeros(bk.shape, F32)
                        for b in range(NUM_BUCKETS - 1):
                            acc = jnp.where(bk == b, (tab_ref[b, col] - far) * LOG2E, acc)
                        out_ref[dst] = jnp.where(bk < 0, NEG, acc)


def _da_bias(rel_table, t):
    key = np.arange(t)[:, None]
    qry = np.arange(t)[None, :]
    d0 = qry - key
    d1 = t + qry - key
    bucket = np.stack([np.where(d0 >= 0, _t5_bucket_np(d0), -1), _t5_bucket_np(d1)]).astype(np.int32)
    return pl.pallas_call(
        functools.partial(_da_bias_kernel, t=t),
        grid=(DA_HEADS,),
        in_specs=[pl.BlockSpec(memory_space=pltpu.SMEM),
                  pl.BlockSpec((2, t, t), lambda h: (0, 0, 0))],
        out_specs=pl.BlockSpec((1, 2, t, 2 * t), lambda h: (h, 0, 0, 0)),
        out_shape=jax.ShapeDtypeStruct((DA_HEADS, 2, t, 2 * t), F32),
        compiler_params=_cparams("arbitrary"),
        name="da_bias",
    )(rel_table, jnp.asarray(bucket))


def _sw_bias_kernel(tab_ref, bucket_ref, out_ref, *, tq):
    p = pl.program_id(1)
    bk = bucket_ref[0]
    for m in range(2):
        col = 2 * p + m
        acc = jnp.zeros(bk.shape, F32)
        for b in range(NUM_BUCKETS):
            acc = jnp.where(bk == b, tab_ref[b, col] * LOG2E, acc)
        out_ref[0, 0, :, m * tq:(m + 1) * tq] = jnp.where(bk < 0, NEG, acc)


def _sw_bias(rel_table, tq):
    band = SW_WINDOW + tq
    key = np.arange(band)[:, None]
    qry = np.arange(tq)[None, :]
    dist = qry + SW_WINDOW - key
    bucket = np.where((dist >= 0) & (dist < SW_WINDOW), _t5_bucket_np(dist), -1).astype(np.int32)
    first = np.where(key < SW_WINDOW, -1, bucket).astype(np.int32)
    return pl.pallas_call(
        functools.partial(_sw_bias_kernel, tq=tq),
        grid=(2, SW_Q_HEADS // 2),
        in_specs=[pl.BlockSpec(memory_space=pltpu.SMEM),
                  pl.BlockSpec((1, band, tq), lambda v, p: (v, 0, 0))],
        out_specs=pl.BlockSpec((1, 1, band, 2 * tq), lambda v, p: (v, p, 0, 0)),
        out_shape=jax.ShapeDtypeStruct((2, SW_Q_HEADS // 2, band, 2 * tq), F32),
        compiler_params=_cparams("arbitrary", "arbitrary"),
        name="sw_bias",
    )(rel_table, jnp.asarray(np.stack([bucket, first])))


def _rms_rows(x, g):
    ms = jnp.mean(x * x, axis=-1, keepdims=True)
    return x * lax.rsqrt(ms + EPS) * g


def _norm_proj_kernel(x_ref, g_ref, w_ref, flag_ref, gain_ref, o_ref, xn_ref, y0_ref, y1_ref):
    i = pl.program_id(0)
    j = pl.program_id(1)
    tm, tn = y0_ref.shape

    @pl.when((i == 0) & (j == 0))
    def _():
        y1_ref[...] = jnp.zeros(y1_ref.shape, F32)

    @pl.when(j == 0)
    def _():
        xn_ref[...] = _rms_rows(x_ref[...], g_ref[...]).astype(BF16)

    def run(parity):
        y_cur, y_prev = (y0_ref, y1_ref)[::1 - 2 * parity]
        lane = lax.broadcasted_iota(jnp.int32, (1, LANES), 1)
        low = lane < HEAD_DIM
        for r in range(tm // PROJ_ROWS):
            rows = slice(r * PROJ_ROWS, (r + 1) * PROJ_ROWS)
            for c in range(tn // LANES):
                sl = slice(c * LANES, (c + 1) * LANES)
                yc = y_prev[rows, sl]
                y2 = yc * yc
                ms_lo = jnp.sum(jnp.where(low, y2, 0.0), axis=-1, keepdims=True) * (1.0 / HEAD_DIM)
                ms_hi = jnp.sum(jnp.where(low, 0.0, y2), axis=-1, keepdims=True) * (1.0 / HEAD_DIM)
                inv = jnp.where(low, lax.rsqrt(ms_lo + EPS), lax.rsqrt(ms_hi + EPS))
                scale = jnp.where(flag_ref[:, sl] > 0.0, inv, 1.0) * gain_ref[:, sl]
                o_ref[rows, sl] = (yc * scale).astype(BF16)
            y_cur[rows, :] = jnp.dot(xn_ref[rows, :], w_ref[...], preferred_element_type=F32)

    even = (j % 2) == 0
    pl.when(even)(functools.partial(run, 0))
    pl.when(jnp.logical_not(even))(functools.partial(run, 1))


def _norm_proj(x, g, w, flag, gain, tn):
    t, d = x.shape
    n = w.shape[1]
    tm = PROJ_TM
    assert t % tm == 0 and n % tn == 0 and tm % PROJ_ROWS == 0
    nj = n // tn

    def tile(j):
        return jnp.clip(j, 0, nj - 1)

    return pl.pallas_call(
        _norm_proj_kernel,
        grid=(t // tm, nj + 1),
        in_specs=[pl.BlockSpec((tm, d), lambda i, j: (i, 0)),
                  pl.BlockSpec((1, d), lambda i, j: (0, 0)),
                  pl.BlockSpec((d, tn), lambda i, j: (0, tile(j))),
                  pl.BlockSpec((1, tn), lambda i, j: (0, tile(j - 1))),
                  pl.BlockSpec((1, tn), lambda i, j: (0, tile(j - 1)))],
        out_specs=pl.BlockSpec((tm, tn), lambda i, j: (i, tile(j - 1))),
        out_shape=jax.ShapeDtypeStruct((t, n), BF16),
        scratch_shapes=[pltpu.VMEM((tm, d), BF16), pltpu.VMEM((tm, tn), F32), pltpu.VMEM((tm, tn), F32)],
        compiler_params=_cparams("arbitrary", "arbitrary"),
        name="norm_proj",
    )(x, g, w, flag, gain)


def _out_proj_kernel(a_ref, w_ref, h_ref, o_ref):
    o_ref[...] = h_ref[...] + jnp.dot(a_ref[...], w_ref[...], preferred_element_type=F32)


def _out_proj(a, w, h):
    t, k = a.shape
    n = w.shape[1]
    tm, tn = OUT_TM, OUT_TN
    assert t % tm == 0 and n % tn == 0
    return pl.pallas_call(
        _out_proj_kernel,
        grid=(t // tm, n // tn),
        in_specs=[pl.BlockSpec((tm, k), lambda i, j: (i, 0)),
                  pl.BlockSpec((k, tn), lambda i, j: (0, j)),
                  pl.BlockSpec((tm, tn), lambda i, j: (i, j))],
        out_specs=pl.BlockSpec((tm, tn), lambda i, j: (i, j)),
        out_shape=jax.ShapeDtypeStruct((t, n), F32),
        compiler_params=_cparams("arbitrary", "arbitrary"),
        name="out_proj",
    )(a, w, h)


def _stack_maps(q):
    lane = lax.broadcasted_iota(jnp.int32, q.shape, 1)
    qf = q.astype(F32)
    return jnp.concatenate([jnp.where(lane < HEAD_DIM, qf, 0.0), jnp.where(lane < HEAD_DIM, 0.0, qf)],
                           axis=0).astype(BF16)


def _da_attn_kernel(fast_ref, q_ref, k_ref, vt_ref, bias_ref, lamv_ref, sg_ref, o_ref, acc_ref, l_ref, m_ref, *, t,
                    lam_init):
    i = pl.program_id(2)
    qt = q_ref[...].astype(F32).T
    row = lax.broadcasted_iota(jnp.int32, qt.shape, 0)
    qqt = jnp.concatenate([jnp.where(row < HEAD_DIM, qt, 0.0), jnp.where(row < HEAD_DIM, 0.0, qt)],
                          axis=1).astype(BF16)
    acc_ref[...] = jnp.zeros(acc_ref.shape, F32)
    l_ref[...] = jnp.zeros(l_ref.shape, F32)

    def scores(u, bias):
        k = k_ref[pl.ds(pl.multiple_of(u * t, t), t), :]
        st = jnp.dot(k, qqt, preferred_element_type=F32)
        return st if bias is None else st + bias

    def chunk(u0, biases):
        pv = None
        ls = None
        for n, bias in enumerate(biases):
            p = jnp.exp2(scores(u0 + n, bias))
            s = jnp.sum(p, axis=0, keepdims=True)
            d = jnp.dot(vt_ref[u0 + n], p.astype(BF16), preferred_element_type=F32)
            pv, ls = (d, s) if pv is None else (pv + d, ls + s)
        acc_ref[...] += pv
        l_ref[...] += ls

    def unit_running_max(u, bias):
        st = scores(u, bias)
        m_prev = m_ref[...]
        m_new = jnp.maximum(m_prev, jnp.max(st, axis=0, keepdims=True))
        alpha = jnp.exp2(m_prev - m_new)
        p = jnp.exp2(st - m_new)
        l_ref[...] = alpha * l_ref[...] + jnp.sum(p, axis=0, keepdims=True)
        acc_ref[...] = alpha * acc_ref[...] + jnp.dot(vt_ref[u], p.astype(BF16), preferred_element_type=F32)
        m_ref[...] = m_new

    fast = fast_ref[0] == 1
    odd = (i % 2) == 1

    far_pairs = jnp.maximum(i - 1, 0) // 2

    @pl.when(fast)
    def _():
        def far_quad(c, carry):
            chunk(4 * c, (None, None, None, None))
            return carry

        lax.fori_loop(0, far_pairs // 2, far_quad, 0)

    @pl.when(fast & ((far_pairs % 2) == 1))
    def _():
        chunk(2 * (far_pairs - 1), (None, None))

    @pl.when(fast & (i == 0))
    def _():
        chunk(0, (bias_ref[0, 0],))

    @pl.when(fast & odd)
    def _():
        chunk(i - 1, (bias_ref[0, 1], bias_ref[0, 0]))

    @pl.when(fast & jnp.logical_not(odd) & (i > 0))
    def _():
        chunk(i - 2, (None, bias_ref[0, 1], bias_ref[0, 0]))

    @pl.when(jnp.logical_not(fast))
    def _():
        m_ref[...] = jnp.full(m_ref.shape, NEG, F32)

        def far_unit(u, carry):
            unit_running_max(u, None)
            return carry

        lax.fori_loop(0, jnp.maximum(i - 1, 0), far_unit, 0)

        @pl.when(i >= 1)
        def _():
            unit_running_max(i - 1, bias_ref[0, 1])

        unit_running_max(i, bias_ref[0, 0])

    lv = lamv_ref[...]
    lam = (jnp.exp(jnp.sum(lv[0:1] * lv[1:2], axis=-1, keepdims=True))
           - jnp.exp(jnp.sum(lv[2:3] * lv[3:4], axis=-1, keepdims=True)) + lam_init)
    o = acc_ref[...] * (1.0 / l_ref[...])
    ot = o[:, :t] - lam * o[:, t:]
    ms = jnp.mean(ot * ot, axis=0, keepdims=True)
    ot = ot * lax.rsqrt(ms + EPS) * (sg_ref[...] * (1.0 - lam_init))
    o_ref[...] = ot.T.astype(BF16)


def _da_attn(fast, qkv, vt, bias, lamv, sub_gain, batch, seq, lam_init):
    t = DA_T
    nq = seq // t
    return pl.pallas_call(
        functools.partial(_da_attn_kernel, t=t, lam_init=lam_init),
        grid=(batch, DA_HEADS, nq),
        in_specs=[pl.BlockSpec(memory_space=pltpu.SMEM),
                  pl.BlockSpec((t, LANES), lambda b, h, i: (b * nq + i, h)),
                  pl.BlockSpec((seq, LANES), lambda b, h, i: (b, DA_HEADS + h)),
                  pl.BlockSpec((None, None, nq, DA_VDIM, t), lambda b, h, i: (b, h, 0, 0, 0)),
                  pl.BlockSpec((1, 2, t, 2 * t), lambda b, h, i: (h, 0, 0, 0)),
                  pl.BlockSpec((4, HEAD_DIM), lambda b, h, i: (0, 0)),
                  pl.BlockSpec((DA_VDIM, 1), lambda b, h, i: (0, 0))],
        out_specs=pl.BlockSpec((t, LANES), lambda b, h, i: (b * nq + i, h)),
        out_shape=jax.ShapeDtypeStruct((batch * seq, DA_HEADS * DA_VDIM), BF16),
        scratch_shapes=[pltpu.VMEM((DA_VDIM, 2 * t), F32), pltpu.VMEM((1, 2 * t), F32), pltpu.VMEM((1, 2 * t), F32)],
        compiler_params=_cparams("arbitrary", "arbitrary", "arbitrary"),
        name="da_attn",
    )(fast, qkv, qkv, vt, bias, lamv, sub_gain)


def _sw_attn_kernel(fast_ref, sink_ref, q_ref, kp_ref, kc_ref, vp_ref, vc_ref, bias_ref, o_ref, *, tq):
    g = pl.program_id(0)
    kb = jnp.concatenate([kp_ref[...], kc_ref[...]], axis=0)
    vtb = jnp.concatenate([vp_ref[...], vc_ref[...]], axis=1)
    col = lax.broadcasted_iota(jnp.int32, (1, 2 * tq), 1)

    def lane_slice(pp):
        return slice(pp * LANES, (pp + 1) * LANES)

    def scores(pp):
        qt = q_ref[:, lane_slice(pp)].astype(F32).T
        row = lax.broadcasted_iota(jnp.int32, qt.shape, 0)
        qqt = jnp.concatenate([jnp.where(row < HEAD_DIM, qt, 0.0), jnp.where(row < HEAD_DIM, 0.0, qt)],
                              axis=1).astype(BF16)
        return jnp.dot(kb, qqt, preferred_element_type=F32) + bias_ref[0, pp]

    def weighted_values(pp, st, fast):
        head = g * SW_GROUP + 2 * pp
        sink = jnp.where(col < tq, sink_ref[head], sink_ref[head + 1]) * LOG2E
        if fast:
            ot = jnp.dot(vtb, jnp.exp2(st).astype(BF16), preferred_element_type=F32)
            return ot, jnp.exp2(sink)
        m = jnp.maximum(jnp.max(st, axis=0, keepdims=True), sink)
        ot = jnp.dot(vtb, jnp.exp2(st - m).astype(BF16), preferred_element_type=F32)
        return ot, jnp.exp2(sink - m)

    def finish(pp, ot, sink_term):
        ot = ot[:HEAD_DIM] * (1.0 / (ot[HEAD_DIM:HEAD_DIM + 1] + sink_term))
        o_ref[:, lane_slice(pp)] = jnp.concatenate([ot[:, :tq], ot[:, tq:]], axis=0).T.astype(BF16)

    def all_pairs(fast):
        pairs = range(SW_GROUP // 2)
        sts = [scores(pp) for pp in pairs]
        ots = [weighted_values(pp, sts[pp], fast) for pp in pairs]
        for pp in pairs:
            finish(pp, *ots[pp])

    pl.when(fast_ref[0] == 1)(functools.partial(all_pairs, True))
    pl.when(fast_ref[0] != 1)(functools.partial(all_pairs, False))


def _sw_attn(fast, q, kdup, vt, bias, sinks, batch, seq):
    tq = SW_TQ
    nq = seq // tq
    r = tq // SW_WINDOW
    band = SW_WINDOW + tq
    gw = SW_GROUP * HEAD_DIM
    return pl.pallas_call(
        functools.partial(_sw_attn_kernel, tq=tq),
        grid=(SW_KV_HEADS, batch, nq),
        in_specs=[pl.BlockSpec(memory_space=pltpu.SMEM),
                  pl.BlockSpec(memory_space=pltpu.SMEM),
                  pl.BlockSpec((tq, gw), lambda g, b, i: (b * nq + i, g)),
                  pl.BlockSpec((SW_WINDOW, LANES), lambda g, b, i: (jnp.maximum((b * nq + i) * r - 1, 0), g)),
                  pl.BlockSpec((tq, LANES), lambda g, b, i: (b * nq + i, g)),
                  pl.BlockSpec((None, None, SW_VROWS, SW_WINDOW), lambda g, b, i: (b, g, 0, jnp.maximum(i * r - 1, 0))),
                  pl.BlockSpec((None, None, SW_VROWS, tq), lambda g, b, i: (b, g, 0, i)),
                  pl.BlockSpec((1, SW_GROUP // 2, band, 2 * tq), lambda g, b, i: (jnp.where(i == 0, 1, 0), g, 0, 0))],
        out_specs=pl.BlockSpec((tq, gw), lambda g, b, i: (b * nq + i, g)),
        out_shape=jax.ShapeDtypeStruct((batch * seq, SW_Q_HEADS * HEAD_DIM), BF16),
        compiler_params=_cparams("arbitrary", "arbitrary", "arbitrary"),
        name="sw_attn",
    )(fast, sinks, q, kdup, kdup, vt, vt, bias)


def _causal_conv(u, halo, cw_ref, cb_ref):
    row = lax.broadcasted_iota(jnp.int32, (8, 1), 0)
    r1 = pltpu.roll(u, 1, axis=0)
    r2 = pltpu.roll(u, 2, axis=0)
    u1 = jnp.concatenate([jnp.where(row == 0, halo[7:8], r1[:8]), r1[8:]], axis=0)
    u2 = jnp.concatenate([jnp.where(row == 0, halo[6:7], jnp.where(row == 1, halo[7:8], r2[:8])), r2[8:]], axis=0)
    return cw_ref[0:1, :] * u2 + cw_ref[1:2, :] * u1 + cw_ref[2:3, :] * u + cb_ref[...]


def _ffn_kernel(h_ref, g_ref, wg_ref, wu_ref, cwg_ref, cwu_ref, cbg_ref, cbu_ref, wo_ref, o_ref,
                f_ref, act_ref, halo_g_ref, halo_u_ref, *, tiles_per_seq, nj):
    i = pl.program_id(0)
    j = pl.program_id(1)
    tm = h_ref.shape[0]

    @pl.when(j == 0)
    def _():
        x = h_ref[...]
        f_ref[...] = _rms_rows(x, g_ref[...]).astype(BF16)
        o_ref[...] = x

    @pl.when(((i % tiles_per_seq) == 0) & (j < nj))
    def _():
        halo_g_ref[j] = jnp.zeros(halo_g_ref.shape[1:], F32)
        halo_u_ref[j] = jnp.zeros(halo_u_ref.shape[1:], F32)

    def run(do_up, do_down):
        if do_up:
            f = f_ref[...]
            ug = jnp.dot(f, wg_ref[...], preferred_element_type=F32)
            uu = jnp.dot(f, wu_ref[...], preferred_element_type=F32)
        if do_down:
            o_ref[...] += jnp.dot(act_ref[(j + 1) % 2], wo_ref[...], preferred_element_type=F32)
        if do_up:
            def conv(u, cw_ref, cb_ref, halo_ref):
                halo = halo_ref[j]
                halo_ref[j] = u[tm - 8:, :]
                return _causal_conv(u, halo, cw_ref, cb_ref)

            gate = conv(ug, cwg_ref, cbg_ref, halo_g_ref)
            up = conv(uu, cwu_ref, cbu_ref, halo_u_ref)
            act_ref[j % 2] = (gate * (1.0 / (1.0 + jnp.exp(-gate))) * up).astype(BF16)

    pl.when(j == 0)(functools.partial(run, True, False))
    pl.when((j > 0) & (j < nj))(functools.partial(run, True, True))
    pl.when(j == nj)(functools.partial(run, False, True))


def _ffn(h, g, w_in, conv_w, conv_b, w_out, seq):
    t, d = h.shape
    dff = w_out.shape[0]
    tm, tf = FFN_TM, FFN_TF
    assert t % tm == 0 and seq % tm == 0 and dff % tf == 0
    nj = dff // tf

    def up_tile(j):
        return jnp.minimum(j, nj - 1)

    return pl.pallas_call(
        functools.partial(_ffn_kernel, tiles_per_seq=seq // tm, nj=nj),
        grid=(t // tm, nj + 1),
        in_specs=[pl.BlockSpec((tm, d), lambda i, j: (i, 0)),
                  pl.BlockSpec((1, d), lambda i, j: (0, 0)),
                  pl.BlockSpec((d, tf), lambda i, j: (0, up_tile(j))),
                  pl.BlockSpec((d, tf), lambda i, j: (0, nj + up_tile(j))),
                  pl.BlockSpec((CONV_WIDTH, tf), lambda i, j: (0, up_tile(j))),
                  pl.BlockSpec((CONV_WIDTH, tf), lambda i, j: (0, nj + up_tile(j))),
                  pl.BlockSpec((1, tf), lambda i, j: (0, up_tile(j))),
                  pl.BlockSpec((1, tf), lambda i, j: (0, nj + up_tile(j))),
                  pl.BlockSpec((tf, d), lambda i, j: (jnp.maximum(j - 1, 0), 0))],
        out_specs=pl.BlockSpec((tm, d), lambda i, j: (i, 0)),
        out_shape=jax.ShapeDtypeStruct((t, d), F32),
        scratch_shapes=[pltpu.VMEM((tm, d), BF16),
                        pltpu.VMEM((2, tm, tf), BF16),
                        pltpu.VMEM((nj, 8, tf), F32),
                        pltpu.VMEM((nj, 8, tf), F32)],
        compiler_params=_cparams("arbitrary", "arbitrary"),
        name="conv_ffn",
    )(h, g, w_in, w_in, conv_w, conv_w, conv_b, conv_b, w_out)


def _row(v):
    return v.reshape(1, -1).astype(F32)


def kernel(x, rel_table, attn_norm_g, ffn_norm_g, da_w_qkv, da_w_o, da_q_gain, da_k_gain, da_lam_q1, da_lam_k1, da_lam_q2, da_lam_k2, da_sub_gain, sw_w_qkv, sw_w_o, sw_q_gain, sw_k_gain, sw_sinks, ffn_w_in, ffn_conv_w, ffn_conv_b, ffn_w_out):
    batch, seq, d = x.shape
    depth = attn_norm_g.shape[0]
    t = batch * seq
    h = x.reshape(t, d)
    q_scale = HEAD_DIM ** -0.5

    da_bias = _da_bias(rel_table, DA_T)
    sw_bias = _sw_bias(rel_table, SW_TQ)

    da_qk = 2 * DA_HEADS * HEAD_DIM
    da_flag = jnp.concatenate([jnp.ones((1, 2 * da_qk), F32), jnp.zeros((1, DA_HEADS * DA_VDIM), F32)], axis=1)
    sw_qw = SW_Q_HEADS * HEAD_DIM
    sw_kw = SW_KV_HEADS * HEAD_DIM
    sw_flag = jnp.concatenate([jnp.ones((1, sw_qw + sw_kw), F32), jnp.zeros((1, sw_kw), F32)], axis=1)

    for i in range(depth):
        j = i // N_MIXERS
        if i % N_MIXERS == 0:
            gain = jnp.concatenate([jnp.tile(da_q_gain[j], da_qk // HEAD_DIM) * (q_scale * LOG2E),
                                    jnp.tile(da_k_gain[j], da_qk // HEAD_DIM),
                                    jnp.ones((DA_HEADS * DA_VDIM,), F32)])
            qkv = _norm_proj(h, _row(attn_norm_g[i]), da_w_qkv[j].astype(BF16), da_flag, _row(gain), PROJ_TN_DA)
            nk = seq // DA_T
            vt = qkv[:, 2 * da_qk:].reshape(batch, nk, DA_T, DA_HEADS, DA_VDIM).transpose(0, 3, 1, 4, 2)
            score_bound = (HEAD_DIM * q_scale * jnp.max(jnp.abs(da_q_gain[j])) * jnp.max(jnp.abs(da_k_gain[j]))
                           + jnp.max(jnp.abs(rel_table - rel_table[NUM_BUCKETS - 1]))) * LOG2E
            fast = (score_bound <= FAST_BOUND).astype(jnp.int32).reshape(1)
            lamv = jnp.stack([da_lam_q1[j], da_lam_k1[j], da_lam_q2[j], da_lam_k2[j]]).astype(F32)
            lam_init = 0.8 - 0.6 * math.exp(-0.3 * i)
            a = _da_attn(fast, qkv, vt, da_bias, lamv, da_sub_gain[j].reshape(DA_VDIM, 1).astype(F32), batch, seq,
                         lam_init)
            h = _out_proj(a, da_w_o[j].astype(BF16), h)
        else:
            gain = jnp.concatenate([jnp.tile(sw_q_gain[j], SW_Q_HEADS) * (q_scale * LOG2E),
                                    jnp.tile(sw_k_gain[j], SW_KV_HEADS),
                                    jnp.ones((sw_kw,), F32)])
            qkv = _norm_proj(h, _row(attn_norm_g[i]), sw_w_qkv[j].astype(BF16), sw_flag, _row(gain), PROJ_TN_SW)
            k = qkv[:, sw_qw:sw_qw + sw_kw].reshape(t, SW_KV_HEADS, 1, HEAD_DIM)
            kdup = jnp.broadcast_to(k, (t, SW_KV_HEADS, 2, HEAD_DIM)).reshape(t, SW_KV_HEADS * LANES)
            vt = qkv[:, sw_qw + sw_kw:].reshape(batch, seq, SW_KV_HEADS, HEAD_DIM).transpose(0, 2, 3, 1)
            ones_rows = jnp.zeros((SW_VROWS - HEAD_DIM, seq), BF16).at[0].set(1.0)
            vt = jnp.concatenate([vt, jnp.broadcast_to(ones_rows, vt.shape[:2] + ones_rows.shape)], axis=2)
            score_bound = (HEAD_DIM * q_scale * jnp.max(jnp.abs(sw_q_gain[j])) * jnp.max(jnp.abs(sw_k_gain[j]))
                           + jnp.max(jnp.abs(rel_table)) + jnp.max(jnp.abs(sw_sinks[j]))) * LOG2E
            fast = (score_bound <= FAST_BOUND).astype(jnp.int32).reshape(1)
            a = _sw_attn(fast, qkv, kdup, vt, sw_bias, sw_sinks[j].astype(F32), batch, seq)
            h = _out_proj(a, sw_w_o[j].astype(BF16), h)
        h = _ffn(h, _row(ffn_norm_g[i]), ffn_w_in[i].astype(BF16), ffn_conv_w[i].astype(F32),
                 _row(ffn_conv_b[i]), ffn_w_out[i].astype(BF16), seq)
    return h.reshape(batch, seq, d)
```

```python
import functools
import math

import numpy as np
import jax
import jax.numpy as jnp
from jax import lax
from jax.experimental import pallas as pl
from jax.experimental.pallas import tpu as pltpu

F32 = jnp.float32
BF16 = jnp.bfloat16

HEAD_DIM = 64
N_MAPS = 32
DA_HEADS = 16
DA_VDIM = 128
SW_Q_HEADS = 32
SW_KV_HEADS = 4
SW_GROUP = SW_Q_HEADS // SW_KV_HEADS
SW_WINDOW = 128
NUM_BUCKETS = 32
MAX_DISTANCE = 128
CONV_WIDTH = 3
EPS = 1e-6
N_MIXERS = 2

LANES = 128
NEG = -1e30
LOG2E = math.log2(math.e)
VMEM_LIMIT = 56 * 1024 * 1024

PROJ_TM = 1024
PROJ_TN_DA = 1024
PROJ_TN_SW = 1280
PROJ_ROWS = 256
OUT_TM = 1024
OUT_TN = 1024
FFN_TM = 512
FFN_TF = 512
DA_T = 512
FAST_BOUND = 60.0
SW_TQ = 256
SW_VROWS = HEAD_DIM + 16


def _cparams(*sem):
    return pltpu.CompilerParams(dimension_semantics=sem, vmem_limit_bytes=VMEM_LIMIT)


def _t5_bucket_np(dist):
    max_exact = NUM_BUCKETS // 2
    d = np.maximum(dist, 0)
    df = np.maximum(d, 1).astype(np.float64)
    large = max_exact + (np.log(df / max_exact) / math.log(MAX_DISTANCE / max_exact)
                         * (NUM_BUCKETS - max_exact)).astype(np.int32)
    large = np.minimum(large, NUM_BUCKETS - 1)
    return np.where(d < max_exact, d, large).astype(np.int32)


def _da_bias_kernel(tab_ref, bucket_ref, out_ref, *, t):
    h = pl.program_id(0)
    s = MAX_DISTANCE
    for off in range(2):
        for m in range(2):
            col = 2 * h + m
            far = tab_ref[NUM_BUCKETS - 1, col]
            for kr in range(t // s):
                for qc in range(t // s):
                    base = off * t + (qc - kr) * s
                    dst = (0, off, slice(kr * s, (kr + 1) * s), slice(m * t + qc * s, m * t + (qc + 1) * s))
                    if base - (s - 1) >= MAX_DISTANCE:
                        out_ref[dst] = jnp.zeros((s, s), F32)
                    elif base + (s - 1) < 0:
                        out_ref[dst] = jnp.full((s, s), NEG, F32)
                    else:
                        bk = bucket_ref[off, kr * s:(kr + 1) * s, qc * s:(qc + 1) * s]
                        acc = jnp.zeros(bk.shape, F32)
                        for b in range(NUM_BUCKETS - 1):
                            acc = jnp.where(bk == b, (tab_ref[b, col] - far) * LOG2E, acc)
                        out_ref[dst] = jnp.where(bk < 0, NEG, acc)


def _da_bias(rel_table, t):
    key = np.arange(t)[:, None]
    qry = np.arange(t)[None, :]
    d0 = qry - key
    d1 = t + qry - key
    bucket = np.stack([np.where(d0 >= 0, _t5_bucket_np(d0), -1), _t5_bucket_np(d1)]).astype(np.int32)
    return pl.pallas_call(
        functools.partial(_da_bias_kernel, t=t),
        grid=(DA_HEADS,),
        in_specs=[pl.BlockSpec(memory_space=pltpu.SMEM),
                  pl.BlockSpec((2, t, t), lambda h: (0, 0, 0))],
        out_specs=pl.BlockSpec((1, 2, t, 2 * t), lambda h: (h, 0, 0, 0)),
        out_shape=jax.ShapeDtypeStruct((DA_HEADS, 2, t, 2 * t), F32),
        compiler_params=_cparams("arbitrary"),
        name="da_bias",
    )(rel_table, jnp.asarray(bucket))


def _sw_bias_kernel(tab_ref, bucket_ref, out_ref, *, tq):
    p = pl.program_id(1)
    bk = bucket_ref[0]
    for m in range(2):
        col = 2 * p + m
        acc = jnp.zeros(bk.shape, F32)
        for b in range(NUM_BUCKETS):
            acc = jnp.where(bk == b, tab_ref[b, col] * LOG2E, acc)
        out_ref[0, 0, :, m * tq:(m + 1) * tq] = jnp.where(bk < 0, NEG, acc)


def _sw_bias(rel_table, tq):
    band = SW_WINDOW + tq
    key = np.arange(band)[:, None]
    qry = np.arange(tq)[None, :]
    dist = qry + SW_WINDOW - key
    bucket = np.where((dist >= 0) & (dist < SW_WINDOW), _t5_bucket_np(dist), -1).astype(np.int32)
    first = np.where(key < SW_WINDOW, -1, bucket).astype(np.int32)
    return pl.pallas_call(
        functools.partial(_sw_bias_kernel, tq=tq),
        grid=(2, SW_Q_HEADS // 2),
        in_specs=[pl.BlockSpec(memory_space=pltpu.SMEM),
                  pl.BlockSpec((1, band, tq), lambda v, p: (v, 0, 0))],
        out_specs=pl.BlockSpec((1, 1, band, 2 * tq), lambda v, p: (v, p, 0, 0)),
        out_shape=jax.ShapeDtypeStruct((2, SW_Q_HEADS // 2, band, 2 * tq), F32),
        compiler_params=_cparams("arbitrary", "arbitrary"),
        name="sw_bias",
    )(rel_table, jnp.asarray(np.stack([bucket, first])))


def _rms_rows(x, g):
    ms = jnp.mean(x * x, axis=-1, keepdims=True)
    return x * lax.rsqrt(ms + EPS) * g


def _norm_proj_kernel(x_ref, g_ref, w_ref, flag_ref, gain_ref, o_ref, *rest, first_v_tile, v_unit):
    if first_v_tile is None:
        xn_ref, y0_ref, y1_ref = rest
    else:
        vt_ref, xn_ref, y0_ref, y1_ref = rest
    i = pl.program_id(0)
    j = pl.program_id(1)
    tm, tn = y0_ref.shape

    @pl.when((i == 0) & (j == 0))
    def _():
        y1_ref[...] = jnp.zeros(y1_ref.shape, F32)

    @pl.when(j == 0)
    def _():
        xn_ref[...] = _rms_rows(x_ref[...], g_ref[...]).astype(BF16)

    def run(parity):
        y_cur, y_prev = (y0_ref, y1_ref)[::1 - 2 * parity]
        lane = lax.broadcasted_iota(jnp.int32, (1, LANES), 1)
        low = lane < HEAD_DIM
        for r in range(tm // PROJ_ROWS):
            rows = slice(r * PROJ_ROWS, (r + 1) * PROJ_ROWS)
            for c in range(tn // LANES):
                sl = slice(c * LANES, (c + 1) * LANES)
                yc = y_prev[rows, sl]
                y2 = yc * yc
                ms_lo = jnp.sum(jnp.where(low, y2, 0.0), axis=-1, keepdims=True) * (1.0 / HEAD_DIM)
                ms_hi = jnp.sum(jnp.where(low, 0.0, y2), axis=-1, keepdims=True) * (1.0 / HEAD_DIM)
                inv = jnp.where(low, lax.rsqrt(ms_lo + EPS), lax.rsqrt(ms_hi + EPS))
                scale = jnp.where(flag_ref[:, sl] > 0.0, inv, 1.0) * gain_ref[:, sl]
                o_ref[rows, sl] = (yc * scale).astype(BF16)
            y_cur[rows, :] = jnp.dot(xn_ref[rows, :], w_ref[...], preferred_element_type=F32)

    def transpose_values(parity):
        y_prev = (y1_ref, y0_ref)[parity]
        for hh in range(tn // LANES):
            for uu in range(tm // v_unit):
                blk = y_prev[uu * v_unit:(uu + 1) * v_unit, hh * LANES:(hh + 1) * LANES]
                vt_ref[hh, uu] = blk.T.astype(BF16)

    even = (j % 2) == 0
    pl.when(even)(functools.partial(run, 0))
    pl.when(jnp.logical_not(even))(functools.partial(run, 1))
    if first_v_tile is not None:
        is_v = j > first_v_tile
        pl.when(is_v & even)(functools.partial(transpose_values, 0))
        pl.when(is_v & jnp.logical_not(even))(functools.partial(transpose_values, 1))


def _norm_proj(x, g, w, flag, gain, tn, v_layout=None):
    t, d = x.shape
    n = w.shape[1]
    tm = PROJ_TM
    assert t % tm == 0 and n % tn == 0 and tm % PROJ_ROWS == 0
    nj = n // tn

    def tile(j):
        return jnp.clip(j, 0, nj - 1)

    out_specs = pl.BlockSpec((tm, tn), lambda i, j: (i, tile(j - 1)))
    out_shape = jax.ShapeDtypeStruct((t, n), BF16)
    first_v_tile = v_unit = None
    if v_layout is not None:
        v_col, batch, seq, v_unit = v_layout
        assert v_col % tn == 0 and seq % tm == 0 and tm % v_unit == 0
        first_v_tile = v_col // tn
        heads = tn // LANES
        tiles_per_seq = seq // tm
        out_specs = [out_specs,
                     pl.BlockSpec((None, heads, tm // v_unit, LANES, v_unit),
                                  lambda i, j: (i // tiles_per_seq, jnp.clip(j - 1, first_v_tile, nj - 1) - first_v_tile,
                                                i % tiles_per_seq, 0, 0))]
        out_shape = [out_shape,
                     jax.ShapeDtypeStruct((batch, (n - v_col) // LANES, seq // v_unit, LANES, v_unit), BF16)]

    return pl.pallas_call(
        functools.partial(_norm_proj_kernel, first_v_tile=first_v_tile, v_unit=v_unit),
        grid=(t // tm, nj + 1),
        in_specs=[pl.BlockSpec((tm, d), lambda i, j: (i, 0)),
                  pl.BlockSpec((1, d), lambda i, j: (0, 0)),
                  pl.BlockSpec((d, tn), lambda i, j: (0, tile(j))),
                  pl.BlockSpec((1, tn), lambda i, j: (0, tile(j - 1))),
                  pl.BlockSpec((1, tn), lambda i, j: (0, tile(j - 1)))],
        out_specs=out_specs,
        out_shape=out_shape,
        scratch_shapes=[pltpu.VMEM((tm, d), BF16), pltpu.VMEM((tm, tn), F32), pltpu.VMEM((tm, tn), F32)],
        compiler_params=_cparams("arbitrary", "arbitrary"),
        name="norm_proj",
    )(x, g, w, flag, gain)


def _out_proj_kernel(a_ref, w_ref, h_ref, o_ref):
    o_ref[...] = h_ref[...] + jnp.dot(a_ref[...], w_ref[...], preferred_element_type=F32)


def _out_proj(a, w, h):
    t, k = a.shape
    n = w.shape[1]
    tm, tn = OUT_TM, OUT_TN
    assert t % tm == 0 and n % tn == 0
    return pl.pallas_call(
        _out_proj_kernel,
        grid=(t // tm, n // tn),
        in_specs=[pl.BlockSpec((tm, k), lambda i, j: (i, 0)),
                  pl.BlockSpec((k, tn), lambda i, j: (0, j)),
                  pl.BlockSpec((tm, tn), lambda i, j: (i, j))],
        out_specs=pl.BlockSpec((tm, tn), lambda i, j: (i, j)),
        out_shape=jax.ShapeDtypeStruct((t, n), F32),
        compiler_params=_cparams("arbitrary", "arbitrary"),
        name="out_proj",
    )(a, w, h)


def _stack_maps(q):
    lane = lax.broadcasted_iota(jnp.int32, q.shape, 1)
    qf = q.astype(F32)
    return jnp.concatenate([jnp.where(lane < HEAD_DIM, qf, 0.0), jnp.where(lane < HEAD_DIM, 0.0, qf)],
                           axis=0).astype(BF16)


def _da_attn_kernel(fast_ref, q_ref, k_ref, vt_ref, bias_ref, lamv_ref, sg_ref, o_ref, acc_ref, l_ref, m_ref, *, t,
                    lam_init):
    i = pl.program_id(2)
    qt = q_ref[...].astype(F32).T
    row = lax.broadcasted_iota(jnp.int32, qt.shape, 0)
    qqt = jnp.concatenate([jnp.where(row < HEAD_DIM, qt, 0.0), jnp.where(row < HEAD_DIM, 0.0, qt)],
                          axis=1).astype(BF16)
    acc_ref[...] = jnp.zeros(acc_ref.shape, F32)
    l_ref[...] = jnp.zeros(l_ref.shape, F32)

    def scores(u, bias):
        k = k_ref[pl.ds(pl.multiple_of(u * t, t), t), :]
        st = jnp.dot(k, qqt, preferred_element_type=F32)
        return st if bias is None else st + bias

    def chunk(u0, biases):
        pv = None
        ls = None
        for n, bias in enumerate(biases):
            p = jnp.exp2(scores(u0 + n, bias))
            s = jnp.sum(p, axis=0, keepdims=True)
            d = jnp.dot(vt_ref[u0 + n], p.astype(BF16), preferred_element_type=F32)
            pv, ls = (d, s) if pv is None else (pv + d, ls + s)
        acc_ref[...] += pv
        l_ref[...] += ls

    def unit_running_max(u, bias):
        st = scores(u, bias)
        m_prev = m_ref[...]
        m_new = jnp.maximum(m_prev, jnp.max(st, axis=0, keepdims=True))
        alpha = jnp.exp2(m_prev - m_new)
        p = jnp.exp2(st - m_new)
        l_ref[...] = alpha * l_ref[...] + jnp.sum(p, axis=0, keepdims=True)
        acc_ref[...] = alpha * acc_ref[...] + jnp.dot(vt_ref[u], p.astype(BF16), preferred_element_type=F32)
        m_ref[...] = m_new

    fast = fast_ref[0] == 1
    odd = (i % 2) == 1

    far_pairs = jnp.maximum(i - 1, 0) // 2

    @pl.when(fast)
    def _():
        def far_quad(c, carry):
            chunk(4 * c, (None, None, None, None))
            return carry

        lax.fori_loop(0, far_pairs // 2, far_quad, 0)

    @pl.when(fast & ((far_pairs % 2) == 1))
    def _():
        chunk(2 * (far_pairs - 1), (None, None))

    @pl.when(fast & (i == 0))
    def _():
        chunk(0, (bias_ref[0, 0],))

    @pl.when(fast & odd)
    def _():
        chunk(i - 1, (bias_ref[0, 1], bias_ref[0, 0]))

    @pl.when(fast & jnp.logical_not(odd) & (i > 0))
    def _():
        chunk(i - 2, (None, bias_ref[0, 1], bias_ref[0, 0]))

    @pl.when(jnp.logical_not(fast))
    def _():
        m_ref[...] = jnp.full(m_ref.shape, NEG, F32)

        def far_unit(u, carry):
            unit_running_max(u, None)
            return carry

        lax.fori_loop(0, jnp.maximum(i - 1, 0), far_unit, 0)

        @pl.when(i >= 1)
        def _():
            unit_running_max(i - 1, bias_ref[0, 1])

        unit_running_max(i, bias_ref[0, 0])

    lv = lamv_ref[...]
    lam = (jnp.exp(jnp.sum(lv[0:1] * lv[1:2], axis=-1, keepdims=True))
           - jnp.exp(jnp.sum(lv[2:3] * lv[3:4], axis=-1, keepdims=True)) + lam_init)
    o = acc_ref[...] * (1.0 / l_ref[...])
    ot = o[:, :t] - lam * o[:, t:]
    ms = jnp.mean(ot * ot, axis=0, keepdims=True)
    ot = ot * lax.rsqrt(ms + EPS) * (sg_ref[...] * (1.0 - lam_init))
    o_ref[...] = ot.T.astype(BF16)


def _da_attn(fast, qkv, vt, bias, lamv, sub_gain, batch, seq, lam_init):
    t = DA_T
    nq = seq // t
    return pl.pallas_call(
        functools.partial(_da_attn_kernel, t=t, lam_init=lam_init),
        grid=(batch, DA_HEADS, nq),
        in_specs=[pl.BlockSpec(memory_space=pltpu.SMEM),
                  pl.BlockSpec((t, LANES), lambda b, h, i: (b * nq + i, h)),
                  pl.BlockSpec((seq, LANES), lambda b, h, i: (b, DA_HEADS + h)),
                  pl.BlockSpec((None, None, nq, DA_VDIM, t), lambda b, h, i: (b, h, 0, 0, 0)),
                  pl.BlockSpec((1, 2, t, 2 * t), lambda b, h, i: (h, 0, 0, 0)),
                  pl.BlockSpec((4, HEAD_DIM), lambda b, h, i: (0, 0)),
                  pl.BlockSpec((DA_VDIM, 1), lambda b, h, i: (0, 0))],
        out_specs=pl.BlockSpec((t, LANES), lambda b, h, i: (b * nq + i, h)),
        out_shape=jax.ShapeDtypeStruct((batch * seq, DA_HEADS * DA_VDIM), BF16),
        scratch_shapes=[pltpu.VMEM((DA_VDIM, 2 * t), F32), pltpu.VMEM((1, 2 * t), F32), pltpu.VMEM((1, 2 * t), F32)],
        compiler_params=_cparams("arbitrary", "arbitrary", "arbitrary"),
        name="da_attn",
    )(fast, qkv, qkv, vt, bias, lamv, sub_gain)


def _sw_attn_kernel(fast_ref, sink_ref, q_ref, kp_ref, kc_ref, vp_ref, vc_ref, bias_ref, o_ref, *, tq):
    g = pl.program_id(0)
    kb = jnp.concatenate([kp_ref[...], kc_ref[...]], axis=0)
    vtb = jnp.concatenate([vp_ref[...], vc_ref[...]], axis=1)
    col = lax.broadcasted_iota(jnp.int32, (1, 2 * tq), 1)

    def lane_slice(pp):
        return slice(pp * LANES, (pp + 1) * LANES)

    def scores(pp):
        qt = q_ref[:, lane_slice(pp)].astype(F32).T
        row = lax.broadcasted_iota(jnp.int32, qt.shape, 0)
        qqt = jnp.concatenate([jnp.where(row < HEAD_DIM, qt, 0.0), jnp.where(row < HEAD_DIM, 0.0, qt)],
                              axis=1).astype(BF16)
        return jnp.dot(kb, qqt, preferred_element_type=F32) + bias_ref[0, pp]

    def weighted_values(pp, st, fast):
        head = g * SW_GROUP + 2 * pp
        sink = jnp.where(col < tq, sink_ref[head], sink_ref[head + 1]) * LOG2E
        if fast:
            ot = jnp.dot(vtb, jnp.exp2(st).astype(BF16), preferred_element_type=F32)
            return ot, jnp.exp2(sink)
        m = jnp.maximum(jnp.max(st, axis=0, keepdims=True), sink)
        ot = jnp.dot(vtb, jnp.exp2(st - m).astype(BF16), preferred_element_type=F32)
        return ot, jnp.exp2(sink - m)

    def finish(pp, ot, sink_term):
        ot = ot[:HEAD_DIM] * (1.0 / (ot[HEAD_DIM:HEAD_DIM + 1] + sink_term))
        o_ref[:, lane_slice(pp)] = jnp.concatenate([ot[:, :tq], ot[:, tq:]], axis=0).T.astype(BF16)

    def all_pairs(fast):
        pairs = range(SW_GROUP // 2)
        sts = [scores(pp) for pp in pairs]
        ots = [weighted_values(pp, sts[pp], fast) for pp in pairs]
        for pp in pairs:
            finish(pp, *ots[pp])

    pl.when(fast_ref[0] == 1)(functools.partial(all_pairs, True))
    pl.when(fast_ref[0] != 1)(functools.partial(all_pairs, False))


def _sw_attn(fast, q, kdup, vt, bias, sinks, batch, seq):
    tq = SW_TQ
    nq = seq // tq
    r = tq // SW_WINDOW
    band = SW_WINDOW + tq
    gw = SW_GROUP * HEAD_DIM
    return pl.pallas_call(
        functools.partial(_sw_attn_kernel, tq=tq),
        grid=(SW_KV_HEADS, batch, nq),
        in_specs=[pl.BlockSpec(memory_space=pltpu.SMEM),
                  pl.BlockSpec(memory_space=pltpu.SMEM),
                  pl.BlockSpec((tq, gw), lambda g, b, i: (b * nq + i, g)),
                  pl.BlockSpec((SW_WINDOW, LANES), lambda g, b, i: (jnp.maximum((b * nq + i) * r - 1, 0), g)),
                  pl.BlockSpec((tq, LANES), lambda g, b, i: (b * nq + i, g)),
                  pl.BlockSpec((None, None, SW_VROWS, SW_WINDOW), lambda g, b, i: (b, g, 0, jnp.maximum(i * r - 1, 0))),
                  pl.BlockSpec((None, None, SW_VROWS, tq), lambda g, b, i: (b, g, 0, i)),
                  pl.BlockSpec((1, SW_GROUP // 2, band, 2 * tq), lambda g, b, i: (jnp.where(i == 0, 1, 0), g, 0, 0))],
        out_specs=pl.BlockSpec((tq, gw), lambda g, b, i: (b * nq + i, g)),
        out_shape=jax.ShapeDtypeStruct((batch * seq, SW_Q_HEADS * HEAD_DIM), BF16),
        compiler_params=_cparams("arbitrary", "arbitrary", "arbitrary"),
        name="sw_attn",
    )(fast, sinks, q, kdup, kdup, vt, vt, bias)


def _causal_conv(u, halo, cw_ref, cb_ref):
    row = lax.broadcasted_iota(jnp.int32, (8, 1), 0)
    r1 = pltpu.roll(u, 1, axis=0)
    r2 = pltpu.roll(u, 2, axis=0)
    u1 = jnp.concatenate([jnp.where(row == 0, halo[7:8], r1[:8]), r1[8:]], axis=0)
    u2 = jnp.concatenate([jnp.where(row == 0, halo[6:7], jnp.where(row == 1, halo[7:8], r2[:8])), r2[8:]], axis=0)
    return cw_ref[0:1, :] * u2 + cw_ref[1:2, :] * u1 + cw_ref[2:3, :] * u + cb_ref[...]


def _ffn_kernel(h_ref, g_ref, wg_ref, wu_ref, cwg_ref, cwu_ref, cbg_ref, cbu_ref, wo_ref, o_ref,
                f_ref, act_ref, halo_g_ref, halo_u_ref, *, tiles_per_seq, nj):
    i = pl.program_id(0)
    j = pl.program_id(1)
    tm = h_ref.shape[0]

    @pl.when(j == 0)
    def _():
        x = h_ref[...]
        f_ref[...] = _rms_rows(x, g_ref[...]).astype(BF16)
        o_ref[...] = x

    @pl.when(((i % tiles_per_seq) == 0) & (j < nj))
    def _():
        halo_g_ref[j] = jnp.zeros(halo_g_ref.shape[1:], F32)
        halo_u_ref[j] = jnp.zeros(halo_u_ref.shape[1:], F32)

    def run(do_up, do_down):
        if do_up:
            f = f_ref[...]
            ug = jnp.dot(f, wg_ref[...], preferred_element_type=F32)
            uu = jnp.dot(f, wu_ref[...], preferred_element_type=F32)
        if do_down:
            o_ref[...] += jnp.dot(act_ref[(j + 1) % 2], wo_ref[...], preferred_element_type=F32)
        if do_up:
            def conv(u, cw_ref, cb_ref, halo_ref):
                halo = halo_ref[j]
                halo_ref[j] = u[tm - 8:, :]
                return _causal_conv(u, halo, cw_ref, cb_ref)

            gate = conv(ug, cwg_ref, cbg_ref, halo_g_ref)
            up = conv(uu, cwu_ref, cbu_ref, halo_u_ref)
            act_ref[j % 2] = (gate * (1.0 / (1.0 + jnp.exp(-gate))) * up).astype(BF16)

    pl.when(j == 0)(functools.partial(run, True, False))
    pl.when((j > 0) & (j < nj))(functools.partial(run, True, True))
    pl.when(j == nj)(functools.partial(run, False, True))


def _ffn(h, g, w_in, conv_w, conv_b, w_out, seq):
    t, d = h.shape
    dff = w_out.shape[0]
    tm, tf = FFN_TM, FFN_TF
    assert t % tm == 0 and seq % tm == 0 and dff % tf == 0
    nj = dff // tf

    def up_tile(j):
        return jnp.minimum(j, nj - 1)

    return pl.pallas_call(
        functools.partial(_ffn_kernel, tiles_per_seq=seq // tm, nj=nj),
        grid=(t // tm, nj + 1),
        in_specs=[pl.BlockSpec((tm, d), lambda i, j: (i, 0)),
                  pl.BlockSpec((1, d), lambda i, j: (0, 0)),
                  pl.BlockSpec((d, tf), lambda i, j: (0, up_tile(j))),
                  pl.BlockSpec((d, tf), lambda i, j: (0, nj + up_tile(j))),
                  pl.BlockSpec((CONV_WIDTH, tf), lambda i, j: (0, up_tile(j))),
                  pl.BlockSpec((CONV_WIDTH, tf), lambda i, j: (0, nj + up_tile(j))),
                  pl.BlockSpec((1, tf), lambda i, j: (0, up_tile(j))),
                  pl.BlockSpec((1, tf), lambda i, j: (0, nj + up_tile(j))),
                  pl.BlockSpec((tf, d), lambda i, j: (jnp.maximum(j - 1, 0), 0))],
        out_specs=pl.BlockSpec((tm, d), lambda i, j: (i, 0)),
        out_shape=jax.ShapeDtypeStruct((t, d), F32),
        scratch_shapes=[pltpu.VMEM((tm, d), BF16),
                        pltpu.VMEM((2, tm, tf), BF16),
                        pltpu.VMEM((nj, 8, tf), F32),
                        pltpu.VMEM((nj, 8, tf), F32)],
        compiler_params=_cparams("arbitrary", "arbitrary"),
        name="conv_ffn",
    )(h, g, w_in, w_in, conv_w, conv_w, conv_b, conv_b, w_out)


def _row(v):
    return v.reshape(1, -1).astype(F32)


def kernel(x, rel_table, attn_norm_g, ffn_norm_g, da_w_qkv, da_w_o, da_q_gain, da_k_gain, da_lam_q1, da_lam_k1, da_lam_q2, da_lam_k2, da_sub_gain, sw_w_qkv, sw_w_o, sw_q_gain, sw_k_gain, sw_sinks, ffn_w_in, ffn_conv_w, ffn_conv_b, ffn_w_out):
    batch, seq, d = x.shape
    depth = attn_norm_g.shape[0]
    t = batch * seq
    h = x.reshape(t, d)
    q_scale = HEAD_DIM ** -0.5

    da_bias = _da_bias(rel_table, DA_T)
    sw_bias = _sw_bias(rel_table, SW_TQ)

    da_qk = 2 * DA_HEADS * HEAD_DIM
    da_flag = jnp.concatenate([jnp.ones((1, 2 * da_qk), F32), jnp.zeros((1, DA_HEADS * DA_VDIM), F32)], axis=1)
    sw_qw = SW_Q_HEADS * HEAD_DIM
    sw_kw = SW_KV_HEADS * HEAD_DIM
    sw_flag = jnp.concatenate([jnp.ones((1, sw_qw + sw_kw), F32), jnp.zeros((1, sw_kw), F32)], axis=1)

    for i in range(depth):
        j = i // N_MIXERS
        if i % N_MIXERS == 0:
            gain = jnp.concatenate([jnp.tile(da_q_gain[j], da_qk // HEAD_DIM) * (q_scale * LOG2E),
                                    jnp.tile(da_k_gain[j], da_qk // HEAD_DIM),
                                    jnp.ones((DA_HEADS * DA_VDIM,), F32)])
            qkv, vt = _norm_proj(h, _row(attn_norm_g[i]), da_w_qkv[j].astype(BF16), da_flag, _row(gain), PROJ_TN_DA,
                                 v_layout=(2 * da_qk, batch, seq, DA_T))
            score_bound = (HEAD_DIM * q_scale * jnp.max(jnp.abs(da_q_gain[j])) * jnp.max(jnp.abs(da_k_gain[j]))
                           + jnp.max(jnp.abs(rel_table - rel_table[NUM_BUCKETS - 1]))) * LOG2E
            fast = (score_bound <= FAST_BOUND).astype(jnp.int32).reshape(1)
            lamv = jnp.stack([da_lam_q1[j], da_lam_k1[j], da_lam_q2[j], da_lam_k2[j]]).astype(F32)
            lam_init = 0.8 - 0.6 * math.exp(-0.3 * i)
            a = _da_attn(fast, qkv, vt, da_bias, lamv, da_sub_gain[j].reshape(DA_VDIM, 1).astype(F32), batch, seq,
                         lam_init)
            h = _out_proj(a, da_w_o[j].astype(BF16), h)
        else:
            gain = jnp.concatenate([jnp.tile(sw_q_gain[j], SW_Q_HEADS) * (q_scale * LOG2E),
                                    jnp.tile(sw_k_gain[j], SW_KV_HEADS),
                                    jnp.ones((sw_kw,), F32)])
            qkv = _norm_proj(h, _row(attn_norm_g[i]), sw_w_qkv[j].astype(BF16), sw_flag, _row(gain), PROJ_TN_SW)
            k = qkv[:, sw_qw:sw_qw + sw_kw].reshape(t, SW_KV_HEADS, 1, HEAD_DIM)
            kdup = jnp.broadcast_to(k, (t, SW_KV_HEADS, 2, HEAD_DIM)).reshape(t, SW_KV_HEADS * LANES)
            vt = qkv[:, sw_qw + sw_kw:].reshape(batch, seq, SW_KV_HEADS, HEAD_DIM).transpose(0, 2, 3, 1)
            ones_rows = jnp.zeros((SW_VROWS - HEAD_DIM, seq), BF16).at[0].set(1.0)
            vt = jnp.concatenate([vt, jnp.broadcast_to(ones_rows, vt.shape[:2] + ones_rows.shape)], axis=2)
            score_bound = (HEAD_DIM * q_scale * jnp.max(jnp.abs(sw_q_gain[j])) * jnp.max(jnp.abs(sw_k_gain[j]))
                           + jnp.max(jnp.abs(rel_table)) + jnp.max(jnp.abs(sw_sinks[j]))) * LOG2E
            fast = (score_bound <= FAST_BOUND).astype(jnp.int32).reshape(1)
            a = _sw_attn(fast, qkv, kdup, vt, sw_bias, sw_sinks[j].astype(F32), batch, seq)
            h = _out_proj(a, sw_w_o[j].astype(BF16), h)
        h = _ffn(h, _row(ffn_norm_g[i]), ffn_w_in[i].astype(BF16), ffn_conv_w[i].astype(F32),
                 _row(ffn_conv_b[i]), ffn_w_out[i].astype(BF16), seq)
    return h.reshape(batch, seq, d)
```

```python
import functools
import math

import numpy as np
import jax
import jax.numpy as jnp
from jax import lax
from jax.experimental import pallas as pl
from jax.experimental.pallas import tpu as pltpu

F32 = jnp.float32
BF16 = jnp.bfloat16

HEAD_DIM = 64
N_MAPS = 32
DA_HEADS = 16
DA_VDIM = 128
SW_Q_HEADS = 32
SW_KV_HEADS = 4
SW_GROUP = SW_Q_HEADS // SW_KV_HEADS
SW_WINDOW = 128
NUM_BUCKETS = 32
MAX_DISTANCE = 128
CONV_WIDTH = 3
EPS = 1e-6
N_MIXERS = 2

LANES = 128
NEG = -1e30
LOG2E = math.log2(math.e)
VMEM_LIMIT = 56 * 1024 * 1024

PROJ_TM = 1024
PROJ_TN_DA = 1024
PROJ_TN_SW = 1280
PROJ_ROWS = 256
OUT_TM = 1024
OUT_TN = 1024
FFN_TM = 512
FFN_TF = 512
DA_T = 512
FAST_BOUND = 60.0
SW_TQ = 256
SW_VROWS = HEAD_DIM + 16


def _cparams(*sem):
    return pltpu.CompilerParams(dimension_semantics=sem, vmem_limit_bytes=VMEM_LIMIT)


def _t5_bucket_np(dist):
    max_exact = NUM_BUCKETS // 2
    d = np.maximum(dist, 0)
    df = np.maximum(d, 1).astype(np.float64)
    large = max_exact + (np.log(df / max_exact) / math.log(MAX_DISTANCE / max_exact)
                         * (NUM_BUCKETS - max_exact)).astype(np.int32)
    large = np.minimum(large, NUM_BUCKETS - 1)
    return np.where(d < max_exact, d, large).astype(np.int32)


def _da_bias_kernel(tab_ref, bucket_ref, out_ref, *, t):
    h = pl.program_id(0)
    s = MAX_DISTANCE
    for off in range(2):
        for m in range(2):
            col = 2 * h + m
            far = tab_ref[NUM_BUCKETS - 1, col]
            for kr in range(t // s):
                for qc in range(t // s):
                    base = off * t + (qc - kr) * s
                    dst = (0, off, slice(kr * s, (kr + 1) * s), slice(m * t + qc * s, m * t + (qc + 1) * s))
                    if base - (s - 1) >= MAX_DISTANCE:
                        out_ref[dst] = jnp.zeros((s, s), F32)
                    elif base + (s - 1) < 0:
                        out_ref[dst] = jnp.full((s, s), NEG, F32)
                    else:
                        bk = bucket_ref[off, kr * s:(kr + 1) * s, qc * s:(qc + 1) * s]
                        acc = jnp.zeros(bk.shape, F32)
                        for b in range(NUM_BUCKETS - 1):
                            acc = jnp.where(bk == b, (tab_ref[b, col] - far) * LOG2E, acc)
                        out_ref[dst] = jnp.where(bk < 0, NEG, acc)


def _da_bias(rel_table, t):
    key = np.arange(t)[:, None]
    qry = np.arange(t)[None, :]
    d0 = qry - key
    d1 = t + qry - key
    bucket = np.stack([np.where(d0 >= 0, _t5_bucket_np(d0), -1), _t5_bucket_np(d1)]).astype(np.int32)
    return pl.pallas_call(
        functools.partial(_da_bias_kernel, t=t),
        grid=(DA_HEADS,),
        in_specs=[pl.BlockSpec(memory_space=pltpu.SMEM),
                  pl.BlockSpec((2, t, t), lambda h: (0, 0, 0))],
        out_specs=pl.BlockSpec((1, 2, t, 2 * t), lambda h: (h, 0, 0, 0)),
        out_shape=jax.ShapeDtypeStruct((DA_HEADS, 2, t, 2 * t), F32),
        compiler_params=_cparams("arbitrary"),
        name="da_bias",
    )(rel_table, jnp.asarray(bucket))


def _sw_bias_kernel(tab_ref, bucket_ref, out_ref, *, tq):
    p = pl.program_id(1)
    bk = bucket_ref[0]
    for m in range(2):
        col = 2 * p + m
        acc = jnp.zeros(bk.shape, F32)
        for b in range(NUM_BUCKETS):
            acc = jnp.where(bk == b, tab_ref[b, col] * LOG2E, acc)
        out_ref[0, 0, :, m * tq:(m + 1) * tq] = jnp.where(bk < 0, NEG, acc)


def _sw_bias(rel_table, tq):
    band = SW_WINDOW + tq
    key = np.arange(band)[:, None]
    qry = np.arange(tq)[None, :]
    dist = qry + SW_WINDOW - key
    bucket = np.where((dist >= 0) & (dist < SW_WINDOW), _t5_bucket_np(dist), -1).astype(np.int32)
    first = np.where(key < SW_WINDOW, -1, bucket).astype(np.int32)
    return pl.pallas_call(
        functools.partial(_sw_bias_kernel, tq=tq),
        grid=(2, SW_Q_HEADS // 2),
        in_specs=[pl.BlockSpec(memory_space=pltpu.SMEM),
                  pl.BlockSpec((1, band, tq), lambda v, p: (v, 0, 0))],
        out_specs=pl.BlockSpec((1, 1, band, 2 * tq), lambda v, p: (v, p, 0, 0)),
        out_shape=jax.ShapeDtypeStruct((2, SW_Q_HEADS // 2, band, 2 * tq), F32),
        compiler_params=_cparams("arbitrary", "arbitrary"),
        name="sw_bias",
    )(rel_table, jnp.asarray(np.stack([bucket, first])))


def _rms_rows(x, g):
    ms = jnp.mean(x * x, axis=-1, keepdims=True)
    return x * lax.rsqrt(ms + EPS) * g


def _norm_proj_kernel(x_ref, g_ref, w_ref, flag_ref, gain_ref, o_ref, *rest, first_v_tile, v_unit):
    if first_v_tile is None:
        xn_ref, y0_ref, y1_ref = rest
    else:
        vt_ref, xn_ref, y0_ref, y1_ref = rest
    i = pl.program_id(0)
    j = pl.program_id(1)
    tm, tn = y0_ref.shape

    @pl.when((i == 0) & (j == 0))
    def _():
        y1_ref[...] = jnp.zeros(y1_ref.shape, F32)

    @pl.when(j == 0)
    def _():
        xn_ref[...] = _rms_rows(x_ref[...], g_ref[...]).astype(BF16)

    def run(parity):
        y_cur, y_prev = (y0_ref, y1_ref)[::1 - 2 * parity]
        lane = lax.broadcasted_iota(jnp.int32, (1, LANES), 1)
        low = lane < HEAD_DIM
        for r in range(tm // PROJ_ROWS):
            rows = slice(r * PROJ_ROWS, (r + 1) * PROJ_ROWS)
            for c in range(tn // LANES):
                sl = slice(c * LANES, (c + 1) * LANES)
                yc = y_prev[rows, sl]
                y2 = yc * yc
                ms_lo = jnp.sum(jnp.where(low, y2, 0.0), axis=-1, keepdims=True) * (1.0 / HEAD_DIM)
                ms_hi = jnp.sum(jnp.where(low, 0.0, y2), axis=-1, keepdims=True) * (1.0 / HEAD_DIM)
                inv = jnp.where(low, lax.rsqrt(ms_lo + EPS), lax.rsqrt(ms_hi + EPS))
                scale = jnp.where(flag_ref[:, sl] > 0.0, inv, 1.0) * gain_ref[:, sl]
                o_ref[rows, sl] = (yc * scale).astype(BF16)
            y_cur[rows, :] = jnp.dot(xn_ref[rows, :], w_ref[...], preferred_element_type=F32)

    def transpose_values(parity):
        y_prev = (y1_ref, y0_ref)[parity]
        for hh in range(tn // LANES):
            for uu in range(tm // v_unit):
                blk = y_prev[uu * v_unit:(uu + 1) * v_unit, hh * LANES:(hh + 1) * LANES]
                vt_ref[hh, uu] = blk.T.astype(BF16)

    even = (j % 2) == 0
    pl.when(even)(functools.partial(run, 0))
    pl.when(jnp.logical_not(even))(functools.partial(run, 1))
    if first_v_tile is not None:
        is_v = j > first_v_tile
        pl.when(is_v & even)(functools.partial(transpose_values, 0))
        pl.when(is_v & jnp.logical_not(even))(functools.partial(transpose_values, 1))


def _norm_proj(x, g, w_layers, layer, flag, gain, tn, v_layout=None):
    t, d = x.shape
    n = w_layers.shape[2]
    tm = PROJ_TM
    assert t % tm == 0 and n % tn == 0 and tm % PROJ_ROWS == 0
    nj = n // tn

    def tile(j):
        return jnp.clip(j, 0, nj - 1)

    out_specs = pl.BlockSpec((tm, tn), lambda i, j: (i, tile(j - 1)))
    out_shape = jax.ShapeDtypeStruct((t, n), BF16)
    first_v_tile = v_unit = None
    if v_layout is not None:
        v_col, batch, seq, v_unit = v_layout
        assert v_col % tn == 0 and seq % tm == 0 and tm % v_unit == 0
        first_v_tile = v_col // tn
        heads = tn // LANES
        tiles_per_seq = seq // tm
        out_specs = [out_specs,
                     pl.BlockSpec((None, heads, tm // v_unit, LANES, v_unit),
                                  lambda i, j: (i // tiles_per_seq, jnp.clip(j - 1, first_v_tile, nj - 1) - first_v_tile,
                                                i % tiles_per_seq, 0, 0))]
        out_shape = [out_shape,
                     jax.ShapeDtypeStruct((batch, (n - v_col) // LANES, seq // v_unit, LANES, v_unit), BF16)]

    return pl.pallas_call(
        functools.partial(_norm_proj_kernel, first_v_tile=first_v_tile, v_unit=v_unit),
        grid=(t // tm, nj + 1),
        in_specs=[pl.BlockSpec((tm, d), lambda i, j: (i, 0)),
                  pl.BlockSpec((1, d), lambda i, j: (0, 0)),
                  pl.BlockSpec((None, d, tn), lambda i, j: (layer, 0, tile(j))),
                  pl.BlockSpec((1, tn), lambda i, j: (0, tile(j - 1))),
                  pl.BlockSpec((1, tn), lambda i, j: (0, tile(j - 1)))],
        out_specs=out_specs,
        out_shape=out_shape,
        scratch_shapes=[pltpu.VMEM((tm, d), BF16), pltpu.VMEM((tm, tn), F32), pltpu.VMEM((tm, tn), F32)],
        compiler_params=_cparams("arbitrary", "arbitrary"),
        name="norm_proj",
    )(x, g, w_layers, flag, gain)


def _out_proj_kernel(a_ref, w_ref, h_ref, o_ref):
    o_ref[...] = h_ref[...] + jnp.dot(a_ref[...], w_ref[...], preferred_element_type=F32)


def _out_proj(a, w_layers, layer, h):
    t, k = a.shape
    n = w_layers.shape[2]
    tm, tn = OUT_TM, OUT_TN
    assert t % tm == 0 and n % tn == 0
    return pl.pallas_call(
        _out_proj_kernel,
        grid=(t // tm, n // tn),
        in_specs=[pl.BlockSpec((tm, k), lambda i, j: (i, 0)),
                  pl.BlockSpec((None, k, tn), lambda i, j: (layer, 0, j)),
                  pl.BlockSpec((tm, tn), lambda i, j: (i, j))],
        out_specs=pl.BlockSpec((tm, tn), lambda i, j: (i, j)),
        out_shape=jax.ShapeDtypeStruct((t, n), F32),
        compiler_params=_cparams("arbitrary", "arbitrary"),
        name="out_proj",
    )(a, w_layers, h)


def _stack_maps(q):
    lane = lax.broadcasted_iota(jnp.int32, q.shape, 1)
    qf = q.astype(F32)
    return jnp.concatenate([jnp.where(lane < HEAD_DIM, qf, 0.0), jnp.where(lane < HEAD_DIM, 0.0, qf)],
                           axis=0).astype(BF16)


def _da_attn_kernel(fast_ref, q_ref, k_ref, vt_ref, bias_ref, lamv_ref, sg_ref, o_ref, acc_ref, l_ref, m_ref, *, t,
                    lam_init):
    i = pl.program_id(2)
    qt = q_ref[...].astype(F32).T
    row = lax.broadcasted_iota(jnp.int32, qt.shape, 0)
    qqt = jnp.concatenate([jnp.where(row < HEAD_DIM, qt, 0.0), jnp.where(row < HEAD_DIM, 0.0, qt)],
                          axis=1).astype(BF16)
    acc_ref[...] = jnp.zeros(acc_ref.shape, F32)
    l_ref[...] = jnp.zeros(l_ref.shape, F32)

    def scores(u, bias):
        k = k_ref[pl.ds(pl.multiple_of(u * t, t), t), :]
        st = jnp.dot(k, qqt, preferred_element_type=F32)
        return st if bias is None else st + bias

    def chunk(u0, biases):
        pv = None
        ls = None
        for n, bias in enumerate(biases):
            p = jnp.exp2(scores(u0 + n, bias))
            s = jnp.sum(p, axis=0, keepdims=True)
            d = jnp.dot(vt_ref[u0 + n], p.astype(BF16), preferred_element_type=F32)
            pv, ls = (d, s) if pv is None else (pv + d, ls + s)
        acc_ref[...] += pv
        l_ref[...] += ls

    def unit_running_max(u, bias):
        st = scores(u, bias)
        m_prev = m_ref[...]
        m_new = jnp.maximum(m_prev, jnp.max(st, axis=0, keepdims=True))
        alpha = jnp.exp2(m_prev - m_new)
        p = jnp.exp2(st - m_new)
        l_ref[...] = alpha * l_ref[...] + jnp.sum(p, axis=0, keepdims=True)
        acc_ref[...] = alpha * acc_ref[...] + jnp.dot(vt_ref[u], p.astype(BF16), preferred_element_type=F32)
        m_ref[...] = m_new

    fast = fast_ref[0] == 1
    odd = (i % 2) == 1

    far_pairs = jnp.maximum(i - 1, 0) // 2

    @pl.when(fast)
    def _():
        def far_quad(c, carry):
            chunk(4 * c, (None, None, None, None))
            return carry

        lax.fori_loop(0, far_pairs // 2, far_quad, 0)

    @pl.when(fast & ((far_pairs % 2) == 1))
    def _():
        chunk(2 * (far_pairs - 1), (None, None))

    @pl.when(fast & (i == 0))
    def _():
        chunk(0, (bias_ref[0, 0],))

    @pl.when(fast & odd)
    def _():
        chunk(i - 1, (bias_ref[0, 1], bias_ref[0, 0]))

    @pl.when(fast & jnp.logical_not(odd) & (i > 0))
    def _():
        chunk(i - 2, (None, bias_ref[0, 1], bias_ref[0, 0]))

    @pl.when(jnp.logical_not(fast))
    def _():
        m_ref[...] = jnp.full(m_ref.shape, NEG, F32)

        def far_unit(u, carry):
            unit_running_max(u, None)
            return carry

        lax.fori_loop(0, jnp.maximum(i - 1, 0), far_unit, 0)

        @pl.when(i >= 1)
        def _():
            unit_running_max(i - 1, bias_ref[0, 1])

        unit_running_max(i, bias_ref[0, 0])

    lv = lamv_ref[...]
    lam = (jnp.exp(jnp.sum(lv[0:1] * lv[1:2], axis=-1, keepdims=True))
           - jnp.exp(jnp.sum(lv[2:3] * lv[3:4], axis=-1, keepdims=True)) + lam_init)
    o = acc_ref[...] * (1.0 / l_ref[...])
    ot = o[:, :t] - lam * o[:, t:]
    ms = jnp.mean(ot * ot, axis=0, keepdims=True)
    ot = ot * lax.rsqrt(ms + EPS) * (sg_ref[...] * (1.0 - lam_init))
    o_ref[...] = ot.T.astype(BF16)


def _da_attn(fast, qkv, vt, bias, lamv, sub_gain, batch, seq, lam_init):
    t = DA_T
    nq = seq // t
    return pl.pallas_call(
        functools.partial(_da_attn_kernel, t=t, lam_init=lam_init),
        grid=(batch, DA_HEADS, nq),
        in_specs=[pl.BlockSpec(memory_space=pltpu.SMEM),
                  pl.BlockSpec((t, LANES), lambda b, h, i: (b * nq + i, h)),
                  pl.BlockSpec((seq, LANES), lambda b, h, i: (b, DA_HEADS + h)),
                  pl.BlockSpec((None, None, nq, DA_VDIM, t), lambda b, h, i: (b, h, 0, 0, 0)),
                  pl.BlockSpec((1, 2, t, 2 * t), lambda b, h, i: (h, 0, 0, 0)),
                  pl.BlockSpec((4, HEAD_DIM), lambda b, h, i: (0, 0)),
                  pl.BlockSpec((DA_VDIM, 1), lambda b, h, i: (0, 0))],
        out_specs=pl.BlockSpec((t, LANES), lambda b, h, i: (b * nq + i, h)),
        out_shape=jax.ShapeDtypeStruct((batch * seq, DA_HEADS * DA_VDIM), BF16),
        scratch_shapes=[pltpu.VMEM((DA_VDIM, 2 * t), F32), pltpu.VMEM((1, 2 * t), F32), pltpu.VMEM((1, 2 * t), F32)],
        compiler_params=_cparams("arbitrary", "arbitrary", "arbitrary"),
        name="da_attn",
    )(fast, qkv, qkv, vt, bias, lamv, sub_gain)


def _sw_attn_kernel(fast_ref, sink_ref, q_ref, kp_ref, kc_ref, vp_ref, vc_ref, bias_ref, o_ref, *, tq):
    g = pl.program_id(0)
    kb = jnp.concatenate([kp_ref[...], kc_ref[...]], axis=0)
    vtb = jnp.concatenate([vp_ref[...], vc_ref[...]], axis=1)
    col = lax.broadcasted_iota(jnp.int32, (1, 2 * tq), 1)

    def lane_slice(pp):
        return slice(pp * LANES, (pp + 1) * LANES)

    def scores(pp):
        qt = q_ref[:, lane_slice(pp)].astype(F32).T
        row = lax.broadcasted_iota(jnp.int32, qt.shape, 0)
        qqt = jnp.concatenate([jnp.where(row < HEAD_DIM, qt, 0.0), jnp.where(row < HEAD_DIM, 0.0, qt)],
                              axis=1).astype(BF16)
        return jnp.dot(kb, qqt, preferred_element_type=F32) + bias_ref[0, pp]

    def weighted_values(pp, st, fast):
        head = g * SW_GROUP + 2 * pp
        sink = jnp.where(col < tq, sink_ref[head], sink_ref[head + 1]) * LOG2E
        if fast:
            ot = jnp.dot(vtb, jnp.exp2(st).astype(BF16), preferred_element_type=F32)
            return ot, jnp.exp2(sink)
        m = jnp.maximum(jnp.max(st, axis=0, keepdims=True), sink)
        ot = jnp.dot(vtb, jnp.exp2(st - m).astype(BF16), preferred_element_type=F32)
        return ot, jnp.exp2(sink - m)

    def finish(pp, ot, sink_term):
        ot = ot[:HEAD_DIM] * (1.0 / (ot[HEAD_DIM:HEAD_DIM + 1] + sink_term))
        o_ref[:, lane_slice(pp)] = jnp.concatenate([ot[:, :tq], ot[:, tq:]], axis=0).T.astype(BF16)

    def all_pairs(fast):
        pairs = range(SW_GROUP // 2)
        sts = [scores(pp) for pp in pairs]
        ots = [weighted_values(pp, sts[pp], fast) for pp in pairs]
        for pp in pairs:
            finish(pp, *ots[pp])

    pl.when(fast_ref[0] == 1)(functools.partial(all_pairs, True))
    pl.when(fast_ref[0] != 1)(functools.partial(all_pairs, False))


def _sw_attn(fast, q, kdup, vt, bias, sinks, batch, seq):
    tq = SW_TQ
    nq = seq // tq
    r = tq // SW_WINDOW
    band = SW_WINDOW + tq
    gw = SW_GROUP * HEAD_DIM
    return pl.pallas_call(
        functools.partial(_sw_attn_kernel, tq=tq),
        grid=(SW_KV_HEADS, batch, nq),
        in_specs=[pl.BlockSpec(memory_space=pltpu.SMEM),
                  pl.BlockSpec(memory_space=pltpu.SMEM),
                  pl.BlockSpec((tq, gw), lambda g, b, i: (b * nq + i, g)),
                  pl.BlockSpec((SW_WINDOW, LANES), lambda g, b, i: (jnp.maximum((b * nq + i) * r - 1, 0), g)),
                  pl.BlockSpec((tq, LANES), lambda g, b, i: (b * nq + i, g)),
                  pl.BlockSpec((None, None, SW_VROWS, SW_WINDOW), lambda g, b, i: (b, g, 0, jnp.maximum(i * r - 1, 0))),
                  pl.BlockSpec((None, None, SW_VROWS, tq), lambda g, b, i: (b, g, 0, i)),
                  pl.BlockSpec((1, SW_GROUP // 2, band, 2 * tq), lambda g, b, i: (jnp.where(i == 0, 1, 0), g, 0, 0))],
        out_specs=pl.BlockSpec((tq, gw), lambda g, b, i: (b * nq + i, g)),
        out_shape=jax.ShapeDtypeStruct((batch * seq, SW_Q_HEADS * HEAD_DIM), BF16),
        compiler_params=_cparams("arbitrary", "arbitrary", "arbitrary"),
        name="sw_attn",
    )(fast, sinks, q, kdup, kdup, vt, vt, bias)


def _causal_conv(u, halo, cw_ref, cb_ref):
    row = lax.broadcasted_iota(jnp.int32, (8, 1), 0)
    r1 = pltpu.roll(u, 1, axis=0)
    r2 = pltpu.roll(u, 2, axis=0)
    u1 = jnp.concatenate([jnp.where(row == 0, halo[7:8], r1[:8]), r1[8:]], axis=0)
    u2 = jnp.concatenate([jnp.where(row == 0, halo[6:7], jnp.where(row == 1, halo[7:8], r2[:8])), r2[8:]], axis=0)
    return cw_ref[0:1, :] * u2 + cw_ref[1:2, :] * u1 + cw_ref[2:3, :] * u + cb_ref[...]


def _ffn_kernel(h_ref, g_ref, wg_ref, wu_ref, cwg_ref, cwu_ref, cbg_ref, cbu_ref, wo_ref, o_ref,
                f_ref, act_ref, halo_g_ref, halo_u_ref, *, tiles_per_seq, nj):
    i = pl.program_id(0)
    j = pl.program_id(1)
    tm = h_ref.shape[0]

    @pl.when(j == 0)
    def _():
        x = h_ref[...]
        f_ref[...] = _rms_rows(x, g_ref[...]).astype(BF16)
        o_ref[...] = x

    @pl.when(((i % tiles_per_seq) == 0) & (j < nj))
    def _():
        halo_g_ref[j] = jnp.zeros(halo_g_ref.shape[1:], F32)
        halo_u_ref[j] = jnp.zeros(halo_u_ref.shape[1:], F32)

    def run(do_up, do_down):
        if do_up:
            f = f_ref[...]
            ug = jnp.dot(f, wg_ref[...], preferred_element_type=F32)
            uu = jnp.dot(f, wu_ref[...], preferred_element_type=F32)
        if do_down:
            o_ref[...] += jnp.dot(act_ref[(j + 1) % 2], wo_ref[...], preferred_element_type=F32)
        if do_up:
            def conv(u, cw_ref, cb_ref, halo_ref):
                halo = halo_ref[j]
                halo_ref[j] = u[tm - 8:, :]
                return _causal_conv(u, halo, cw_ref, cb_ref)

            gate = conv(ug, cwg_ref, cbg_ref, halo_g_ref)
            up = conv(uu, cwu_ref, cbu_ref, halo_u_ref)
            act_ref[j % 2] = (gate * (1.0 / (1.0 + jnp.exp(-gate))) * up).astype(BF16)

    pl.when(j == 0)(functools.partial(run, True, False))
    pl.when((j > 0) & (j < nj))(functools.partial(run, True, True))
    pl.when(j == nj)(functools.partial(run, False, True))


def _ffn(h, g, w_in_layers, layer, conv_w, conv_b, w_out_layers, seq):
    t, d = h.shape
    dff = w_out_layers.shape[1]
    tm, tf = FFN_TM, FFN_TF
    assert t % tm == 0 and seq % tm == 0 and dff % tf == 0
    nj = dff // tf

    def up_tile(j):
        return jnp.minimum(j, nj - 1)

    return pl.pallas_call(
        functools.partial(_ffn_kernel, tiles_per_seq=seq // tm, nj=nj),
        grid=(t // tm, nj + 1),
        in_specs=[pl.BlockSpec((tm, d), lambda i, j: (i, 0)),
                  pl.BlockSpec((1, d), lambda i, j: (0, 0)),
                  pl.BlockSpec((None, d, tf), lambda i, j: (layer, 0, up_tile(j))),
                  pl.BlockSpec((None, d, tf), lambda i, j: (layer, 0, nj + up_tile(j))),
                  pl.BlockSpec((CONV_WIDTH, tf), lambda i, j: (0, up_tile(j))),
                  pl.BlockSpec((CONV_WIDTH, tf), lambda i, j: (0, nj + up_tile(j))),
                  pl.BlockSpec((1, tf), lambda i, j: (0, up_tile(j))),
                  pl.BlockSpec((1, tf), lambda i, j: (0, nj + up_tile(j))),
                  pl.BlockSpec((None, tf, d), lambda i, j: (layer, jnp.maximum(j - 1, 0), 0))],
        out_specs=pl.BlockSpec((tm, d), lambda i, j: (i, 0)),
        out_shape=jax.ShapeDtypeStruct((t, d), F32),
        scratch_shapes=[pltpu.VMEM((tm, d), BF16),
                        pltpu.VMEM((2, tm, tf), BF16),
                        pltpu.VMEM((nj, 8, tf), F32),
                        pltpu.VMEM((nj, 8, tf), F32)],
        compiler_params=_cparams("arbitrary", "arbitrary"),
        name="conv_ffn",
    )(h, g, w_in_layers, w_in_layers, conv_w, conv_w, conv_b, conv_b, w_out_layers)


def _row(v):
    return v.reshape(1, -1).astype(F32)


def kernel(x, rel_table, attn_norm_g, ffn_norm_g, da_w_qkv, da_w_o, da_q_gain, da_k_gain, da_lam_q1, da_lam_k1, da_lam_q2, da_lam_k2, da_sub_gain, sw_w_qkv, sw_w_o, sw_q_gain, sw_k_gain, sw_sinks, ffn_w_in, ffn_conv_w, ffn_conv_b, ffn_w_out):
    batch, seq, d = x.shape
    depth = attn_norm_g.shape[0]
    t = batch * seq
    h = x.reshape(t, d)
    q_scale = HEAD_DIM ** -0.5

    da_bias = _da_bias(rel_table, DA_T)
    sw_bias = _sw_bias(rel_table, SW_TQ)

    da_qk = 2 * DA_HEADS * HEAD_DIM
    da_flag = jnp.concatenate([jnp.ones((1, 2 * da_qk), F32), jnp.zeros((1, DA_HEADS * DA_VDIM), F32)], axis=1)
    sw_qw = SW_Q_HEADS * HEAD_DIM
    sw_kw = SW_KV_HEADS * HEAD_DIM
    sw_flag = jnp.concatenate([jnp.ones((1, sw_qw + sw_kw), F32), jnp.zeros((1, sw_kw), F32)], axis=1)

    da_w_qkv, da_w_o, sw_w_qkv, sw_w_o, ffn_w_in, ffn_w_out = (
        w.astype(BF16) for w in (da_w_qkv, da_w_o, sw_w_qkv, sw_w_o, ffn_w_in, ffn_w_out))

    for i in range(depth):
        j = i // N_MIXERS
        if i % N_MIXERS == 0:
            gain = jnp.concatenate([jnp.tile(da_q_gain[j], da_qk // HEAD_DIM) * (q_scale * LOG2E),
                                    jnp.tile(da_k_gain[j], da_qk // HEAD_DIM),
                                    jnp.ones((DA_HEADS * DA_VDIM,), F32)])
            qkv, vt = _norm_proj(h, _row(attn_norm_g[i]), da_w_qkv, j, da_flag, _row(gain), PROJ_TN_DA,
                                 v_layout=(2 * da_qk, batch, seq, DA_T))
            score_bound = (HEAD_DIM * q_scale * jnp.max(jnp.abs(da_q_gain[j])) * jnp.max(jnp.abs(da_k_gain[j]))
                           + jnp.max(jnp.abs(rel_table - rel_table[NUM_BUCKETS - 1]))) * LOG2E
            fast = (score_bound <= FAST_BOUND).astype(jnp.int32).reshape(1)
            lamv = jnp.stack([da_lam_q1[j], da_lam_k1[j], da_lam_q2[j], da_lam_k2[j]]).astype(F32)
            lam_init = 0.8 - 0.6 * math.exp(-0.3 * i)
            a = _da_attn(fast, qkv, vt, da_bias, lamv, da_sub_gain[j].reshape(DA_VDIM, 1).astype(F32), batch, seq,
                         lam_init)
            h = _out_proj(a, da_w_o, j, h)
        else:
            gain = jnp.concatenate([jnp.tile(sw_q_gain[j], SW_Q_HEADS) * (q_scale * LOG2E),
                                    jnp.tile(sw_k_gain[j], SW_KV_HEADS),
                                    jnp.ones((sw_kw,), F32)])
            qkv = _norm_proj(h, _row(attn_norm_g[i]), sw_w_qkv, j, sw_flag, _row(gain), PROJ_TN_SW)
            k = qkv[:, sw_qw:sw_qw + sw_kw].reshape(t, SW_KV_HEADS, 1, HEAD_DIM)
            kdup = jnp.broadcast_to(k, (t, SW_KV_HEADS, 2, HEAD_DIM)).reshape(t, SW_KV_HEADS * LANES)
            vt = qkv[:, sw_qw + sw_kw:].reshape(batch, seq, SW_KV_HEADS, HEAD_DIM).transpose(0, 2, 3, 1)
            ones_rows = jnp.zeros((SW_VROWS - HEAD_DIM, seq), BF16).at[0].set(1.0)
            vt = jnp.concatenate([vt, jnp.broadcast_to(ones_rows, vt.shape[:2] + ones_rows.shape)], axis=2)
            score_bound = (HEAD_DIM * q_scale * jnp.max(jnp.abs(sw_q_gain[j])) * jnp.max(jnp.abs(sw_k_gain[j]))
                           + jnp.max(jnp.abs(rel_table)) + jnp.max(jnp.abs(sw_sinks[j]))) * LOG2E
            fast = (score_bound <= FAST_BOUND).astype(jnp.int32).reshape(1)
            a = _sw_attn(fast, qkv, kdup, vt, sw_bias, sw_sinks[j].astype(F32), batch, seq)
            h = _out_proj(a, sw_w_o, j, h)
        h = _ffn(h, _row(ffn_norm_g[i]), ffn_w_in, i, ffn_conv_w[i].astype(F32), _row(ffn_conv_b[i]), ffn_w_out, seq)
    return h.reshape(batch, seq, d)
```

```python
import functools
import math

import numpy as np
import jax
import jax.numpy as jnp
from jax import lax
from jax.experimental import pallas as pl
from jax.experimental.pallas import tpu as pltpu

F32 = jnp.float32
BF16 = jnp.bfloat16

HEAD_DIM = 64
N_MAPS = 32
DA_HEADS = 16
DA_VDIM = 128
SW_Q_HEADS = 32
SW_KV_HEADS = 4
SW_GROUP = SW_Q_HEADS // SW_KV_HEADS
SW_WINDOW = 128
NUM_BUCKETS = 32
MAX_DISTANCE = 128
CONV_WIDTH = 3
EPS = 1e-6
N_MIXERS = 2

LANES = 128
NEG = -1e30
LOG2E = math.log2(math.e)
VMEM_LIMIT = 56 * 1024 * 1024

PROJ_TM = 1024
PROJ_TN_DA = 1024
PROJ_TN_SW = 1280
PROJ_ROWS = 256
OUT_TM = 1024
OUT_TN = 1024
FFN_TM = 512
FFN_TF = 512
DA_T = 512
FAST_BOUND = 60.0
SW_TQ = 256
SW_VROWS = HEAD_DIM + 16


def _cparams(*sem):
    return pltpu.CompilerParams(dimension_semantics=sem, vmem_limit_bytes=VMEM_LIMIT)


def _t5_bucket_np(dist):
    max_exact = NUM_BUCKETS // 2
    d = np.maximum(dist, 0)
    df = np.maximum(d, 1).astype(np.float64)
    large = max_exact + (np.log(df / max_exact) / math.log(MAX_DISTANCE / max_exact)
                         * (NUM_BUCKETS - max_exact)).astype(np.int32)
    large = np.minimum(large, NUM_BUCKETS - 1)
    return np.where(d < max_exact, d, large).astype(np.int32)


def _da_bias_kernel(tab_ref, bucket_ref, out_ref, *, t):
    h = pl.program_id(0)
    s = MAX_DISTANCE
    for off in range(2):
        for m in range(2):
            col = 2 * h + m
            far = tab_ref[NUM_BUCKETS - 1, col]
            for kr in range(t // s):
                for qc in range(t // s):
                    base = off * t + (qc - kr) * s
                    dst = (0, off, slice(kr * s, (kr + 1) * s), slice(m * t + qc * s, m * t + (qc + 1) * s))
                    if base - (s - 1) >= MAX_DISTANCE:
                        out_ref[dst] = jnp.zeros((s, s), F32)
                    elif base + (s - 1) < 0:
                        out_ref[dst] = jnp.full((s, s), NEG, F32)
                    else:
                        bk = bucket_ref[off, kr * s:(kr + 1) * s, qc * s:(qc + 1) * s]
                        acc = jnp.zeros(bk.shape, F32)
                        for b in range(NUM_BUCKETS - 1):
                            acc = jnp.where(bk == b, (tab_ref[b, col] - far) * LOG2E, acc)
                        out_ref[dst] = jnp.where(bk < 0, NEG, acc)


def _da_bias(rel_table, t):
    key = np.arange(t)[:, None]
    qry = np.arange(t)[None, :]
    d0 = qry - key
    d1 = t + qry - key
    bucket = np.stack([np.where(d0 >= 0, _t5_bucket_np(d0), -1), _t5_bucket_np(d1)]).astype(np.int32)
    return pl.pallas_call(
        functools.partial(_da_bias_kernel, t=t),
        grid=(DA_HEADS,),
        in_specs=[pl.BlockSpec(memory_space=pltpu.SMEM),
                  pl.BlockSpec((2, t, t), lambda h: (0, 0, 0))],
        out_specs=pl.BlockSpec((1, 2, t, 2 * t), lambda h: (h, 0, 0, 0)),
        out_shape=jax.ShapeDtypeStruct((DA_HEADS, 2, t, 2 * t), F32),
        compiler_params=_cparams("arbitrary"),
        name="da_bias",
    )(rel_table, jnp.asarray(bucket))


def _sw_bias_kernel(tab_ref, bucket_ref, out_ref, *, tq):
    p = pl.program_id(1)
    bk = bucket_ref[0]
    for m in range(2):
        col = 2 * p + m
        acc = jnp.zeros(bk.shape, F32)
        for b in range(NUM_BUCKETS):
            acc = jnp.where(bk == b, tab_ref[b, col] * LOG2E, acc)
        out_ref[0, 0, :, m * tq:(m + 1) * tq] = jnp.where(bk < 0, NEG, acc)


def _sw_bias(rel_table, tq):
    band = SW_WINDOW + tq
    key = np.arange(band)[:, None]
    qry = np.arange(tq)[None, :]
    dist = qry + SW_WINDOW - key
    bucket = np.where((dist >= 0) & (dist < SW_WINDOW), _t5_bucket_np(dist), -1).astype(np.int32)
    first = np.where(key < SW_WINDOW, -1, bucket).astype(np.int32)
    return pl.pallas_call(
        functools.partial(_sw_bias_kernel, tq=tq),
        grid=(2, SW_Q_HEADS // 2),
        in_specs=[pl.BlockSpec(memory_space=pltpu.SMEM),
                  pl.BlockSpec((1, band, tq), lambda v, p: (v, 0, 0))],
        out_specs=pl.BlockSpec((1, 1, band, 2 * tq), lambda v, p: (v, p, 0, 0)),
        out_shape=jax.ShapeDtypeStruct((2, SW_Q_HEADS // 2, band, 2 * tq), F32),
        compiler_params=_cparams("arbitrary", "arbitrary"),
        name="sw_bias",
    )(rel_table, jnp.asarray(np.stack([bucket, first])))


def _rms_rows(x, g):
    ms = jnp.mean(x * x, axis=-1, keepdims=True)
    return x * lax.rsqrt(ms + EPS) * g


def _norm_proj_kernel(x_ref, g_ref, w_ref, flag_ref, gain_ref, o_ref, *rest, first_v_tile, v_unit):
    if first_v_tile is None:
        xn_ref, y0_ref, y1_ref = rest
    else:
        vt_ref, xn_ref, y0_ref, y1_ref = rest
    i = pl.program_id(0)
    j = pl.program_id(1)
    tm, tn = y0_ref.shape

    @pl.when((i == 0) & (j == 0))
    def _():
        y1_ref[...] = jnp.zeros(y1_ref.shape, F32)

    @pl.when(j == 0)
    def _():
        xn_ref[...] = _rms_rows(x_ref[...], g_ref[...]).astype(BF16)

    def run(parity):
        y_cur, y_prev = (y0_ref, y1_ref)[::1 - 2 * parity]
        lane = lax.broadcasted_iota(jnp.int32, (1, LANES), 1)
        low = lane < HEAD_DIM
        for r in range(tm // PROJ_ROWS):
            rows = slice(r * PROJ_ROWS, (r + 1) * PROJ_ROWS)
            for c in range(tn // LANES):
                sl = slice(c * LANES, (c + 1) * LANES)
                yc = y_prev[rows, sl]
                y2 = yc * yc
                ms_lo = jnp.sum(jnp.where(low, y2, 0.0), axis=-1, keepdims=True) * (1.0 / HEAD_DIM)
                ms_hi = jnp.sum(jnp.where(low, 0.0, y2), axis=-1, keepdims=True) * (1.0 / HEAD_DIM)
                inv = jnp.where(low, lax.rsqrt(ms_lo + EPS), lax.rsqrt(ms_hi + EPS))
                scale = jnp.where(flag_ref[:, sl] > 0.0, inv, 1.0) * gain_ref[:, sl]
                o_ref[rows, sl] = (yc * scale).astype(BF16)
            y_cur[rows, :] = jnp.dot(xn_ref[rows, :], w_ref[...], preferred_element_type=F32)

    def transpose_values(parity):
        y_prev = (y1_ref, y0_ref)[parity]
        for hh in range(tn // LANES):
            for uu in range(tm // v_unit):
                blk = y_prev[uu * v_unit:(uu + 1) * v_unit, hh * LANES:(hh + 1) * LANES]
                vt_ref[hh, uu] = blk.T.astype(BF16)

    even = (j % 2) == 0
    pl.when(even)(functools.partial(run, 0))
    pl.when(jnp.logical_not(even))(functools.partial(run, 1))
    if first_v_tile is not None:
        is_v = j > first_v_tile
        pl.when(is_v & even)(functools.partial(transpose_values, 0))
        pl.when(is_v & jnp.logical_not(even))(functools.partial(transpose_values, 1))


def _norm_proj(x, g, w_layers, layer, flag, gain, tn, v_layout=None):
    t, d = x.shape
    n = w_layers.shape[2]
    tm = PROJ_TM
    assert t % tm == 0 and n % tn == 0 and tm % PROJ_ROWS == 0
    nj = n // tn

    def tile(j):
        return jnp.clip(j, 0, nj - 1)

    out_specs = pl.BlockSpec((tm, tn), lambda i, j: (i, tile(j - 1)))
    out_shape = jax.ShapeDtypeStruct((t, n), BF16)
    first_v_tile = v_unit = None
    if v_layout is not None:
        v_col, batch, seq, v_unit = v_layout
        assert v_col % tn == 0 and seq % tm == 0 and tm % v_unit == 0
        first_v_tile = v_col // tn
        heads = tn // LANES
        tiles_per_seq = seq // tm
        out_specs = [out_specs,
                     pl.BlockSpec((None, heads, tm // v_unit, LANES, v_unit),
                                  lambda i, j: (i // tiles_per_seq, jnp.clip(j - 1, first_v_tile, nj - 1) - first_v_tile,
                                                i % tiles_per_seq, 0, 0))]
        out_shape = [out_shape,
                     jax.ShapeDtypeStruct((batch, (n - v_col) // LANES, seq // v_unit, LANES, v_unit), BF16)]

    return pl.pallas_call(
        functools.partial(_norm_proj_kernel, first_v_tile=first_v_tile, v_unit=v_unit),
        grid=(t // tm, nj + 1),
        in_specs=[pl.BlockSpec((tm, d), lambda i, j: (i, 0)),
                  pl.BlockSpec((1, d), lambda i, j: (0, 0)),
                  pl.BlockSpec((None, d, tn), lambda i, j: (layer, 0, tile(j))),
                  pl.BlockSpec((1, tn), lambda i, j: (0, tile(j - 1))),
                  pl.BlockSpec((1, tn), lambda i, j: (0, tile(j - 1)))],
        out_specs=out_specs,
        out_shape=out_shape,
        scratch_shapes=[pltpu.VMEM((tm, d), BF16), pltpu.VMEM((tm, tn), F32), pltpu.VMEM((tm, tn), F32)],
        compiler_params=_cparams("arbitrary", "arbitrary"),
        name="norm_proj",
    )(x, g, w_layers, flag, gain)


def _out_proj_kernel(a_ref, w_ref, h_ref, o_ref):
    o_ref[...] = h_ref[...] + jnp.dot(a_ref[...], w_ref[...], preferred_element_type=F32)


def _out_proj(a, w_layers, layer, h):
    t, k = a.shape
    n = w_layers.shape[2]
    tm, tn = OUT_TM, OUT_TN
    assert t % tm == 0 and n % tn == 0
    return pl.pallas_call(
        _out_proj_kernel,
        grid=(t // tm, n // tn),
        in_specs=[pl.BlockSpec((tm, k), lambda i, j: (i, 0)),
                  pl.BlockSpec((None, k, tn), lambda i, j: (layer, 0, j)),
                  pl.BlockSpec((tm, tn), lambda i, j: (i, j))],
        out_specs=pl.BlockSpec((tm, tn), lambda i, j: (i, j)),
        out_shape=jax.ShapeDtypeStruct((t, n), F32),
        compiler_params=_cparams("arbitrary", "arbitrary"),
        name="out_proj",
    )(a, w_layers, h)


def _stack_maps(q):
    lane = lax.broadcasted_iota(jnp.int32, q.shape, 1)
    qf = q.astype(F32)
    return jnp.concatenate([jnp.where(lane < HEAD_DIM, qf, 0.0), jnp.where(lane < HEAD_DIM, 0.0, qf)],
                           axis=0).astype(BF16)


def _da_attn_kernel(fast_ref, q_ref, k_ref, vt_ref, bias_ref, lamv_ref, sg_ref, o_ref, acc_ref, l_ref, m_ref, *, t,
                    lam_init):
    i = pl.program_id(2)
    qt = q_ref[...].astype(F32).T
    row = lax.broadcasted_iota(jnp.int32, qt.shape, 0)
    qqt = jnp.concatenate([jnp.where(row < HEAD_DIM, qt, 0.0), jnp.where(row < HEAD_DIM, 0.0, qt)],
                          axis=1).astype(BF16)
    acc_ref[...] = jnp.zeros(acc_ref.shape, F32)
    l_ref[...] = jnp.zeros(l_ref.shape, F32)

    def scores(u, bias):
        k = k_ref[pl.ds(pl.multiple_of(u * t, t), t), :]
        st = jnp.dot(k, qqt, preferred_element_type=F32)
        return st if bias is None else st + bias

    def chunk(u0, biases):
        pv = None
        ls = None
        for n, bias in enumerate(biases):
            p = jnp.exp2(scores(u0 + n, bias))
            s = jnp.sum(p, axis=0, keepdims=True)
            d = jnp.dot(vt_ref[u0 + n], p.astype(BF16), preferred_element_type=F32)
            pv, ls = (d, s) if pv is None else (pv + d, ls + s)
        acc_ref[...] += pv
        l_ref[...] += ls

    def unit_running_max(u, bias):
        st = scores(u, bias)
        m_prev = m_ref[...]
        m_new = jnp.maximum(m_prev, jnp.max(st, axis=0, keepdims=True))
        alpha = jnp.exp2(m_prev - m_new)
        p = jnp.exp2(st - m_new)
        l_ref[...] = alpha * l_ref[...] + jnp.sum(p, axis=0, keepdims=True)
        acc_ref[...] = alpha * acc_ref[...] + jnp.dot(vt_ref[u], p.astype(BF16), preferred_element_type=F32)
        m_ref[...] = m_new

    fast = fast_ref[0] == 1
    odd = (i % 2) == 1

    far_pairs = jnp.maximum(i - 1, 0) // 2

    @pl.when(fast)
    def _():
        def far_quad(c, carry):
            chunk(4 * c, (None, None, None, None))
            return carry

        lax.fori_loop(0, far_pairs // 2, far_quad, 0)

    @pl.when(fast & ((far_pairs % 2) == 1))
    def _():
        chunk(2 * (far_pairs - 1), (None, None))

    @pl.when(fast & (i == 0))
    def _():
        chunk(0, (bias_ref[0, 0],))

    @pl.when(fast & odd)
    def _():
        chunk(i - 1, (bias_ref[0, 1], bias_ref[0, 0]))

    @pl.when(fast & jnp.logical_not(odd) & (i > 0))
    def _():
        chunk(i - 2, (None, bias_ref[0, 1], bias_ref[0, 0]))

    @pl.when(jnp.logical_not(fast))
    def _():
        m_ref[...] = jnp.full(m_ref.shape, NEG, F32)

        def far_unit(u, carry):
            unit_running_max(u, None)
            return carry

        lax.fori_loop(0, jnp.maximum(i - 1, 0), far_unit, 0)

        @pl.when(i >= 1)
        def _():
            unit_running_max(i - 1, bias_ref[0, 1])

        unit_running_max(i, bias_ref[0, 0])

    lv = lamv_ref[...]
    lam = (jnp.exp(jnp.sum(lv[0:1] * lv[1:2], axis=-1, keepdims=True))
           - jnp.exp(jnp.sum(lv[2:3] * lv[3:4], axis=-1, keepdims=True)) + lam_init)
    o = acc_ref[...] * (1.0 / l_ref[...])
    ot = o[:, :t] - lam * o[:, t:]
    ms = jnp.mean(ot * ot, axis=0, keepdims=True)
    ot = ot * lax.rsqrt(ms + EPS) * (sg_ref[...] * (1.0 - lam_init))
    o_ref[...] = ot.T.astype(BF16)


def _da_attn(fast, qkv, vt, bias, lamv, sub_gain, batch, seq, lam_init):
    t = DA_T
    nq = seq // t
    return pl.pallas_call(
        functools.partial(_da_attn_kernel, t=t, lam_init=lam_init),
        grid=(batch, DA_HEADS, nq),
        in_specs=[pl.BlockSpec(memory_space=pltpu.SMEM),
                  pl.BlockSpec((t, LANES), lambda b, h, i: (b * nq + i, h)),
                  pl.BlockSpec((seq, LANES), lambda b, h, i: (b, DA_HEADS + h)),
                  pl.BlockSpec((None, None, nq, DA_VDIM, t), lambda b, h, i: (b, h, 0, 0, 0)),
                  pl.BlockSpec((1, 2, t, 2 * t), lambda b, h, i: (h, 0, 0, 0)),
                  pl.BlockSpec((4, HEAD_DIM), lambda b, h, i: (0, 0)),
                  pl.BlockSpec((DA_VDIM, 1), lambda b, h, i: (0, 0))],
        out_specs=pl.BlockSpec((t, LANES), lambda b, h, i: (b * nq + i, h)),
        out_shape=jax.ShapeDtypeStruct((batch * seq, DA_HEADS * DA_VDIM), BF16),
        scratch_shapes=[pltpu.VMEM((DA_VDIM, 2 * t), F32), pltpu.VMEM((1, 2 * t), F32), pltpu.VMEM((1, 2 * t), F32)],
        compiler_params=_cparams("arbitrary", "arbitrary", "arbitrary"),
        name="da_attn",
    )(fast, qkv, qkv, vt, bias, lamv, sub_gain)


def _sw_attn_kernel(fast_ref, sink_ref, q_ref, kp_ref, kc_ref, vp_ref, vc_ref, bias_ref, o_ref, *, tq):
    g = pl.program_id(0)
    kb = jnp.concatenate([kp_ref[...], kc_ref[...]], axis=0)
    vtb = jnp.concatenate([vp_ref[...], vc_ref[...]], axis=1)
    col = lax.broadcasted_iota(jnp.int32, (1, 2 * tq), 1)

    def lane_slice(pp):
        return slice(pp * LANES, (pp + 1) * LANES)

    def scores(pp):
        qt = q_ref[:, lane_slice(pp)].astype(F32).T
        row = lax.broadcasted_iota(jnp.int32, (LANES, 1), 0)
        at_key = (row // HEAD_DIM) == (g % 2)
        qqt = jnp.concatenate(
            [jnp.where(at_key, jnp.concatenate([qt[:HEAD_DIM], qt[:HEAD_DIM]], axis=0), 0.0),
             jnp.where(at_key, jnp.concatenate([qt[HEAD_DIM:], qt[HEAD_DIM:]], axis=0), 0.0)],
            axis=1).astype(BF16)
        return jnp.dot(kb, qqt, preferred_element_type=F32) + bias_ref[0, pp]

    def weighted_values(pp, st, fast):
        head = g * SW_GROUP + 2 * pp
        sink = jnp.where(col < tq, sink_ref[head], sink_ref[head + 1]) * LOG2E
        if fast:
            ot = jnp.dot(vtb, jnp.exp2(st).astype(BF16), preferred_element_type=F32)
            return ot, jnp.exp2(sink)
        m = jnp.maximum(jnp.max(st, axis=0, keepdims=True), sink)
        ot = jnp.dot(vtb, jnp.exp2(st - m).astype(BF16), preferred_element_type=F32)
        return ot, jnp.exp2(sink - m)

    def finish(pp, ot, sink_term):
        ot = ot[:HEAD_DIM] * (1.0 / (ot[HEAD_DIM:HEAD_DIM + 1] + sink_term))
        o_ref[:, lane_slice(pp)] = jnp.concatenate([ot[:, :tq], ot[:, tq:]], axis=0).T.astype(BF16)

    def all_pairs(fast):
        pairs = range(SW_GROUP // 2)
        sts = [scores(pp) for pp in pairs]
        ots = [weighted_values(pp, sts[pp], fast) for pp in pairs]
        for pp in pairs:
            finish(pp, *ots[pp])

    pl.when(fast_ref[0] == 1)(functools.partial(all_pairs, True))
    pl.when(fast_ref[0] != 1)(functools.partial(all_pairs, False))


def _sw_attn(fast, qkv, vt, bias, sinks, batch, seq):
    tq = SW_TQ
    nq = seq // tq
    r = tq // SW_WINDOW
    band = SW_WINDOW + tq
    gw = SW_GROUP * HEAD_DIM
    kblk = SW_Q_HEADS * HEAD_DIM // LANES
    return pl.pallas_call(
        functools.partial(_sw_attn_kernel, tq=tq),
        grid=(SW_KV_HEADS, batch, nq),
        in_specs=[pl.BlockSpec(memory_space=pltpu.SMEM),
                  pl.BlockSpec(memory_space=pltpu.SMEM),
                  pl.BlockSpec((tq, gw), lambda g, b, i: (b * nq + i, g)),
                  pl.BlockSpec((SW_WINDOW, LANES), lambda g, b, i: (jnp.maximum((b * nq + i) * r - 1, 0), kblk + g // 2)),
                  pl.BlockSpec((tq, LANES), lambda g, b, i: (b * nq + i, kblk + g // 2)),
                  pl.BlockSpec((None, None, SW_VROWS, SW_WINDOW), lambda g, b, i: (b, g, 0, jnp.maximum(i * r - 1, 0))),
                  pl.BlockSpec((None, None, SW_VROWS, tq), lambda g, b, i: (b, g, 0, i)),
                  pl.BlockSpec((1, SW_GROUP // 2, band, 2 * tq), lambda g, b, i: (jnp.where(i == 0, 1, 0), g, 0, 0))],
        out_specs=pl.BlockSpec((tq, gw), lambda g, b, i: (b * nq + i, g)),
        out_shape=jax.ShapeDtypeStruct((batch * seq, SW_Q_HEADS * HEAD_DIM), BF16),
        compiler_params=_cparams("arbitrary", "arbitrary", "arbitrary"),
        name="sw_attn",
    )(fast, sinks, qkv, qkv, qkv, vt, vt, bias)


def _causal_conv(u, halo, cw_ref, cb_ref):
    row = lax.broadcasted_iota(jnp.int32, (8, 1), 0)
    r1 = pltpu.roll(u, 1, axis=0)
    r2 = pltpu.roll(u, 2, axis=0)
    u1 = jnp.concatenate([jnp.where(row == 0, halo[7:8], r1[:8]), r1[8:]], axis=0)
    u2 = jnp.concatenate([jnp.where(row == 0, halo[6:7], jnp.where(row == 1, halo[7:8], r2[:8])), r2[8:]], axis=0)
    return cw_ref[0:1, :] * u2 + cw_ref[1:2, :] * u1 + cw_ref[2:3, :] * u + cb_ref[...]


def _ffn_kernel(h_ref, g_ref, wg_ref, wu_ref, cwg_ref, cwu_ref, cbg_ref, cbu_ref, wo_ref, o_ref,
                f_ref, act_ref, halo_g_ref, halo_u_ref, *, tiles_per_seq, nj):
    i = pl.program_id(0)
    j = pl.program_id(1)
    tm = h_ref.shape[0]

    @pl.when(j == 0)
    def _():
        x = h_ref[...]
        f_ref[...] = _rms_rows(x, g_ref[...]).astype(BF16)
        o_ref[...] = x

    @pl.when(((i % tiles_per_seq) == 0) & (j < nj))
    def _():
        halo_g_ref[j] = jnp.zeros(halo_g_ref.shape[1:], F32)
        halo_u_ref[j] = jnp.zeros(halo_u_ref.shape[1:], F32)

    def run(do_up, do_down):
        if do_up:
            f = f_ref[...]
            ug = jnp.dot(f, wg_ref[...], preferred_element_type=F32)
            uu = jnp.dot(f, wu_ref[...], preferred_element_type=F32)
        if do_down:
            o_ref[...] += jnp.dot(act_ref[(j + 1) % 2], wo_ref[...], preferred_element_type=F32)
        if do_up:
            def conv(u, cw_ref, cb_ref, halo_ref):
                halo = halo_ref[j]
                halo_ref[j] = u[tm - 8:, :]
                return _causal_conv(u, halo, cw_ref, cb_ref)

            gate = conv(ug, cwg_ref, cbg_ref, halo_g_ref)
            up = conv(uu, cwu_ref, cbu_ref, halo_u_ref)
            act_ref[j % 2] = (gate * (1.0 / (1.0 + jnp.exp(-gate))) * up).astype(BF16)

    pl.when(j == 0)(functools.partial(run, True, False))
    pl.when((j > 0) & (j < nj))(functools.partial(run, True, True))
    pl.when(j == nj)(functools.partial(run, False, True))


def _ffn(h, g, w_in_layers, layer, conv_w, conv_b, w_out_layers, seq):
    t, d = h.shape
    dff = w_out_layers.shape[1]
    tm, tf = FFN_TM, FFN_TF
    assert t % tm == 0 and seq % tm == 0 and dff % tf == 0
    nj = dff // tf

    def up_tile(j):
        return jnp.minimum(j, nj - 1)

    return pl.pallas_call(
        functools.partial(_ffn_kernel, tiles_per_seq=seq // tm, nj=nj),
        grid=(t // tm, nj + 1),
        in_specs=[pl.BlockSpec((tm, d), lambda i, j: (i, 0)),
                  pl.BlockSpec((1, d), lambda i, j: (0, 0)),
                  pl.BlockSpec((None, d, tf), lambda i, j: (layer, 0, up_tile(j))),
                  pl.BlockSpec((None, d, tf), lambda i, j: (layer, 0, nj + up_tile(j))),
                  pl.BlockSpec((CONV_WIDTH, tf), lambda i, j: (0, up_tile(j))),
                  pl.BlockSpec((CONV_WIDTH, tf), lambda i, j: (0, nj + up_tile(j))),
                  pl.BlockSpec((1, tf), lambda i, j: (0, up_tile(j))),
                  pl.BlockSpec((1, tf), lambda i, j: (0, nj + up_tile(j))),
                  pl.BlockSpec((None, tf, d), lambda i, j: (layer, jnp.maximum(j - 1, 0), 0))],
        out_specs=pl.BlockSpec((tm, d), lambda i, j: (i, 0)),
        out_shape=jax.ShapeDtypeStruct((t, d), F32),
        scratch_shapes=[pltpu.VMEM((tm, d), BF16),
                        pltpu.VMEM((2, tm, tf), BF16),
                        pltpu.VMEM((nj, 8, tf), F32),
                        pltpu.VMEM((nj, 8, tf), F32)],
        compiler_params=_cparams("arbitrary", "arbitrary"),
        name="conv_ffn",
    )(h, g, w_in_layers, w_in_layers, conv_w, conv_w, conv_b, conv_b, w_out_layers)


def _row(v):
    return v.reshape(1, -1).astype(F32)


def kernel(x, rel_table, attn_norm_g, ffn_norm_g, da_w_qkv, da_w_o, da_q_gain, da_k_gain, da_lam_q1, da_lam_k1, da_lam_q2, da_lam_k2, da_sub_gain, sw_w_qkv, sw_w_o, sw_q_gain, sw_k_gain, sw_sinks, ffn_w_in, ffn_conv_w, ffn_conv_b, ffn_w_out):
    batch, seq, d = x.shape
    depth = attn_norm_g.shape[0]
    t = batch * seq
    h = x.reshape(t, d)
    q_scale = HEAD_DIM ** -0.5

    da_bias = _da_bias(rel_table, DA_T)
    sw_bias = _sw_bias(rel_table, SW_TQ)

    da_qk = 2 * DA_HEADS * HEAD_DIM
    da_flag = jnp.concatenate([jnp.ones((1, 2 * da_qk), F32), jnp.zeros((1, DA_HEADS * DA_VDIM), F32)], axis=1)
    sw_qw = SW_Q_HEADS * HEAD_DIM
    sw_kw = SW_KV_HEADS * HEAD_DIM
    sw_flag = jnp.concatenate([jnp.ones((1, sw_qw + sw_kw), F32), jnp.zeros((1, sw_kw), F32)], axis=1)

    da_w_qkv, da_w_o, sw_w_qkv, sw_w_o, ffn_w_in, ffn_w_out = (
        w.astype(BF16) for w in (da_w_qkv, da_w_o, sw_w_qkv, sw_w_o, ffn_w_in, ffn_w_out))

    for i in range(depth):
        j = i // N_MIXERS
        if i % N_MIXERS == 0:
            gain = jnp.concatenate([jnp.tile(da_q_gain[j], da_qk // HEAD_DIM) * (q_scale * LOG2E),
                                    jnp.tile(da_k_gain[j], da_qk // HEAD_DIM),
                                    jnp.ones((DA_HEADS * DA_VDIM,), F32)])
            qkv, vt = _norm_proj(h, _row(attn_norm_g[i]), da_w_qkv, j, da_flag, _row(gain), PROJ_TN_DA,
                                 v_layout=(2 * da_qk, batch, seq, DA_T))
            score_bound = (HEAD_DIM * q_scale * jnp.max(jnp.abs(da_q_gain[j])) * jnp.max(jnp.abs(da_k_gain[j]))
                           + jnp.max(jnp.abs(rel_table - rel_table[NUM_BUCKETS - 1]))) * LOG2E
            fast = (score_bound <= FAST_BOUND).astype(jnp.int32).reshape(1)
            lamv = jnp.stack([da_lam_q1[j], da_lam_k1[j], da_lam_q2[j], da_lam_k2[j]]).astype(F32)
            lam_init = 0.8 - 0.6 * math.exp(-0.3 * i)
            a = _da_attn(fast, qkv, vt, da_bias, lamv, da_sub_gain[j].reshape(DA_VDIM, 1).astype(F32), batch, seq,
                         lam_init)
            h = _out_proj(a, da_w_o, j, h)
        else:
            gain = jnp.concatenate([jnp.tile(sw_q_gain[j], SW_Q_HEADS) * (q_scale * LOG2E),
                                    jnp.tile(sw_k_gain[j], SW_KV_HEADS),
                                    jnp.ones((sw_kw,), F32)])
            qkv = _norm_proj(h, _row(attn_norm_g[i]), sw_w_qkv, j, sw_flag, _row(gain), PROJ_TN_SW)
            vt = qkv[:, sw_qw + sw_kw:].reshape(batch, seq, SW_KV_HEADS, HEAD_DIM).transpose(0, 2, 3, 1)
            ones_rows = jnp.zeros((SW_VROWS - HEAD_DIM, seq), BF16).at[0].set(1.0)
            vt = jnp.concatenate([vt, jnp.broadcast_to(ones_rows, vt.shape[:2] + ones_rows.shape)], axis=2)
            score_bound = (HEAD_DIM * q_scale * jnp.max(jnp.abs(sw_q_gain[j])) * jnp.max(jnp.abs(sw_k_gain[j]))
                           + jnp.max(jnp.abs(rel_table)) + jnp.max(jnp.abs(sw_sinks[j]))) * LOG2E
            fast = (score_bound <= FAST_BOUND).astype(jnp.int32).reshape(1)
            a = _sw_attn(fast, qkv, vt, sw_bias, sw_sinks[j].astype(F32), batch, seq)
            h = _out_proj(a, sw_w_o, j, h)
        h = _ffn(h, _row(ffn_norm_g[i]), ffn_w_in, i, ffn_conv_w[i].astype(F32), _row(ffn_conv_b[i]), ffn_w_out, seq)
    return h.reshape(batch, seq, d)
```
